```python
import math
import jax, jax.numpy as jnp
from jax import lax
import numpy as np

D_MODEL = 1024
BATCH = 8
SEQ = 2048
DEPTH = 2

N_BRANCH = 4
BRANCH_W = D_MODEL // 2
DN_HEADS = 4
DN_DK = BRANCH_W // DN_HEADS
DN_DV = BRANCH_W // DN_HEADS
DN_CHUNK = 64
CONV_W = 4
GM_GROUPS = 4
GM_CH = BRANCH_W // GM_GROUPS
GM_CHUNK = 128
SW_HEADS = 8
SW_KV_HEADS = 2
SW_HD = BRANCH_W // SW_HEADS
SW_WINDOW = 128
SW_BLOCK = 128
MEM_LEN = 256
XM_HEADS = 4
XM_HD = BRANCH_W // XM_HEADS

EPS = 1e-6
NEG_INF = -1e30

IN_SPLITS = [
    3 * BRANCH_W,
    BRANCH_W,
    DN_HEADS,
    DN_HEADS,
    2 * BRANCH_W,
    BRANCH_W,
    SW_HEADS * SW_HD,
    SW_KV_HEADS * SW_HD,
    SW_KV_HEADS * SW_HD,
    BRANCH_W,
    XM_HEADS * XM_HD,
    BRANCH_W,
    N_BRANCH * D_MODEL,
]
D_IN = sum(IN_SPLITS)

kernel_name = "hybrid_parallel_gated_deltanet_gmlp_swa_memory"


def _rmsnorm(x, g):
    xf = x.astype(jnp.float32)
    y = xf * lax.rsqrt(jnp.mean(xf * xf, axis=-1, keepdims=True) + EPS)
    return (y * g.astype(jnp.float32)).astype(x.dtype)


def _l2norm(x):
    return x * lax.rsqrt(jnp.sum(x * x, axis=-1, keepdims=True) + EPS)


def _split_cols(cols):
    idx, acc = [], 0
    for s in IN_SPLITS[:-1]:
        acc += s
        idx.append(acc)
    return jnp.split(cols, idx, axis=-1)


def _causal_dwconv(x, w):
    c = x.shape[-1]
    return lax.conv_general_dilated(
        x, w[:, None, :].astype(x.dtype), window_strides=(1,),
        padding=[(CONV_W - 1, 0)], dimension_numbers=("NWC", "WIO", "NWC"),
        feature_group_count=c)


def _gated_delta_chunked(q, k, v, g, beta):
    B, T, H, dk = q.shape
    dv = v.shape[-1]
    C = DN_CHUNK
    n = T // C

    def blk(a):
        return jnp.moveaxis(a.reshape((B, n, C, H) + a.shape[3:]), 3, 1)

    q, k, v, g, beta = blk(q), blk(k), blk(v), blk(g), blk(beta)
    gc = jnp.cumsum(g, axis=-1)
    diff = gc[..., :, None] - gc[..., None, :]
    ii = jnp.arange(C)
    strict = ii[:, None] > ii[None, :]
    incl = ii[:, None] >= ii[None, :]
    kb = k * beta[..., None]
    L = jnp.where(strict, jnp.einsum("bhncd,bhnsd->bhncs", kb, k)
                  * jnp.exp(jnp.where(strict, diff, 0.0)), 0.0)
    eye = jnp.eye(C, dtype=q.dtype)
    rhs = jnp.concatenate([v * beta[..., None], kb * jnp.exp(gc)[..., None]], axis=-1)
    sol = lax.linalg.triangular_solve(eye + L, rhs, left_side=True, lower=True,
                                      unit_diagonal=True)
    u, w = sol[..., :dv], sol[..., dv:]
    a_qk = jnp.where(incl, jnp.einsum("bhncd,bhnsd->bhncs", q, k)
                     * jnp.exp(jnp.where(incl, diff, 0.0)), 0.0)
    g_last = gc[..., -1]
    qg = q * jnp.exp(gc)[..., None]
    kd = k * jnp.exp(g_last[..., None] - gc)[..., None]
    d_last = jnp.exp(g_last)
    xs = tuple(jnp.moveaxis(a, 2, 0) for a in (qg, kd, u, w, a_qk, d_last))

    def step(S, inp):
        qg_c, kd_c, u_c, w_c, a_c, d_c = inp
        v_new = u_c - jnp.einsum("bhck,bhkv->bhcv", w_c, S)
        o = (jnp.einsum("bhck,bhkv->bhcv", qg_c, S)
             + jnp.einsum("bhcs,bhsv->bhcv", a_c, v_new))
        S = S * d_c[..., None, None] + jnp.einsum("bhck,bhcv->bhkv", kd_c, v_new)
        return S, o

    S0 = jnp.zeros((B, H, dk, dv), jnp.float32)
    _, o = lax.scan(step, S0, xs)
    o = jnp.moveaxis(o, 0, 2)
    return jnp.moveaxis(o, 1, 3).reshape(B, T, H, dv)


def _deltanet_branch(qkv, z, b_logit, a_logit, conv_w, a_log, dt_bias, o_norm):
    B, T, _ = qkv.shape
    qkv = jax.nn.silu(_causal_dwconv(qkv, conv_w)).astype(jnp.float32)
    q, k, v = jnp.split(qkv, 3, axis=-1)
    q = _l2norm(q.reshape(B, T, DN_HEADS, DN_DK)) * (DN_DK ** -0.5)
    k = _l2norm(k.reshape(B, T, DN_HEADS, DN_DK))
    v = v.reshape(B, T, DN_HEADS, DN_DV)
    beta = jax.nn.sigmoid(b_logit.astype(jnp.float32))
    g = -jnp.exp(a_log.astype(jnp.float32)) * jax.nn.softplus(
        a_logit.astype(jnp.float32) + dt_bias.astype(jnp.float32))
    o = _gated_delta_chunked(q, k, v, g, beta)
    o = _rmsnorm(o, o_norm) * jax.nn.silu(z.astype(jnp.float32).reshape(B, T, DN_HEADS, DN_DV))
    return o.reshape(B, T, BRANCH_W).astype(z.dtype)


def _spatial_gating_branch(uv, z, v_gain, w_s, b_s):
    B, T, _ = uv.shape
    u, v = jnp.split(jax.nn.gelu(uv), 2, axis=-1)
    v = _rmsnorm(v, v_gain)
    n = T // GM_CHUNK
    vb = v.reshape(B, n, GM_CHUNK, GM_GROUPS, GM_CH)
    causal = jnp.tril(jnp.ones((GM_CHUNK, GM_CHUNK), dtype=bool))
    ws = jnp.where(causal, w_s, 0.0).astype(v.dtype)
    s = jnp.einsum("gpq,bnqgc->bnpgc", ws, vb) + b_s.T.astype(v.dtype)[None, None, :, :, None]
    return u * s.reshape(B, T, BRANCH_W) * jax.nn.silu(z)


def _sliding_window_branch(q, k, v, z, sinks):
    B, T, _ = q.shape
    P = SW_BLOCK
    n = T // P
    G = SW_HEADS // SW_KV_HEADS
    q = q.reshape(B, n, P, SW_KV_HEADS, G, SW_HD)
    k = k.reshape(B, n, P, SW_KV_HEADS, SW_HD)
    v = v.reshape(B, n, P, SW_KV_HEADS, SW_HD)
    pad = jnp.zeros_like(k[:, :1])
    kb = jnp.concatenate([jnp.concatenate([pad, k[:, :-1]], axis=1), k], axis=2)
    vb = jnp.concatenate([jnp.concatenate([pad, v[:, :-1]], axis=1), v], axis=2)
    s = jnp.einsum("bnqkgd,bnskd->bnkgqs", q, kb).astype(jnp.float32) * (SW_HD ** -0.5)
    qi = jnp.arange(P)[:, None]
    kj = jnp.arange(2 * P)[None, :]
    dist = qi + P - kj
    blk = jnp.arange(n)[:, None, None]
    valid = (dist >= 0) & (dist < SW_WINDOW) & (blk * P - P + kj >= 0)
    s = jnp.where(valid[None, :, None, None], s, NEG_INF)
    sink = jnp.broadcast_to(
        sinks.astype(jnp.float32).reshape(SW_KV_HEADS, G)[None, None, :, :, None, None],
        s.shape[:-1] + (1,))
    p = jax.nn.softmax(jnp.concatenate([s, sink], axis=-1), axis=-1)[..., :-1]
    o = jnp.einsum("bnkgqs,bnskd->bnqkgd", p.astype(vb.dtype), vb).reshape(B, T, BRANCH_W)
    return o * jax.nn.silu(z)


def _memory_branch(q, z, mem_kv):
    B, T, _ = q.shape
    q = q.reshape(B, T, XM_HEADS, XM_HD)
    mk, mv = jnp.split(mem_kv, 2, axis=-1)
    mk = mk.reshape(B, -1, XM_HEADS, XM_HD)
    mv = mv.reshape(B, -1, XM_HEADS, XM_HD)
    s = jnp.einsum("bthd,bmhd->bhtm", q, mk).astype(jnp.float32) * (XM_HD ** -0.5)
    p = jax.nn.softmax(s, axis=-1)
    o = jnp.einsum("bhtm,bmhd->bthd", p.astype(mv.dtype), mv).reshape(B, T, BRANCH_W)
    return o * jax.nn.silu(z)


def _layer(x, mem, norm_pre, norm_post, norm_mem, w_in, conv_w, a_log, dt_bias, dn_norm,
           gm_norm, spatial_w, spatial_b, sinks, w_mem_kv, w_up, w_out):
    B, T, D = x.shape
    h = _rmsnorm(x, norm_pre)
    cols = h @ w_in
    (dn_qkv, dn_z, dn_b, dn_a, gm_uv, gm_z, sw_q, sw_k, sw_v, sw_z,
     xm_q, xm_z, gate_logits) = _split_cols(cols)
    mem_kv = _rmsnorm(mem, norm_mem) @ w_mem_kv
    y_a = _deltanet_branch(dn_qkv, dn_z, dn_b, dn_a, conv_w, a_log, dt_bias, dn_norm)
    y_b = _spatial_gating_branch(gm_uv, gm_z, gm_norm, spatial_w, spatial_b)
    y_c = _sliding_window_branch(sw_q, sw_k, sw_v, sw_z, sinks)
    y_m = _memory_branch(xm_q, xm_z, mem_kv)
    ys = jnp.stack([y_a, y_b, y_c, y_m], axis=2)
    proj = jnp.einsum("btnc,ncd->btnd", ys, w_up)
    gates = jax.nn.sigmoid(gate_logits.reshape(B, T, N_BRANCH, D))
    merged = jnp.sum(gates * proj, axis=2)
    out = merged @ w_out
    return x + _rmsnorm(out, norm_post)


def setup_inputs(seed: int = 0) -> dict:
    key = jax.random.key(seed)
    ks = jax.random.split(key, 20)
    f32 = jnp.float32
    nrm = lambda k, shape, scale: jax.random.normal(k, shape, f32) * scale
    dt = jnp.exp(jax.random.uniform(ks[8], (DEPTH, DN_HEADS), f32,
                                    math.log(1e-3), math.log(1e-1)))
    return {
        "x": nrm(ks[0], (BATCH, SEQ, D_MODEL), 1.0),
        "mem": nrm(ks[1], (BATCH, MEM_LEN, D_MODEL), 1.0),
        "norm_pre": 1.0 + nrm(ks[2], (DEPTH, D_MODEL), 0.1),
        "norm_post": 1.0 + nrm(ks[3], (DEPTH, D_MODEL), 0.1),
        "norm_mem": 1.0 + nrm(ks[4], (DEPTH, D_MODEL), 0.1),
        "w_in": nrm(ks[5], (DEPTH, D_MODEL, D_IN), D_MODEL ** -0.5),
        "conv_w": nrm(ks[6], (DEPTH, CONV_W, 3 * BRANCH_W), CONV_W ** -0.5),
        "a_log": jnp.log(jax.random.uniform(ks[7], (DEPTH, DN_HEADS), f32, 1.0, 16.0)),
        "dt_bias": dt + jnp.log(-jnp.expm1(-dt)),
        "dn_norm": 1.0 + nrm(ks[9], (DEPTH, DN_DV), 0.1),
        "gm_norm": 1.0 + nrm(ks[10], (DEPTH, BRANCH_W), 0.1),
        "spatial_w": nrm(ks[11], (DEPTH, GM_GROUPS, GM_CHUNK, GM_CHUNK), GM_CHUNK ** -0.5),
        "spatial_b": 1.0 + nrm(ks[12], (DEPTH, GM_GROUPS, GM_CHUNK), 0.1),
        "sinks": nrm(ks[13], (DEPTH, SW_HEADS), 1.0),
        "w_mem_kv": nrm(ks[14], (DEPTH, D_MODEL, 2 * BRANCH_W), D_MODEL ** -0.5),
        "w_up": nrm(ks[15], (DEPTH, N_BRANCH, BRANCH_W, D_MODEL), BRANCH_W ** -0.5),
        "w_out": nrm(ks[16], (DEPTH, D_MODEL, D_MODEL), D_MODEL ** -0.5),
    }


def reference(x, mem, norm_pre, norm_post, norm_mem, w_in, conv_w, a_log, dt_bias, dn_norm,
              gm_norm, spatial_w, spatial_b, sinks, w_mem_kv, w_up, w_out):
    for l in range(DEPTH):
        x = _layer(x, mem, norm_pre[l], norm_post[l], norm_mem[l], w_in[l], conv_w[l],
                   a_log[l], dt_bias[l], dn_norm[l], gm_norm[l], spatial_w[l], spatial_b[l],
                   sinks[l], w_mem_kv[l], w_up[l], w_out[l])
    return x
```

```python
import functools

import jax
import jax.numpy as jnp
from jax import lax
from jax.experimental import pallas as pl
from jax.experimental.pallas import tpu as pltpu

F32 = jnp.float32
BF16 = jnp.bfloat16

D_MODEL = 1024
N_BRANCH = 4
BRANCH_W = 512
DN_HEADS = 4
DN_DK = 128
DN_CHUNK = 64
CONV_W = 4
GM_GROUPS = 4
GM_CHUNK = 128
SW_HEADS = 8
SW_KV_HEADS = 2
SW_HD = 64
SW_BLOCK = 128
XM_HEADS = 4
XM_HD = 128
EPS = 1e-6
NEG_INF = -1e30

GATE_COLS = N_BRANCH * D_MODEL
MAIN_COLS = GATE_COLS + 11 * BRANCH_W + 2 * SW_KV_HEADS * SW_HD
SMALL_COLS = 128
CB_AQ, CB_AK, CB_AV, CB_AZ, CB_BU, CB_BV, CB_BZ, CB_CQ, CB_CZ, CB_MQ, CB_MZ = range(8, 19)
CB_CKV_256 = (GATE_COLS + 11 * BRANCH_W) // 256

VMEM_LIMIT = 48 * 1024 * 1024


def _sigmoid(x):
    return 1.0 / (1.0 + jnp.exp(-x))


def _silu(x):
    return x * _sigmoid(x)


def _softplus(x):
    return jnp.maximum(x, 0.0) + jnp.log(1.0 + jnp.exp(-jnp.abs(x)))


def _rms(x, gain):
    return x * lax.rsqrt(jnp.mean(x * x, axis=-1, keepdims=True) + EPS) * gain


def _dot(a, b):
    return jnp.dot(a, b, preferred_element_type=F32)


def _dot_nt(a, b):
    return lax.dot_general(a, b, (((1,), (1,)), ((), ())), preferred_element_type=F32)


def _inproj_kernel(x_ref, g_ref, w_ref, ws_ref, o_ref, os_ref, h_ref):
    @pl.when(pl.program_id(1) == 0)
    def _():
        hb = _rms(x_ref[...], g_ref[...]).astype(BF16)
        h_ref[...] = hb
        os_ref[...] = _dot(hb, ws_ref[...])

    o_ref[...] = _dot(h_ref[...], w_ref[...]).astype(BF16)


def _inproj(x2, gain, w_main, w_small, *, tm, tn):
    m = x2.shape[0]
    return pl.pallas_call(
        _inproj_kernel,
        grid=(m // tm, MAIN_COLS // tn),
        in_specs=[
            pl.BlockSpec((tm, D_MODEL), lambda i, j: (i, 0)),
            pl.BlockSpec((1, D_MODEL), lambda i, j: (0, 0)),
            pl.BlockSpec((D_MODEL, tn), lambda i, j: (0, j)),
            pl.BlockSpec((D_MODEL, SMALL_COLS), lambda i, j: (0, 0)),
        ],
        out_specs=[
            pl.BlockSpec((tm, tn), lambda i, j: (i, j)),
            pl.BlockSpec((tm, SMALL_COLS), lambda i, j: (i, 0)),
        ],
        out_shape=[
            jax.ShapeDtypeStruct((m, MAIN_COLS), BF16),
            jax.ShapeDtypeStruct((m, SMALL_COLS), F32),
        ],
        scratch_shapes=[pltpu.VMEM((tm, D_MODEL), BF16)],
        compiler_params=pltpu.CompilerParams(
            dimension_semantics=("arbitrary", "arbitrary"), vmem_limit_bytes=VMEM_LIMIT),
        name="inproj",
    )(x2, gain, w_main, w_small)


def _deltanet_kernel(q_ref, k_ref, v_ref, z_ref, sm_ref, cw_ref, gp_ref, on_ref, o_ref,
                     tail_ref, s_ref, *, tt):
    H, C, DK = DN_HEADS, DN_CHUNK, DN_DK
    HC = H * C

    @pl.when(pl.program_id(1) == 0)
    def _():
        tail_ref[...] = jnp.zeros_like(tail_ref)
        s_ref[...] = jnp.zeros_like(s_ref)

    cw = cw_ref[...]

    def conv_silu(x_ref, idx):
        cur = x_ref[...].astype(F32)
        ext = jnp.concatenate([tail_ref[idx], cur], axis=0)
        lo = idx * BRANCH_W
        acc = cur * cw[CONV_W - 1:CONV_W, lo:lo + BRANCH_W]
        for j in range(1, CONV_W):
            shifted = pltpu.roll(ext, j, axis=0)[8:]
            acc = acc + shifted * cw[CONV_W - 1 - j:CONV_W - j, lo:lo + BRANCH_W]
        tail_ref[idx] = cur[tt - 8:]
        return _silu(acc)

    qc = conv_silu(q_ref, 0)
    kc = conv_silu(k_ref, 1)
    vc = conv_silu(v_ref, 2)

    def l2n(xh):
        return xh * lax.rsqrt(jnp.sum(xh * xh, axis=-1, keepdims=True) + EPS)

    r = lax.broadcasted_iota(jnp.int32, (HC, HC), 0)
    c = lax.broadcasted_iota(jnp.int32, (HC, HC), 1)
    same = (r // C) == (c // C)
    strict = jnp.logical_and(same, r > c)
    incl = jnp.logical_and(same, r >= c)
    eye = (r == c).astype(F32)
    rl = lax.broadcasted_iota(jnp.int32, (HC, H * DK), 0)
    cl = lax.broadcasted_iota(jnp.int32, (HC, H * DK), 1)
    head_mask = ((rl // C) == (cl // DK)).astype(F32)
    tr = lax.broadcasted_iota(jnp.int32, (C, C), 0)
    tc = lax.broadcasted_iota(jnp.int32, (C, C), 1)
    ltri = (tr >= tc).astype(F32)

    neg_a = gp_ref[0:1, :]
    dt_b = gp_ref[1:2, :]
    o_gain = on_ref[...]

    def stack(x, lo):
        return jnp.concatenate([x[lo:lo + C, h * DK:(h + 1) * DK] for h in range(H)], axis=0)

    def diag_blocks(m):
        return jnp.concatenate([m[h * C:(h + 1) * C, h * DK:(h + 1) * DK] for h in range(H)], axis=0)

    s_cat = s_ref[...]
    for ci in range(tt // C):
        lo = ci * C
        qs = jnp.concatenate([l2n(qc[lo:lo + C, h * DK:(h + 1) * DK]) for h in range(H)], axis=0) * (DK ** -0.5)
        ks = jnp.concatenate([l2n(kc[lo:lo + C, h * DK:(h + 1) * DK]) for h in range(H)], axis=0)
        vs = stack(vc, lo)
        sm = sm_ref[lo:lo + C, :]
        beta_all = _sigmoid(sm)
        g_all = neg_a * _softplus(sm + dt_b)
        gc_all = jnp.dot(ltri, g_all, precision=lax.Precision.HIGHEST, preferred_element_type=F32)
        cb = jnp.concatenate(
            [jnp.broadcast_to(gc_all[:, H + h:H + h + 1], (C, HC)) for h in range(H)], axis=0)
        bb = jnp.concatenate(
            [jnp.broadcast_to(beta_all[:, h:h + 1], (C, DK)) for h in range(H)], axis=0)
        diff = cb - cb.T
        cb1 = cb[:, :DK]
        egc = jnp.exp(cb1)
        glast = jnp.concatenate(
            [jnp.broadcast_to(cb1[h * C + C - 1:h * C + C, :], (C, DK)) for h in range(H)], axis=0)
        dl_row = jnp.concatenate([jnp.exp(cb1[h * C + C - 1:h * C + C, :]) for h in range(H)], axis=1)

        kb = ks * bb
        ksb = ks.astype(BF16)
        lmat = jnp.where(strict, _dot_nt(kb.astype(BF16), ksb) * jnp.exp(jnp.where(strict, diff, 0.0)), 0.0)
        nm = -lmat
        tinv = eye + nm
        for _ in range(5):
            nb = nm.astype(BF16)
            nm = _dot(nb, nb)
            tinv = tinv + _dot(tinv.astype(BF16), nm.astype(BF16))
        rhs = jnp.concatenate([vs * bb, kb * egc], axis=1)
        sol = _dot(tinv.astype(BF16), rhs.astype(BF16))
        u, w = sol[:, :DK], sol[:, DK:]
        aqk = jnp.where(incl, _dot_nt(qs.astype(BF16), ksb) * jnp.exp(jnp.where(incl, diff, 0.0)), 0.0)
        qg = qs * egc
        kd = ks * jnp.exp(glast - cb1)

        sb = s_cat.astype(BF16)
        v_new = u - diag_blocks(_dot(w.astype(BF16), sb))
        o_st = diag_blocks(_dot(qg.astype(BF16), sb)) + _dot(aqk.astype(BF16), v_new.astype(BF16))
        v_bd = jnp.concatenate([v_new] * H, axis=1) * head_mask
        s_cat = s_cat * dl_row + _dot(kd.T.astype(BF16), v_bd.astype(BF16))

        for h in range(H):
            oh = _rms(o_st[h * C:(h + 1) * C, :], o_gain)
            zh = z_ref[lo:lo + C, h * DK:(h + 1) * DK].astype(F32)
            o_ref[lo:lo + C, h * DK:(h + 1) * DK] = (oh * _silu(zh)).astype(BF16)
    s_ref[...] = s_cat


def _deltanet(cols, small, conv_w, gate_params, o_norm, *, batch, seq, tt):
    nt = seq // tt
    row = lambda b, t: b * nt + t
    cspec = lambda cb: pl.BlockSpec((tt, BRANCH_W), lambda b, t, cb=cb: (row(b, t), cb))
    return pl.pallas_call(
        functools.partial(_deltanet_kernel, tt=tt),
        grid=(batch, nt),
        in_specs=[
            cspec(CB_AQ), cspec(CB_AK), cspec(CB_AV), cspec(CB_AZ),
            pl.BlockSpec((tt, SMALL_COLS), lambda b, t: (row(b, t), 0)),
            pl.BlockSpec((CONV_W, 3 * BRANCH_W), lambda b, t: (0, 0)),
            pl.BlockSpec((8, SMALL_COLS), lambda b, t: (0, 0)),
            pl.BlockSpec((1, DN_DK), lambda b, t: (0, 0)),
        ],
        out_specs=pl.BlockSpec((tt, BRANCH_W), lambda b, t: (row(b, t), 0)),
        out_shape=jax.ShapeDtypeStruct((batch * seq, BRANCH_W), BF16),
        scratch_shapes=[
            pltpu.VMEM((3, 8, BRANCH_W), F32),
            pltpu.VMEM((DN_DK, DN_HEADS * DN_DK), F32),
        ],
        compiler_params=pltpu.CompilerParams(
            dimension_semantics=("arbitrary", "arbitrary"), vmem_limit_bytes=VMEM_LIMIT),
        name="deltanet",
    )(cols, cols, cols, cols, small, conv_w, gate_params, o_norm)


def _gmlp_kernel(u_ref, v_ref, z_ref, gain_ref, ws_ref, bias_ref, o_ref, *, tt):
    P, G = GM_CHUNK, GM_GROUPS
    u = jax.nn.gelu(u_ref[...].astype(F32))
    v = _rms(jax.nn.gelu(v_ref[...].astype(F32)), gain_ref[...])
    vb = v.astype(BF16)
    tr = lax.broadcasted_iota(jnp.int32, (P, P), 0)
    tc = lax.broadcasted_iota(jnp.int32, (P, P), 1)
    causal = tr >= tc
    bias = bias_ref[...]
    for g in range(G):
        wg = jnp.where(causal, ws_ref[g], 0.0).astype(BF16)
        for ci in range(tt // P):
            lo = ci * P
            s = _dot(wg, vb[lo:lo + P, g * P:(g + 1) * P]) + bias[:, g * P:(g + 1) * P]
            z = z_ref[lo:lo + P, g * P:(g + 1) * P].astype(F32)
            o_ref[lo:lo + P, g * P:(g + 1) * P] = (u[lo:lo + P, g * P:(g + 1) * P] * s * _silu(z)).astype(BF16)


def _gmlp(cols, gain, spatial_w, bias_mat, *, tt):
    m = cols.shape[0]
    cspec = lambda cb: pl.BlockSpec((tt, BRANCH_W), lambda i, cb=cb: (i, cb))
    return pl.pallas_call(
        functools.partial(_gmlp_kernel, tt=tt),
        grid=(m // tt,),
        in_specs=[
            cspec(CB_BU), cspec(CB_BV), cspec(CB_BZ),
            pl.BlockSpec((1, BRANCH_W), lambda i: (0, 0)),
            pl.BlockSpec((GM_GROUPS, GM_CHUNK, GM_CHUNK), lambda i: (0, 0, 0)),
            pl.BlockSpec((GM_CHUNK, BRANCH_W), lambda i: (0, 0)),
        ],
        out_specs=pl.BlockSpec((tt, BRANCH_W), lambda i: (i, 0)),
        out_shape=jax.ShapeDtypeStruct((m, BRANCH_W), BF16),
        compiler_params=pltpu.CompilerParams(
            dimension_semantics=("arbitrary",), vmem_limit_bytes=VMEM_LIMIT),
        name="gmlp",
    )(cols, cols, cols, gain, spatial_w, bias_mat)


def _swa_kernel(sinks_ref, q_ref, z_ref, kvc_ref, kvp_ref, o_ref):
    P, HD, KV = SW_BLOCK, SW_HD, SW_KV_HEADS
    G = SW_HEADS // KV
    first = pl.program_id(1) == 0
    qi = lax.broadcasted_iota(jnp.int32, (P, 2 * P), 0)
    kj = lax.broadcasted_iota(jnp.int32, (P, 2 * P), 1)
    dist = qi + P - kj
    upper = jnp.where(kj >= P, 2 * P, jnp.where(first, 0, P))
    valid = jnp.logical_and(dist >= 0, dist < upper)
    valid = jnp.concatenate([valid] * G, axis=0)
    kvc = kvc_ref[...]
    kvp = kvp_ref[...]
    for kh in range(KV):
        kb = jnp.concatenate([kvp[:, kh * HD:(kh + 1) * HD], kvc[:, kh * HD:(kh + 1) * HD]], axis=0)
        vb = jnp.concatenate([kvp[:, (KV + kh) * HD:(KV + kh + 1) * HD],
                              kvc[:, (KV + kh) * HD:(KV + kh + 1) * HD]], axis=0)
        qs = jnp.concatenate([q_ref[:, (kh * G + g) * HD:(kh * G + g + 1) * HD] for g in range(G)], axis=0)
        s = jnp.where(valid, _dot_nt(qs, kb) * (HD ** -0.5), NEG_INF)
        sink = jnp.concatenate(
            [jnp.full((P, 1), sinks_ref[kh * G + g], F32) for g in range(G)], axis=0)
        mx = jnp.maximum(jnp.max(s, axis=-1, keepdims=True), sink)
        e = jnp.exp(s - mx)
        den = jnp.sum(e, axis=-1, keepdims=True) + jnp.exp(sink - mx)
        o = _dot((e / den).astype(BF16), vb)
        for g2 in range(G // 2):
            h0 = kh * G + 2 * g2
            pair = jnp.concatenate([o[2 * g2 * P:(2 * g2 + 1) * P], o[(2 * g2 + 1) * P:(2 * g2 + 2) * P]], axis=1)
            z = z_ref[:, h0 * HD:(h0 + 2) * HD].astype(F32)
            o_ref[:, h0 * HD:(h0 + 2) * HD] = (pair * _silu(z)).astype(BF16)


def _swa(cols, sinks, *, batch, seq):
    P = SW_BLOCK
    nb = seq // P
    row = lambda b, i: b * nb + i
    return pl.pallas_call(
        _swa_kernel,
        grid_spec=pltpu.PrefetchScalarGridSpec(
            num_scalar_prefetch=1,
            grid=(batch, nb),
            in_specs=[
                pl.BlockSpec((P, BRANCH_W), lambda b, i, s: (row(b, i), CB_CQ)),
                pl.BlockSpec((P, BRANCH_W), lambda b, i, s: (row(b, i), CB_CZ)),
                pl.BlockSpec((P, 256), lambda b, i, s: (row(b, i), CB_CKV_256)),
                pl.BlockSpec((P, 256), lambda b, i, s: (row(b, jnp.maximum(i - 1, 0)), CB_CKV_256)),
            ],
            out_specs=pl.BlockSpec((P, BRANCH_W), lambda b, i, s: (row(b, i), 0)),
        ),
        out_shape=jax.ShapeDtypeStruct((batch * seq, BRANCH_W), BF16),
        compiler_params=pltpu.CompilerParams(
            dimension_semantics=("arbitrary", "arbitrary"), vmem_limit_bytes=VMEM_LIMIT),
        name="swa",
    )(sinks, cols, cols, cols, cols)


def _memkv_kernel(m_ref, g_ref, w_ref, o_ref):
    o_ref[0] = _dot(_rms(m_ref[0], g_ref[...]).astype(BF16), w_ref[...]).astype(BF16)


def _memkv(mem, gain, w):
    b, ml, _ = mem.shape
    return pl.pallas_call(
        _memkv_kernel,
        grid=(b,),
        in_specs=[
            pl.BlockSpec((1, ml, D_MODEL), lambda i: (i, 0, 0)),
            pl.BlockSpec((1, D_MODEL), lambda i: (0, 0)),
            pl.BlockSpec((D_MODEL, 2 * BRANCH_W), lambda i: (0, 0)),
        ],
        out_specs=pl.BlockSpec((1, ml, 2 * BRANCH_W), lambda i: (i, 0, 0)),
        out_shape=jax.ShapeDtypeStruct((b, ml, 2 * BRANCH_W), BF16),
        compiler_params=pltpu.CompilerParams(
            dimension_semantics=("arbitrary",), vmem_limit_bytes=VMEM_LIMIT),
        name="memkv",
    )(mem, gain, w)


def _memattn_kernel(q_ref, z_ref, kv_ref, o_ref):
    HD = XM_HD
    for h in range(XM_HEADS):
        mk = kv_ref[0, :, h * HD:(h + 1) * HD]
        mv = kv_ref[0, :, BRANCH_W + h * HD:BRANCH_W + (h + 1) * HD]
        s = _dot_nt(q_ref[:, h * HD:(h + 1) * HD], mk) * (HD ** -0.5)
        e = jnp.exp(s - jnp.max(s, axis=-1, keepdims=True))
        p = e / jnp.sum(e, axis=-1, keepdims=True)
        o = _dot(p.astype(BF16), mv)
        z = z_ref[:, h * HD:(h + 1) * HD].astype(F32)
        o_ref[:, h * HD:(h + 1) * HD] = (o * _silu(z)).astype(BF16)


def _memattn(cols, mem_kv, *, batch, seq, tt):
    nt = seq // tt
    ml = mem_kv.shape[1]
    row = lambda b, t: b * nt + t
    return pl.pallas_call(
        _memattn_kernel,
        grid=(batch, nt),
        in_specs=[
            pl.BlockSpec((tt, BRANCH_W), lambda b, t: (row(b, t), CB_MQ)),
            pl.BlockSpec((tt, BRANCH_W), lambda b, t: (row(b, t), CB_MZ)),
            pl.BlockSpec((1, ml, 2 * BRANCH_W), lambda b, t: (b, 0, 0)),
        ],
        out_specs=pl.BlockSpec((tt, BRANCH_W), lambda b, t: (row(b, t), 0)),
        out_shape=jax.ShapeDtypeStruct((batch * seq, BRANCH_W), BF16),
        compiler_params=pltpu.CompilerParams(
            dimension_semantics=("arbitrary", "arbitrary"), vmem_limit_bytes=VMEM_LIMIT),
        name="memattn",
    )(cols, cols, mem_kv)


def _merge_kernel(ya_ref, yb_ref, yc_ref, ym_ref, g0_ref, g1_ref, g2_ref, g3_ref, x_ref,
                  wup_ref, wout_ref, gain_ref, o_ref):
    ys = (ya_ref, yb_ref, yc_ref, ym_ref)
    gs = (g0_ref, g1_ref, g2_ref, g3_ref)
    merged = None
    for n in range(N_BRANCH):
        term = _sigmoid(gs[n][...].astype(F32)) * _dot(ys[n][...], wup_ref[n])
        merged = term if merged is None else merged + term
    out = _dot(merged.astype(BF16), wout_ref[...])
    o_ref[...] = x_ref[...] + _rms(out, gain_ref[...])


def _merge(ys, cols, x2, w_up, w_out, gain, *, tm):
    m = x2.shape[0]
    yspec = pl.BlockSpec((tm, BRANCH_W), lambda i: (i, 0))
    gspec = lambda n: pl.BlockSpec((tm, D_MODEL), lambda i, n=n: (i, n))
    return pl.pallas_call(
        _merge_kernel,
        grid=(m // tm,),
        in_specs=[
            yspec, yspec, yspec, yspec, gspec(0), gspec(1), gspec(2), gspec(3),
            pl.BlockSpec((tm, D_MODEL), lambda i: (i, 0)),
            pl.BlockSpec((N_BRANCH, BRANCH_W, D_MODEL), lambda i: (0, 0, 0)),
            pl.BlockSpec((D_MODEL, D_MODEL), lambda i: (0, 0)),
            pl.BlockSpec((1, D_MODEL), lambda i: (0, 0)),
        ],
        out_specs=pl.BlockSpec((tm, D_MODEL), lambda i: (i, 0)),
        out_shape=jax.ShapeDtypeStruct((m, D_MODEL), F32),
        compiler_params=pltpu.CompilerParams(
            dimension_semantics=("arbitrary",), vmem_limit_bytes=VMEM_LIMIT),
        name="merge",
    )(*ys, cols, cols, cols, cols, x2, w_up, w_out, gain)


def _permute_w_in(w):
    a_end = 4 * BRANCH_W
    b_lo = a_end + 2 * DN_HEADS
    b_end = b_lo + 3 * BRANCH_W
    cq_end = b_end + BRANCH_W
    ckv_end = cq_end + 2 * SW_KV_HEADS * SW_HD
    cz_end = ckv_end + BRANCH_W
    m_end = cz_end + 2 * BRANCH_W
    main = jnp.concatenate(
        [w[:, m_end:], w[:, :a_end], w[:, b_lo:b_end], w[:, b_end:cq_end], w[:, ckv_end:cz_end],
         w[:, cz_end:m_end], w[:, cq_end:ckv_end]], axis=1)
    small = jnp.pad(w[:, a_end:b_lo], ((0, 0), (0, SMALL_COLS - 2 * DN_HEADS)))
    return main.astype(BF16), small.astype(BF16)


def _layer(x2, mem, p, *, batch, seq):
    w_main, w_small = _permute_w_in(p["w_in"])
    cols, small = _inproj(x2, p["norm_pre"][None], w_main, w_small, tm=512, tn=MAIN_COLS // 3)

    gate_params = jnp.zeros((8, SMALL_COLS), F32)
    gate_params = gate_params.at[0, DN_HEADS:2 * DN_HEADS].set(-jnp.exp(p["a_log"]))
    gate_params = gate_params.at[1, DN_HEADS:2 * DN_HEADS].set(p["dt_bias"])
    y_a = _deltanet(cols, small, p["conv_w"], gate_params, p["dn_norm"][None], batch=batch, seq=seq, tt=256)

    bias_mat = jnp.repeat(p["spatial_b"].T, GM_CHUNK, axis=1)
    y_b = _gmlp(cols, p["gm_norm"][None], p["spatial_w"], bias_mat, tt=256)

    y_c = _swa(cols, p["sinks"], batch=batch, seq=seq)

    mem_kv = _memkv(mem, p["norm_mem"][None], p["w_mem_kv"].astype(BF16))
    y_m = _memattn(cols, mem_kv, batch=batch, seq=seq, tt=256)

    return _merge((y_a, y_b, y_c, y_m), cols, x2, p["w_up"].astype(BF16), p["w_out"].astype(BF16),
                  p["norm_post"][None], tm=512)


def kernel(x, mem, norm_pre, norm_post, norm_mem, w_in, conv_w, a_log, dt_bias, dn_norm, gm_norm,
           spatial_w, spatial_b, sinks, w_mem_kv, w_up, w_out):
    batch, seq, d = x.shape
    x2 = x.reshape(batch * seq, d)
    names = ("norm_pre", "norm_post", "norm_mem", "w_in", "conv_w", "a_log", "dt_bias", "dn_norm", "gm_norm",
             "spatial_w", "spatial_b", "sinks", "w_mem_kv", "w_up", "w_out")
    stacked = (norm_pre, norm_post, norm_mem, w_in, conv_w, a_log, dt_bias, dn_norm, gm_norm,
               spatial_w, spatial_b, sinks, w_mem_kv, w_up, w_out)
    for l in range(norm_pre.shape[0]):
        p = {n: a[l] for n, a in zip(names, stacked)}
        x2 = _layer(x2, mem, p, batch=batch, seq=seq)
    return x2.reshape(batch, seq, d)
```

```python
import functools

import jax
import jax.numpy as jnp
from jax import lax
from jax.experimental import pallas as pl
from jax.experimental.pallas import tpu as pltpu

F32 = jnp.float32
BF16 = jnp.bfloat16

D_MODEL = 1024
N_BRANCH = 4
BRANCH_W = 512
DN_HEADS = 4
DN_DK = 128
DN_CHUNK = 64
CONV_W = 4
GM_GROUPS = 4
GM_CHUNK = 128
SW_HEADS = 8
SW_KV_HEADS = 2
SW_HD = 64
SW_BLOCK = 128
XM_HEADS = 4
XM_HD = 128
EPS = 1e-6
NEG_INF = -1e30

GATE_COLS = N_BRANCH * D_MODEL
MAIN_COLS = GATE_COLS + 11 * BRANCH_W + 2 * SW_KV_HEADS * SW_HD
SMALL_COLS = 128
CB_AQ, CB_AK, CB_AV, CB_AZ, CB_BU, CB_BV, CB_BZ, CB_CQ, CB_CZ, CB_MQ, CB_MZ = range(8, 19)
CB_CKV_256 = (GATE_COLS + 11 * BRANCH_W) // 256

VMEM_LIMIT = 56 * 1024 * 1024
TM_INPROJ = 512
TN_INPROJ = MAIN_COLS // 3
TT_MIX = 256


def _sigmoid(x):
    return 1.0 / (1.0 + jnp.exp(-x))


def _silu(x):
    return x * _sigmoid(x)


def _softplus(x):
    return jnp.maximum(x, 0.0) + jnp.log(1.0 + jnp.exp(-jnp.abs(x)))


def _rms(x, gain):
    return x * lax.rsqrt(jnp.mean(x * x, axis=-1, keepdims=True) + EPS) * gain


def _dot(a, b):
    return jnp.dot(a, b, preferred_element_type=F32)


def _dot_nt(a, b):
    return lax.dot_general(a, b, (((1,), (1,)), ((), ())), preferred_element_type=F32)


def _inproj_kernel(x_ref, g_ref, w_ref, ws_ref, o_ref, os_ref, h_ref):
    @pl.when(pl.program_id(1) == 0)
    def _():
        hb = _rms(x_ref[...], g_ref[...]).astype(BF16)
        h_ref[...] = hb
        os_ref[...] = _dot(hb, ws_ref[...])

    o_ref[...] = _dot(h_ref[...], w_ref[...]).astype(BF16)


def _inproj(x2, gain, w_main, w_small, *, tm, tn):
    m = x2.shape[0]
    return pl.pallas_call(
        _inproj_kernel,
        grid=(m // tm, MAIN_COLS // tn),
        in_specs=[
            pl.BlockSpec((tm, D_MODEL), lambda i, j: (i, 0)),
            pl.BlockSpec((1, D_MODEL), lambda i, j: (0, 0)),
            pl.BlockSpec((D_MODEL, tn), lambda i, j: (0, j)),
            pl.BlockSpec((D_MODEL, SMALL_COLS), lambda i, j: (0, 0)),
        ],
        out_specs=[
            pl.BlockSpec((tm, tn), lambda i, j: (i, j)),
            pl.BlockSpec((tm, SMALL_COLS), lambda i, j: (i, 0)),
        ],
        out_shape=[
            jax.ShapeDtypeStruct((m, MAIN_COLS), BF16),
            jax.ShapeDtypeStruct((m, SMALL_COLS), F32),
        ],
        scratch_shapes=[pltpu.VMEM((tm, D_MODEL), BF16)],
        compiler_params=pltpu.CompilerParams(
            dimension_semantics=("arbitrary", "arbitrary"), vmem_limit_bytes=VMEM_LIMIT),
        name="inproj",
    )(x2, gain, w_main, w_small)


def _memkv_kernel(m_ref, g_ref, w_ref, o_ref):
    o_ref[0] = _dot(_rms(m_ref[0], g_ref[...]).astype(BF16), w_ref[...]).astype(BF16)


def _memkv(mem, gain, w):
    b, ml, _ = mem.shape
    return pl.pallas_call(
        _memkv_kernel,
        grid=(b,),
        in_specs=[
            pl.BlockSpec((1, ml, D_MODEL), lambda i: (i, 0, 0)),
            pl.BlockSpec((1, D_MODEL), lambda i: (0, 0)),
            pl.BlockSpec((D_MODEL, 2 * BRANCH_W), lambda i: (0, 0)),
        ],
        out_specs=pl.BlockSpec((1, ml, 2 * BRANCH_W), lambda i: (i, 0, 0)),
        out_shape=jax.ShapeDtypeStruct((b, ml, 2 * BRANCH_W), BF16),
        compiler_params=pltpu.CompilerParams(
            dimension_semantics=("arbitrary",), vmem_limit_bytes=VMEM_LIMIT),
        name="memkv",
    )(mem, gain, w)


def _deltanet_branch(q_ref, k_ref, v_ref, z_ref, sm_ref, cw_ref, gp_ref, on_ref, y_ref, tail_ref, s_ref, *, tt):
    H, C, DK = DN_HEADS, DN_CHUNK, DN_DK
    HC = H * C
    cw = cw_ref[...]

    def conv_silu(x_ref, idx):
        cur = x_ref[...].astype(F32)
        ext = jnp.concatenate([tail_ref[idx], cur], axis=0)
        lo = idx * BRANCH_W
        acc = cur * cw[CONV_W - 1:CONV_W, lo:lo + BRANCH_W]
        for j in range(1, CONV_W):
            shifted = pltpu.roll(ext, j, axis=0)[8:]
            acc = acc + shifted * cw[CONV_W - 1 - j:CONV_W - j, lo:lo + BRANCH_W]
        tail_ref[idx] = cur[tt - 8:]
        return _silu(acc)

    qc = conv_silu(q_ref, 0)
    kc = conv_silu(k_ref, 1)
    vc = conv_silu(v_ref, 2)

    def l2n(xh):
        return xh * lax.rsqrt(jnp.sum(xh * xh, axis=-1, keepdims=True) + EPS)

    ci = lax.broadcasted_iota(jnp.int32, (C, HC), 0)
    cl = lax.broadcasted_iota(jnp.int32, (C, HC), 1)
    cj = cl % C
    strict = ci > cj
    incl = ci >= cj
    eye_cat = (ci == cj).astype(F32)
    lane_head = cl // C
    br = lax.broadcasted_iota(jnp.int32, (HC, HC), 0)
    bc = lax.broadcasted_iota(jnp.int32, (HC, HC), 1)
    bd_mask = jnp.where((br // C) == (bc // C), 1.0, 0.0).astype(BF16)
    kl = lax.broadcasted_iota(jnp.int32, (DK, HC), 1) // C
    head_lane_masks = [jnp.where(kl == h, 1.0, 0.0).astype(BF16) for h in range(H)]
    l64 = lax.broadcasted_iota(jnp.int32, (C, DK), 1) < C
    tr = lax.broadcasted_iota(jnp.int32, (C, C), 0)
    tc = lax.broadcasted_iota(jnp.int32, (C, C), 1)
    ltri = jnp.where(tr >= tc, 1.0, 0.0).astype(BF16)

    neg_a = gp_ref[0:1, :]
    dt_b = gp_ref[1:2, :]
    o_gain = on_ref[...]

    def stack(x, lo):
        return jnp.concatenate([x[lo:lo + C, h * DK:(h + 1) * DK] for h in range(H)], axis=0)

    def block_diag(x_cat):
        return jnp.concatenate([x_cat.astype(BF16)] * H, axis=0) * bd_mask

    def diag_to_cat(m):
        out = m[(H - 1) * C:H * C]
        for h in range(H - 2, -1, -1):
            out = jnp.where(lane_head == h, m[h * C:(h + 1) * C], out)
        return out

    for ci_ in range(tt // C):
        lo = ci_ * C
        qs = jnp.concatenate([l2n(qc[lo:lo + C, h * DK:(h + 1) * DK]) for h in range(H)], axis=0) * (DK ** -0.5)
        ks = jnp.concatenate([l2n(kc[lo:lo + C, h * DK:(h + 1) * DK]) for h in range(H)], axis=0)
        vs = stack(vc, lo)
        sm = sm_ref[lo:lo + C, :]
        beta_all = _sigmoid(sm)
        g_all = neg_a * _softplus(sm + dt_b)
        g_hi = g_all.astype(BF16)
        r1 = g_all - g_hi.astype(F32)
        g_mid = r1.astype(BF16)
        g_lo = (r1 - g_mid.astype(F32)).astype(BF16)
        gc3 = _dot(ltri, jnp.concatenate([g_hi, g_mid, g_lo], axis=1))
        gc_all = gc3[:, :SMALL_COLS] + gc3[:, SMALL_COLS:2 * SMALL_COLS] + gc3[:, 2 * SMALL_COLS:]
        gcb = [jnp.broadcast_to(gc_all[:, H + h:H + h + 1], (C, DK)) for h in range(H)]
        cb = jnp.concatenate(gcb, axis=0)
        bb = jnp.concatenate(
            [jnp.broadcast_to(beta_all[:, h:h + 1], (C, DK)) for h in range(H)], axis=0)
        c_cat = jnp.concatenate([jnp.where(l64, gcb[0], gcb[1]), jnp.where(l64, gcb[2], gcb[3])], axis=1)
        gct = jnp.concatenate([gc_all, gc_all], axis=0).T
        r_cat = jnp.concatenate(
            [jnp.where(l64[0:1], gct[H:H + 1], gct[H + 1:H + 2]),
             jnp.where(l64[0:1], gct[H + 2:H + 3], gct[H + 3:H + 4])], axis=1)
        decay = jnp.exp(jnp.where(incl, c_cat - r_cat, 0.0))
        egc = jnp.exp(cb)
        glast = jnp.concatenate(
            [jnp.broadcast_to(cb[h * C + C - 1:h * C + C, :], (C, DK)) for h in range(H)], axis=0)

        kb = ks * bb
        kq = _dot_nt(jnp.concatenate([kb, qs], axis=0).astype(BF16), ks.astype(BF16))
        l_cat = jnp.where(strict, diag_to_cat(kq[:HC]) * decay, 0.0)
        a_cat = jnp.where(incl, diag_to_cat(kq[HC:]) * decay, 0.0)
        n_cat = -l_cat
        p_cat = eye_cat + n_cat
        cur = _dot(n_cat.astype(BF16), block_diag(n_cat))
        for _ in range(4):
            out = _dot(jnp.concatenate([cur, p_cat], axis=0).astype(BF16), block_diag(cur))
            p_cat = p_cat + out[C:]
            cur = out[:C]
        t_cat = p_cat + _dot(p_cat.astype(BF16), block_diag(cur))
        rhs = jnp.concatenate([vs * bb, kb * egc], axis=1)
        sol = _dot(block_diag(t_cat), rhs.astype(BF16))
        solb = sol.astype(BF16)
        auw = _dot(block_diag(a_cat), solb)
        qp = qs * egc - auw[:, DK:]
        kd_t = (ks * jnp.exp(glast - cb)).T.astype(BF16)

        for h in range(H):
            gb = _dot(kd_t * head_lane_masks[h], solb)
            s_h = s_ref[:, h * DK:(h + 1) * DK]
            gq = _dot(jnp.concatenate([gb[:, DK:], qp[h * C:(h + 1) * C]], axis=0).astype(BF16),
                      s_h.astype(BF16))
            d_h = jnp.exp(cb[h * C + C - 1:h * C + C, :])
            s_ref[:, h * DK:(h + 1) * DK] = s_h * d_h - gq[:DK] + gb[:, :DK]
            oh = _rms(gq[DK:] + auw[h * C:(h + 1) * C, :DK], o_gain)
            zh = z_ref[lo:lo + C, h * DK:(h + 1) * DK].astype(F32)
            y_ref[lo:lo + C, h * DK:(h + 1) * DK] = (oh * _silu(zh)).astype(BF16)


def _gmlp_branch(u_ref, v_ref, z_ref, gain_ref, ws_ref, bias_ref, y_ref, *, tt):
    P, G = GM_CHUNK, GM_GROUPS
    u = jax.nn.gelu(u_ref[...].astype(F32))
    v = _rms(jax.nn.gelu(v_ref[...].astype(F32)), gain_ref[...])
    vb = v.astype(BF16)
    tr = lax.broadcasted_iota(jnp.int32, (P, P), 0)
    tc = lax.broadcasted_iota(jnp.int32, (P, P), 1)
    causal = tr >= tc
    bias = bias_ref[...]
    for g in range(G):
        wg = jnp.where(causal, ws_ref[g], 0.0).astype(BF16)
        for ci in range(tt // P):
            lo = ci * P
            s = _dot(wg, vb[lo:lo + P, g * P:(g + 1) * P]) + bias[:, g * P:(g + 1) * P]
            z = z_ref[lo:lo + P, g * P:(g + 1) * P].astype(F32)
            y_ref[lo:lo + P, g * P:(g + 1) * P] = (u[lo:lo + P, g * P:(g + 1) * P] * s * _silu(z)).astype(BF16)


def _swa_branch(sinks_ref, q_ref, z_ref, kv_ref, y_ref, kvprev_ref, *, tt):
    P, HD, KV = SW_BLOCK, SW_HD, SW_KV_HEADS
    G = SW_HEADS // KV
    kj = lax.broadcasted_iota(jnp.int32, (2 * P, P), 0)
    qi = lax.broadcasted_iota(jnp.int32, (2 * P, P), 1)
    dist = qi + P - kj
    window = jnp.logical_and(dist >= 0, dist < P)
    valid_mid = jnp.concatenate([window] * G, axis=1)
    has_prev = pl.program_id(1) > 0
    valid_first = jnp.concatenate([jnp.logical_and(window, jnp.logical_or(kj >= P, has_prev))] * G, axis=1)
    kv = jnp.concatenate([kvprev_ref[...], kv_ref[...]], axis=0)
    kvprev_ref[...] = kv_ref[tt - P:, :]
    for kh in range(KV):
        sink = jnp.concatenate([jnp.full((1, P), sinks_ref[kh * G + g], F32) for g in range(G)], axis=1)
        for j in range(tt // P):
            lo = j * P
            kb = kv[lo:lo + 2 * P, kh * HD:(kh + 1) * HD]
            vb = kv[lo:lo + 2 * P, (KV + kh) * HD:(KV + kh + 1) * HD]
            qs = jnp.concatenate(
                [q_ref[lo:lo + P, (kh * G + g) * HD:(kh * G + g + 1) * HD] for g in range(G)], axis=0)
            qs = qs * (HD ** -0.5)
            s = jnp.where(valid_first if j == 0 else valid_mid, _dot_nt(kb, qs), NEG_INF)
            mx = jnp.maximum(jnp.max(s, axis=0, keepdims=True), sink)
            e = jnp.exp(s - mx)
            den = jnp.sum(e, axis=0, keepdims=True) + jnp.exp(sink - mx)
            o_t = _dot(vb.astype(F32).T.astype(BF16), e.astype(BF16)) * (1.0 / den)
            for g2 in range(G // 2):
                h0 = kh * G + 2 * g2
                pair = jnp.concatenate(
                    [o_t[:, 2 * g2 * P:(2 * g2 + 1) * P], o_t[:, (2 * g2 + 1) * P:(2 * g2 + 2) * P]], axis=0).T
                z = z_ref[lo:lo + P, h0 * HD:(h0 + 2) * HD].astype(F32)
                y_ref[lo:lo + P, h0 * HD:(h0 + 2) * HD] = (pair * _silu(z)).astype(BF16)


def _memattn_branch(q_ref, z_ref, kv_ref, y_ref):
    HD = XM_HD
    for h in range(XM_HEADS):
        mk = kv_ref[0, :, h * HD:(h + 1) * HD]
        mv = kv_ref[0, :, BRANCH_W + h * HD:BRANCH_W + (h + 1) * HD]
        s = _dot_nt(q_ref[:, h * HD:(h + 1) * HD], mk) * (HD ** -0.5)
        e = jnp.exp(s - jnp.max(s, axis=-1, keepdims=True))
        o = _dot(e.astype(BF16), mv) * (1.0 / jnp.sum(e, axis=-1, keepdims=True))
        z = z_ref[:, h * HD:(h + 1) * HD].astype(F32)
        y_ref[:, h * HD:(h + 1) * HD] = (o * _silu(z)).astype(BF16)


def _mix_kernel(sinks_ref,
                aq_ref, ak_ref, av_ref, az_ref, bu_ref, bv_ref, bz_ref, cq_ref, cz_ref, ckv_ref, mq_ref, mz_ref,
                g0_ref, g1_ref, g2_ref, g3_ref, sm_ref, x_ref,
                cw_ref, gp_ref, on_ref, gmn_ref, ws_ref, bias_ref, mkv_ref, wup_ref, wout_ref, npost_ref,
                o_ref,
                tail_ref, s_ref, kvprev_ref, y_ref, *, tt):
    @pl.when(pl.program_id(1) == 0)
    def _():
        tail_ref[...] = jnp.zeros_like(tail_ref)
        s_ref[...] = jnp.zeros_like(s_ref)
        kvprev_ref[...] = jnp.zeros_like(kvprev_ref)

    _deltanet_branch(aq_ref, ak_ref, av_ref, az_ref, sm_ref, cw_ref, gp_ref, on_ref, y_ref.at[0],
                     tail_ref, s_ref, tt=tt)
    _gmlp_branch(bu_ref, bv_ref, bz_ref, gmn_ref, ws_ref, bias_ref, y_ref.at[1], tt=tt)
    _swa_branch(sinks_ref, cq_ref, cz_ref, ckv_ref, y_ref.at[2], kvprev_ref, tt=tt)
    _memattn_branch(mq_ref, mz_ref, mkv_ref, y_ref.at[3])

    gs = (g0_ref, g1_ref, g2_ref, g3_ref)
    merged = None
    for n in range(N_BRANCH):
        term = _sigmoid(gs[n][...].astype(F32)) * _dot(y_ref[n], wup_ref[n])
        merged = term if merged is None else merged + term
    out = _dot(merged.astype(BF16), wout_ref[...])
    o_ref[...] = x_ref[...] + _rms(out, npost_ref[...])


def _mix(cols, small, x2, sinks, conv_w, gate_params, o_norm, gm_norm, spatial_w, bias_mat, mem_kv,
         w_up, w_out, norm_post, *, batch, seq, tt):
    nt = seq // tt
    ml = mem_kv.shape[1]
    row = lambda b, t: b * nt + t
    cspec = lambda cb: pl.BlockSpec((tt, BRANCH_W), lambda b, t, s, cb=cb: (row(b, t), cb))
    gspec = lambda n: pl.BlockSpec((tt, D_MODEL), lambda b, t, s, n=n: (row(b, t), n))
    const = lambda shape: pl.BlockSpec(shape, lambda b, t, s: (0,) * len(shape))
    in_specs = [cspec(cb) for cb in (CB_AQ, CB_AK, CB_AV, CB_AZ, CB_BU, CB_BV, CB_BZ, CB_CQ, CB_CZ)]
    in_specs += [pl.BlockSpec((tt, 256), lambda b, t, s: (row(b, t), CB_CKV_256)), cspec(CB_MQ), cspec(CB_MZ)]
    in_specs += [gspec(n) for n in range(N_BRANCH)]
    in_specs += [
        pl.BlockSpec((tt, SMALL_COLS), lambda b, t, s: (row(b, t), 0)),
        pl.BlockSpec((tt, D_MODEL), lambda b, t, s: (row(b, t), 0)),
        const((CONV_W, 3 * BRANCH_W)), const((8, SMALL_COLS)), const((1, DN_DK)), const((1, BRANCH_W)),
        const((GM_GROUPS, GM_CHUNK, GM_CHUNK)), const((GM_CHUNK, BRANCH_W)),
        pl.BlockSpec((1, ml, 2 * BRANCH_W), lambda b, t, s: (b, 0, 0)),
        const((N_BRANCH, BRANCH_W, D_MODEL)), const((D_MODEL, D_MODEL)), const((1, D_MODEL)),
    ]
    return pl.pallas_call(
        functools.partial(_mix_kernel, tt=tt),
        grid_spec=pltpu.PrefetchScalarGridSpec(
            num_scalar_prefetch=1,
            grid=(batch, nt),
            in_specs=in_specs,
            out_specs=pl.BlockSpec((tt, D_MODEL), lambda b, t, s: (row(b, t), 0)),
            scratch_shapes=[
                pltpu.VMEM((3, 8, BRANCH_W), F32),
                pltpu.VMEM((DN_DK, DN_HEADS * DN_DK), F32),
                pltpu.VMEM((SW_BLOCK, 256), BF16),
                pltpu.VMEM((N_BRANCH, tt, BRANCH_W), BF16),
            ],
        ),
        out_shape=jax.ShapeDtypeStruct((batch * seq, D_MODEL), F32),
        compiler_params=pltpu.CompilerParams(
            dimension_semantics=("arbitrary", "arbitrary"), vmem_limit_bytes=VMEM_LIMIT),
        name="mix",
    )(sinks, *([cols] * 16), small, x2, conv_w, gate_params, o_norm, gm_norm, spatial_w, bias_mat, mem_kv,
      w_up, w_out, norm_post)


def _permute_w_in(w):
    a_end = 4 * BRANCH_W
    b_lo = a_end + 2 * DN_HEADS
    b_end = b_lo + 3 * BRANCH_W
    cq_end = b_end + BRANCH_W
    ckv_end = cq_end + 2 * SW_KV_HEADS * SW_HD
    cz_end = ckv_end + BRANCH_W
    m_end = cz_end + 2 * BRANCH_W
    main = jnp.concatenate(
        [w[:, m_end:], w[:, :a_end], w[:, b_lo:b_end], w[:, b_end:cq_end], w[:, ckv_end:cz_end],
         w[:, cz_end:m_end], w[:, cq_end:ckv_end]], axis=1)
    small = jnp.pad(w[:, a_end:b_lo], ((0, 0), (0, SMALL_COLS - 2 * DN_HEADS)))
    return main.astype(BF16), small.astype(BF16)


def _layer(x2, mem, p, *, batch, seq):
    w_main, w_small = _permute_w_in(p["w_in"])
    cols, small = _inproj(x2, p["norm_pre"][None], w_main, w_small, tm=TM_INPROJ, tn=TN_INPROJ)
    mem_kv = _memkv(mem, p["norm_mem"][None], p["w_mem_kv"].astype(BF16))

    gate_params = jnp.zeros((8, SMALL_COLS), F32)
    gate_params = gate_params.at[0, DN_HEADS:2 * DN_HEADS].set(-jnp.exp(p["a_log"]))
    gate_params = gate_params.at[1, DN_HEADS:2 * DN_HEADS].set(p["dt_bias"])
    bias_mat = jnp.repeat(p["spatial_b"].T, GM_CHUNK, axis=1)
    return _mix(cols, small, x2, p["sinks"], p["conv_w"], gate_params, p["dn_norm"][None], p["gm_norm"][None],
                p["spatial_w"], bias_mat, mem_kv, p["w_up"].astype(BF16), p["w_out"].astype(BF16),
                p["norm_post"][None], batch=batch, seq=seq, tt=TT_MIX)


def kernel(x, mem, norm_pre, norm_post, norm_mem, w_in, conv_w, a_log, dt_bias, dn_norm, gm_norm,
           spatial_w, spatial_b, sinks, w_mem_kv, w_up, w_out):
    batch, seq, d = x.shape
    x2 = x.reshape(batch * seq, d)
    names = ("norm_pre", "norm_post", "norm_mem", "w_in", "conv_w", "a_log", "dt_bias", "dn_norm", "gm_norm",
             "spatial_w", "spatial_b", "sinks", "w_mem_kv", "w_up", "w_out")
    stacked = (norm_pre, norm_post, norm_mem, w_in, conv_w, a_log, dt_bias, dn_norm, gm_norm,
               spatial_w, spatial_b, sinks, w_mem_kv, w_up, w_out)
    for l in range(norm_pre.shape[0]):
        p = {n: a[l] for n, a in zip(names, stacked)}
        x2 = _layer(x2, mem, p, batch=batch, seq=seq)
    return x2.reshape(batch, seq, d)
```

```python
import functools

import jax
import jax.numpy as jnp
from jax import lax
from jax.experimental import pallas as pl
from jax.experimental.pallas import tpu as pltpu

F32 = jnp.float32
BF16 = jnp.bfloat16

D_MODEL = 1024
N_BRANCH = 4
BRANCH_W = 512
DN_HEADS = 4
DN_DK = 128
DN_CHUNK = 64
CONV_W = 4
GM_GROUPS = 4
GM_CHUNK = 128
SW_HEADS = 8
SW_KV_HEADS = 2
SW_HD = 64
SW_BLOCK = 128
XM_HEADS = 4
XM_HD = 128
EPS = 1e-6
NEG_INF = -1e30

GATE_COLS = N_BRANCH * D_MODEL
OFF_GATES = 0
OFF_AQKV = GATE_COLS
OFF_AZ = OFF_AQKV + 3 * BRANCH_W
OFF_BUV = OFF_AZ + BRANCH_W
OFF_BZ = OFF_BUV + 2 * BRANCH_W
OFF_CQ = OFF_BZ + BRANCH_W
OFF_CZ = OFF_CQ + BRANCH_W
OFF_MQ = OFF_CZ + BRANCH_W
OFF_MZ = OFF_MQ + BRANCH_W
OFF_CKV = OFF_MZ + BRANCH_W
CKV_COLS = 2 * SW_KV_HEADS * SW_HD
MAIN_COLS = OFF_CKV + CKV_COLS
SMALL_COLS = 128

VMEM_LIMIT = 60 * 1024 * 1024
TT_LAYER = 256


def _sigmoid(x):
    return 1.0 / (1.0 + jnp.exp(-x))


def _silu(x):
    return x * _sigmoid(x)


def _softplus(x):
    return jnp.maximum(x, 0.0) + jnp.log(1.0 + jnp.exp(-jnp.abs(x)))


def _rms(x, gain):
    return x * lax.rsqrt(jnp.mean(x * x, axis=-1, keepdims=True) + EPS) * gain


def _dot(a, b):
    return jnp.dot(a, b, preferred_element_type=F32)


def _dot_nt(a, b):
    return lax.dot_general(a, b, (((1,), (1,)), ((), ())), preferred_element_type=F32)


def _memkv_kernel(m_ref, g_ref, w_ref, o_ref):
    o_ref[0] = _dot(_rms(m_ref[0], g_ref[...]).astype(BF16), w_ref[...]).astype(BF16)


def _memkv(mem, gain, w):
    b, ml, _ = mem.shape
    return pl.pallas_call(
        _memkv_kernel,
        grid=(b,),
        in_specs=[
            pl.BlockSpec((1, ml, D_MODEL), lambda i: (i, 0, 0)),
            pl.BlockSpec((1, D_MODEL), lambda i: (0, 0)),
            pl.BlockSpec((D_MODEL, 2 * BRANCH_W), lambda i: (0, 0)),
        ],
        out_specs=pl.BlockSpec((1, ml, 2 * BRANCH_W), lambda i: (i, 0, 0)),
        out_shape=jax.ShapeDtypeStruct((b, ml, 2 * BRANCH_W), BF16),
        compiler_params=pltpu.CompilerParams(dimension_semantics=("arbitrary",)),
        name="memkv",
    )(mem, gain, w)


def _deltanet_branch(qkv, z, sm, cw_ref, gp_ref, on_ref, y_ref, tail_ref, s_ref, *, tt):
    H, C, DK = DN_HEADS, DN_CHUNK, DN_DK
    HC = H * C
    cw = cw_ref[...]

    def conv_silu(idx):
        lo = idx * BRANCH_W
        cur = qkv[:, lo:lo + BRANCH_W]
        ext = jnp.concatenate([tail_ref[idx], cur], axis=0)
        acc = cur * cw[CONV_W - 1:CONV_W, lo:lo + BRANCH_W]
        for j in range(1, CONV_W):
            shifted = pltpu.roll(ext, j, axis=0)[8:]
            acc = acc + shifted * cw[CONV_W - 1 - j:CONV_W - j, lo:lo + BRANCH_W]
        tail_ref[idx] = cur[tt - 8:]
        return _silu(acc)

    qc = conv_silu(0)
    kc = conv_silu(1)
    vc = conv_silu(2)

    def l2n(xh):
        return xh * lax.rsqrt(jnp.sum(xh * xh, axis=-1, keepdims=True) + EPS)

    ci = lax.broadcasted_iota(jnp.int32, (C, HC), 0)
    cl = lax.broadcasted_iota(jnp.int32, (C, HC), 1)
    cj = cl % C
    strict = ci > cj
    incl = ci >= cj
    eye_cat = (ci == cj).astype(F32)
    lane_head = cl // C
    br = lax.broadcasted_iota(jnp.int32, (HC, HC), 0)
    bc = lax.broadcasted_iota(jnp.int32, (HC, HC), 1)
    bd_mask = jnp.where((br // C) == (bc // C), 1.0, 0.0).astype(BF16)
    kl = lax.broadcasted_iota(jnp.int32, (DK, HC), 1) // C
    head_lane_masks = [jnp.where(kl == h, 1.0, 0.0).astype(BF16) for h in range(H)]
    l64 = lax.broadcasted_iota(jnp.int32, (C, DK), 1) < C
    tr = lax.broadcasted_iota(jnp.int32, (C, C), 0)
    tc = lax.broadcasted_iota(jnp.int32, (C, C), 1)
    ltri = jnp.where(tr >= tc, 1.0, 0.0).astype(BF16)

    neg_a = gp_ref[0:1, :]
    dt_b = gp_ref[1:2, :]
    o_gain = on_ref[...]

    def stack(x, lo):
        return jnp.concatenate([x[lo:lo + C, h * DK:(h + 1) * DK] for h in range(H)], axis=0)

    def block_diag(x_cat):
        return jnp.concatenate([x_cat.astype(BF16)] * H, axis=0) * bd_mask

    def diag_to_cat(m):
        out = m[(H - 1) * C:H * C]
        for h in range(H - 2, -1, -1):
            out = jnp.where(lane_head == h, m[h * C:(h + 1) * C], out)
        return out

    for ci_ in range(tt // C):
        lo = ci_ * C
        qs = jnp.concatenate([l2n(qc[lo:lo + C, h * DK:(h + 1) * DK]) for h in range(H)], axis=0) * (DK ** -0.5)
        ks = jnp.concatenate([l2n(kc[lo:lo + C, h * DK:(h + 1) * DK]) for h in range(H)], axis=0)
        vs = stack(vc, lo)
        smc = sm[lo:lo + C, :]
        beta_all = _sigmoid(smc)
        g_all = neg_a * _softplus(smc + dt_b)
        g_hi = g_all.astype(BF16)
        r1 = g_all - g_hi.astype(F32)
        g_mid = r1.astype(BF16)
        g_lo = (r1 - g_mid.astype(F32)).astype(BF16)
        gc3 = _dot(ltri, jnp.concatenate([g_hi, g_mid, g_lo], axis=1))
        gc_all = gc3[:, :SMALL_COLS] + gc3[:, SMALL_COLS:2 * SMALL_COLS] + gc3[:, 2 * SMALL_COLS:]
        gcb = [jnp.broadcast_to(gc_all[:, H + h:H + h + 1], (C, DK)) for h in range(H)]
        cb = jnp.concatenate(gcb, axis=0)
        bb = jnp.concatenate(
            [jnp.broadcast_to(beta_all[:, h:h + 1], (C, DK)) for h in range(H)], axis=0)
        c_cat = jnp.concatenate([jnp.where(l64, gcb[0], gcb[1]), jnp.where(l64, gcb[2], gcb[3])], axis=1)
        gct = jnp.concatenate([gc_all, gc_all], axis=0).T
        r_cat = jnp.concatenate(
            [jnp.where(l64[0:1], gct[H:H + 1], gct[H + 1:H + 2]),
             jnp.where(l64[0:1], gct[H + 2:H + 3], gct[H + 3:H + 4])], axis=1)
        decay = jnp.exp(jnp.where(incl, c_cat - r_cat, 0.0))
        egc = jnp.exp(cb)
        glast = jnp.concatenate(
            [jnp.broadcast_to(cb[h * C + C - 1:h * C + C, :], (C, DK)) for h in range(H)], axis=0)

        kb = ks * bb
        kq = _dot_nt(jnp.concatenate([kb, qs], axis=0).astype(BF16), ks.astype(BF16))
        l_cat = jnp.where(strict, diag_to_cat(kq[:HC]) * decay, 0.0)
        a_cat = jnp.where(incl, diag_to_cat(kq[HC:]) * decay, 0.0)
        n_cat = -l_cat
        p_cat = eye_cat + n_cat
        cur = _dot(n_cat.astype(BF16), block_diag(n_cat))
        for _ in range(4):
            out = _dot(jnp.concatenate([cur, p_cat], axis=0).astype(BF16), block_diag(cur))
            p_cat = p_cat + out[C:]
            cur = out[:C]
        t_cat = p_cat + _dot(p_cat.astype(BF16), block_diag(cur))
        rhs = jnp.concatenate([vs * bb, kb * egc], axis=1)
        sol = _dot(block_diag(t_cat), rhs.astype(BF16))
        solb = sol.astype(BF16)
        auw = _dot(block_diag(a_cat), solb)
        qp = qs * egc - auw[:, DK:]
        kd_t = (ks * jnp.exp(glast - cb)).T.astype(BF16)

        for h in range(H):
            gb = _dot(kd_t * head_lane_masks[h], solb)
            s_h = s_ref[:, h * DK:(h + 1) * DK]
            gq = _dot(jnp.concatenate([gb[:, DK:], qp[h * C:(h + 1) * C]], axis=0).astype(BF16),
                      s_h.astype(BF16))
            d_h = jnp.exp(cb[h * C + C - 1:h * C + C, :])
            s_ref[:, h * DK:(h + 1) * DK] = s_h * d_h - gq[:DK] + gb[:, :DK]
            oh = _rms(gq[DK:] + auw[h * C:(h + 1) * C, :DK], o_gain)
            zh = z[lo:lo + C, h * DK:(h + 1) * DK]
            y_ref[lo:lo + C, h * DK:(h + 1) * DK] = (oh * _silu(zh)).astype(BF16)


def _gmlp_branch(uv, z, gain_ref, ws_ref, bias_ref, y_ref, *, tt):
    P, G = GM_CHUNK, GM_GROUPS
    u = jax.nn.gelu(uv[:, :BRANCH_W])
    v = _rms(jax.nn.gelu(uv[:, BRANCH_W:]), gain_ref[...])
    vb = v.astype(BF16)
    tr = lax.broadcasted_iota(jnp.int32, (P, P), 0)
    tc = lax.broadcasted_iota(jnp.int32, (P, P), 1)
    causal = tr >= tc
    bias = bias_ref[...]
    for g in range(G):
        wg = jnp.where(causal, ws_ref[g], 0.0).astype(BF16)
        for ci in range(tt // P):
            lo = ci * P
            s = _dot(wg, vb[lo:lo + P, g * P:(g + 1) * P]) + bias[:, g * P:(g + 1) * P]
            zg = z[lo:lo + P, g * P:(g + 1) * P]
            y_ref[lo:lo + P, g * P:(g + 1) * P] = (u[lo:lo + P, g * P:(g + 1) * P] * s * _silu(zg)).astype(BF16)


def _swa_branch(sinks_ref, q, z, kv_cur, y_ref, kvprev_ref, *, tt):
    P, HD, KV = SW_BLOCK, SW_HD, SW_KV_HEADS
    G = SW_HEADS // KV
    kj = lax.broadcasted_iota(jnp.int32, (2 * P, P), 0)
    qi = lax.broadcasted_iota(jnp.int32, (2 * P, P), 1)
    dist = qi + P - kj
    window = jnp.logical_and(dist >= 0, dist < P)
    valid_mid = jnp.concatenate([window] * G, axis=1)
    has_prev = pl.program_id(1) > 0
    valid_first = jnp.concatenate([jnp.logical_and(window, jnp.logical_or(kj >= P, has_prev))] * G, axis=1)
    kv = jnp.concatenate([kvprev_ref[...], kv_cur], axis=0)
    kvprev_ref[...] = kv_cur[tt - P:, :]
    qb = (q * (HD ** -0.5)).astype(BF16)
    for kh in range(KV):
        sink = jnp.concatenate([jnp.full((1, P), sinks_ref[kh * G + g], F32) for g in range(G)], axis=1)
        k_all = kv[:, kh * HD:(kh + 1) * HD].astype(BF16)
        v_all_t = kv[:, (KV + kh) * HD:(KV + kh + 1) * HD].T.astype(BF16)
        for j in range(tt // P):
            lo = j * P
            qs = jnp.concatenate(
                [qb[lo:lo + P, (kh * G + g) * HD:(kh * G + g + 1) * HD] for g in range(G)], axis=0)
            s = jnp.where(valid_first if j == 0 else valid_mid, _dot_nt(k_all[lo:lo + 2 * P], qs), NEG_INF)
            mx = jnp.maximum(jnp.max(s, axis=0, keepdims=True), sink)
            e = jnp.exp(s - mx)
            den = jnp.sum(e, axis=0, keepdims=True) + jnp.exp(sink - mx)
            o_t = _dot(v_all_t[:, lo:lo + 2 * P], e.astype(BF16)) * (1.0 / den)
            for g2 in range(G // 2):
                h0 = kh * G + 2 * g2
                pair = jnp.concatenate(
                    [o_t[:, 2 * g2 * P:(2 * g2 + 1) * P], o_t[:, (2 * g2 + 1) * P:(2 * g2 + 2) * P]], axis=0).T
                zp = z[lo:lo + P, h0 * HD:(h0 + 2) * HD]
                y_ref[lo:lo + P, h0 * HD:(h0 + 2) * HD] = (pair * _silu(zp)).astype(BF16)


def _memattn_branch(q, z, kv_ref, y_ref):
    HD = XM_HD
    qb = q.astype(BF16)
    for h in range(XM_HEADS):
        mk = kv_ref[0, :, h * HD:(h + 1) * HD]
        mv = kv_ref[0, :, BRANCH_W + h * HD:BRANCH_W + (h + 1) * HD]
        s = _dot_nt(qb[:, h * HD:(h + 1) * HD], mk) * (HD ** -0.5)
        e = jnp.exp(s - jnp.max(s, axis=-1, keepdims=True))
        o = _dot(e.astype(BF16), mv) * (1.0 / jnp.sum(e, axis=-1, keepdims=True))
        y_ref[:, h * HD:(h + 1) * HD] = (o * _silu(z[:, h * HD:(h + 1) * HD])).astype(BF16)


def _layer_kernel(sinks_ref, x_ref, npre_ref, w_ref, wsm_ref, cw_ref, gp_ref, on_ref, gmn_ref, ws_ref, bias_ref,
                  mkv_ref, wup_ref, wout_ref, npost_ref, o_ref, tail_ref, s_ref, kvprev_ref, y_ref, *, tt):
    @pl.when(pl.program_id(1) == 0)
    def _():
        tail_ref[...] = jnp.zeros_like(tail_ref)
        s_ref[...] = jnp.zeros_like(s_ref)
        kvprev_ref[...] = jnp.zeros_like(kvprev_ref)

    x = x_ref[...]
    h = _rms(x, npre_ref[...]).astype(BF16)

    def proj(lo, width):
        return _dot(h, w_ref[:, lo:lo + width])

    _deltanet_branch(proj(OFF_AQKV, 3 * BRANCH_W), proj(OFF_AZ, BRANCH_W), _dot(h, wsm_ref[...]),
                     cw_ref, gp_ref, on_ref, y_ref.at[0], tail_ref, s_ref, tt=tt)
    _gmlp_branch(proj(OFF_BUV, 2 * BRANCH_W), proj(OFF_BZ, BRANCH_W), gmn_ref, ws_ref, bias_ref, y_ref.at[1], tt=tt)
    _swa_branch(sinks_ref, proj(OFF_CQ, BRANCH_W), proj(OFF_CZ, BRANCH_W), proj(OFF_CKV, CKV_COLS),
                y_ref.at[2], kvprev_ref, tt=tt)
    _memattn_branch(proj(OFF_MQ, BRANCH_W), proj(OFF_MZ, BRANCH_W), mkv_ref, y_ref.at[3])

    merged = None
    for n in range(N_BRANCH):
        term = _sigmoid(proj(OFF_GATES + n * D_MODEL, D_MODEL)) * _dot(y_ref[n], wup_ref[n])
        merged = term if merged is None else merged + term
    out = _dot(merged.astype(BF16), wout_ref[...])
    o_ref[...] = x + _rms(out, npost_ref[...])


def _layer_call(x2, sinks, norm_pre, w_main, w_small, conv_w, gate_params, o_norm, gm_norm, spatial_w, bias_mat,
                mem_kv, w_up, w_out, norm_post, *, batch, seq, tt):
    nt = seq // tt
    ml = mem_kv.shape[1]
    row = lambda b, t, s: (b * nt + t, 0)

    def const(shape):
        return pl.BlockSpec(shape, lambda b, t, s: (0,) * len(shape), pipeline_mode=pl.Buffered(1))

    in_specs = [
        pl.BlockSpec((tt, D_MODEL), row),
        const((1, D_MODEL)), const((D_MODEL, MAIN_COLS)), const((D_MODEL, SMALL_COLS)),
        const((CONV_W, 3 * BRANCH_W)), const((8, SMALL_COLS)), const((1, DN_DK)), const((1, BRANCH_W)),
        const((GM_GROUPS, GM_CHUNK, GM_CHUNK)), const((GM_CHUNK, BRANCH_W)),
        pl.BlockSpec((1, ml, 2 * BRANCH_W), lambda b, t, s: (b, 0, 0)),
        const((N_BRANCH, BRANCH_W, D_MODEL)), const((D_MODEL, D_MODEL)), const((1, D_MODEL)),
    ]
    return pl.pallas_call(
        functools.partial(_layer_kernel, tt=tt),
        grid_spec=pltpu.PrefetchScalarGridSpec(
            num_scalar_prefetch=1,
            grid=(batch, nt),
            in_specs=in_specs,
            out_specs=pl.BlockSpec((tt, D_MODEL), row),
            scratch_shapes=[
                pltpu.VMEM((3, 8, BRANCH_W), F32),
                pltpu.VMEM((DN_DK, DN_HEADS * DN_DK), F32),
                pltpu.VMEM((SW_BLOCK, CKV_COLS), F32),
                pltpu.VMEM((N_BRANCH, tt, BRANCH_W), BF16),
            ],
        ),
        out_shape=jax.ShapeDtypeStruct((batch * seq, D_MODEL), F32),
        compiler_params=pltpu.CompilerParams(
            dimension_semantics=("arbitrary", "arbitrary"), vmem_limit_bytes=VMEM_LIMIT),
        name="layer",
    )(sinks, x2, norm_pre, w_main, w_small, conv_w, gate_params, o_norm, gm_norm, spatial_w, bias_mat, mem_kv,
      w_up, w_out, norm_post)


def _permute_w_in(w):
    a_end = 4 * BRANCH_W
    b_lo = a_end + 2 * DN_HEADS
    b_end = b_lo + 3 * BRANCH_W
    cq_end = b_end + BRANCH_W
    ckv_end = cq_end + CKV_COLS
    cz_end = ckv_end + BRANCH_W
    m_end = cz_end + 2 * BRANCH_W
    main = jnp.concatenate(
        [w[:, m_end:], w[:, :a_end], w[:, b_lo:b_end], w[:, b_end:cq_end], w[:, ckv_end:cz_end],
         w[:, cz_end:m_end], w[:, cq_end:ckv_end]], axis=1)
    small = jnp.pad(w[:, a_end:b_lo], ((0, 0), (0, SMALL_COLS - 2 * DN_HEADS)))
    return main.astype(BF16), small.astype(BF16)


def _layer(x2, mem, p, *, batch, seq):
    w_main, w_small = _permute_w_in(p["w_in"])
    mem_kv = _memkv(mem, p["norm_mem"][None], p["w_mem_kv"].astype(BF16))
    gate_params = jnp.zeros((8, SMALL_COLS), F32)
    gate_params = gate_params.at[0, DN_HEADS:2 * DN_HEADS].set(-jnp.exp(p["a_log"]))
    gate_params = gate_params.at[1, DN_HEADS:2 * DN_HEADS].set(p["dt_bias"])
    bias_mat = jnp.repeat(p["spatial_b"].T, GM_CHUNK, axis=1)
    return _layer_call(x2, p["sinks"], p["norm_pre"][None], w_main, w_small, p["conv_w"], gate_params,
                       p["dn_norm"][None], p["gm_norm"][None], p["spatial_w"], bias_mat, mem_kv,
                       p["w_up"].astype(BF16), p["w_out"].astype(BF16), p["norm_post"][None],
                       batch=batch, seq=seq, tt=TT_LAYER)


def kernel(x, mem, norm_pre, norm_post, norm_mem, w_in, conv_w, a_log, dt_bias, dn_norm, gm_norm,
           spatial_w, spatial_b, sinks, w_mem_kv, w_up, w_out):
    batch, seq, d = x.shape
    x2 = x.reshape(batch * seq, d)
    names = ("norm_pre", "norm_post", "norm_mem", "w_in", "conv_w", "a_log", "dt_bias", "dn_norm", "gm_norm",
             "spatial_w", "spatial_b", "sinks", "w_mem_kv", "w_up", "w_out")
    stacked = (norm_pre, norm_post, norm_mem, w_in, conv_w, a_log, dt_bias, dn_norm, gm_norm,
               spatial_w, spatial_b, sinks, w_mem_kv, w_up, w_out)
    for l in range(norm_pre.shape[0]):
        p = {n: a[l] for n, a in zip(names, stacked)}
        x2 = _layer(x2, mem, p, batch=batch, seq=seq)
    return x2.reshape(batch, seq, d)
```

```python
import functools

import jax
import jax.numpy as jnp
from jax import lax
from jax.experimental import pallas as pl
from jax.experimental.pallas import tpu as pltpu

F32 = jnp.float32
BF16 = jnp.bfloat16

D_MODEL = 1024
N_BRANCH = 4
BRANCH_W = 512
DN_HEADS = 4
DN_DK = 128
DN_CHUNK = 64
CONV_W = 4
GM_GROUPS = 4
GM_CHUNK = 128
SW_HEADS = 8
SW_KV_HEADS = 2
SW_HD = 64
SW_BLOCK = 128
XM_HEADS = 4
XM_HD = 128
EPS = 1e-6
NEG_INF = -1e30

GATE_COLS = N_BRANCH * D_MODEL
OFF_GATES = 0
OFF_AQKV = GATE_COLS
OFF_AZ = OFF_AQKV + 3 * BRANCH_W
OFF_BUV = OFF_AZ + BRANCH_W
OFF_BZ = OFF_BUV + 2 * BRANCH_W
OFF_CQ = OFF_BZ + BRANCH_W
OFF_CZ = OFF_CQ + BRANCH_W
OFF_MQ = OFF_CZ + BRANCH_W
OFF_MZ = OFF_MQ + BRANCH_W
OFF_CKV = OFF_MZ + BRANCH_W
CKV_COLS = 2 * SW_KV_HEADS * SW_HD
MAIN_COLS = OFF_CKV + CKV_COLS
SMALL_COLS = 128

VMEM_LIMIT = 60 * 1024 * 1024
TT_LAYER = 256


def _sigmoid(x):
    return 1.0 / (1.0 + jnp.exp(-x))


def _silu(x):
    return x * _sigmoid(x)


def _softplus(x):
    return jnp.maximum(x, 0.0) + jnp.log(1.0 + jnp.exp(-jnp.abs(x)))


def _rms(x, gain):
    return x * lax.rsqrt(jnp.mean(x * x, axis=-1, keepdims=True) + EPS) * gain


def _dot(a, b):
    return jnp.dot(a, b, preferred_element_type=F32)


def _dot_nt(a, b):
    return lax.dot_general(a, b, (((1,), (1,)), ((), ())), preferred_element_type=F32)


def _memkv_kernel(m_ref, g_ref, w_ref, o_ref):
    o_ref[0] = _dot(_rms(m_ref[0], g_ref[...]).astype(BF16), w_ref[...]).astype(BF16)


def _memkv(mem, gain, w):
    b, ml, _ = mem.shape
    return pl.pallas_call(
        _memkv_kernel,
        grid=(b,),
        in_specs=[
            pl.BlockSpec((1, ml, D_MODEL), lambda i: (i, 0, 0)),
            pl.BlockSpec((1, D_MODEL), lambda i: (0, 0)),
            pl.BlockSpec((D_MODEL, 2 * BRANCH_W), lambda i: (0, 0)),
        ],
        out_specs=pl.BlockSpec((1, ml, 2 * BRANCH_W), lambda i: (i, 0, 0)),
        out_shape=jax.ShapeDtypeStruct((b, ml, 2 * BRANCH_W), BF16),
        compiler_params=pltpu.CompilerParams(dimension_semantics=("arbitrary",)),
        name="memkv",
    )(mem, gain, w)


def _deltanet_branch(qkv, z, sm, cw_ref, gp_ref, on_ref, y_ref, tail_ref, s_ref, *, tt):
    H, C, DK = DN_HEADS, DN_CHUNK, DN_DK
    HC = H * C
    cw = cw_ref[...]

    def conv_silu(idx):
        lo = idx * BRANCH_W
        cur = qkv[:, lo:lo + BRANCH_W]
        ext = jnp.concatenate([tail_ref[idx], cur], axis=0)
        acc = cur * cw[CONV_W - 1:CONV_W, lo:lo + BRANCH_W]
        for j in range(1, CONV_W):
            shifted = pltpu.roll(ext, j, axis=0)[8:]
            acc = acc + shifted * cw[CONV_W - 1 - j:CONV_W - j, lo:lo + BRANCH_W]
        tail_ref[idx] = cur[tt - 8:]
        return _silu(acc)

    qc = conv_silu(0)
    kc = conv_silu(1)
    vc = conv_silu(2)

    def l2n(xh):
        return xh * lax.rsqrt(jnp.sum(xh * xh, axis=-1, keepdims=True) + EPS)

    ci = lax.broadcasted_iota(jnp.int32, (C, HC), 0)
    cl = lax.broadcasted_iota(jnp.int32, (C, HC), 1)
    cj = cl % C
    strict = ci > cj
    incl = ci >= cj
    eye_cat = (ci == cj).astype(F32)
    lane_head = cl // C
    br = lax.broadcasted_iota(jnp.int32, (HC, HC), 0)
    bc = lax.broadcasted_iota(jnp.int32, (HC, HC), 1)
    bd_mask = jnp.where((br // C) == (bc // C), 1.0, 0.0).astype(BF16)
    kl = lax.broadcasted_iota(jnp.int32, (DK, HC), 1) // C
    head_lane_masks = [jnp.where(kl == h, 1.0, 0.0).astype(BF16) for h in range(H)]
    l64 = lax.broadcasted_iota(jnp.int32, (C, DK), 1) < C
    tr = lax.broadcasted_iota(jnp.int32, (C, C), 0)
    tc = lax.broadcasted_iota(jnp.int32, (C, C), 1)
    ltri = jnp.where(tr >= tc, 1.0, 0.0).astype(BF16)

    neg_a = gp_ref[0:1, :]
    dt_b = gp_ref[1:2, :]
    o_gain = on_ref[...]

    def stack(x, lo):
        return jnp.concatenate([x[lo:lo + C, h * DK:(h + 1) * DK] for h in range(H)], axis=0)

    def block_diag(x_cat):
        return jnp.concatenate([x_cat.astype(BF16)] * H, axis=0) * bd_mask

    def diag_to_cat(m):
        out = m[(H - 1) * C:H * C]
        for h in range(H - 2, -1, -1):
            out = jnp.where(lane_head == h, m[h * C:(h + 1) * C], out)
        return out

    chunks = range(tt // C)
    qs, ks, vs, beta_all, gc3 = [], [], [], [], []
    for c in chunks:
        lo = c * C
        qs.append(jnp.concatenate([l2n(qc[lo:lo + C, h * DK:(h + 1) * DK]) for h in range(H)], axis=0) * (DK ** -0.5))
        ks.append(jnp.concatenate([l2n(kc[lo:lo + C, h * DK:(h + 1) * DK]) for h in range(H)], axis=0))
        vs.append(stack(vc, lo))
        smc = sm[lo:lo + C, :]
        beta_all.append(_sigmoid(smc))
        g_all = neg_a * _softplus(smc + dt_b)
        g_hi = g_all.astype(BF16)
        r1 = g_all - g_hi.astype(F32)
        g_mid = r1.astype(BF16)
        g_lo = (r1 - g_mid.astype(F32)).astype(BF16)
        gc3.append(_dot(ltri, jnp.concatenate([g_hi, g_mid, g_lo], axis=1)))

    cb, bb, decay, egc, kb, kq = [], [], [], [], [], []
    for c in chunks:
        gc_all = gc3[c][:, :SMALL_COLS] + gc3[c][:, SMALL_COLS:2 * SMALL_COLS] + gc3[c][:, 2 * SMALL_COLS:]
        gcb = [jnp.broadcast_to(gc_all[:, H + h:H + h + 1], (C, DK)) for h in range(H)]
        cb.append(jnp.concatenate(gcb, axis=0))
        bb.append(jnp.concatenate(
            [jnp.broadcast_to(beta_all[c][:, h:h + 1], (C, DK)) for h in range(H)], axis=0))
        c_cat = jnp.concatenate([jnp.where(l64, gcb[0], gcb[1]), jnp.where(l64, gcb[2], gcb[3])], axis=1)
        gct = jnp.concatenate([gc_all, gc_all], axis=0).T
        r_cat = jnp.concatenate(
            [jnp.where(l64[0:1], gct[H:H + 1], gct[H + 1:H + 2]),
             jnp.where(l64[0:1], gct[H + 2:H + 3], gct[H + 3:H + 4])], axis=1)
        decay.append(jnp.exp(jnp.where(incl, c_cat - r_cat, 0.0)))
        egc.append(jnp.exp(cb[c]))
        kb.append(ks[c] * bb[c])
        kq.append(_dot_nt(jnp.concatenate([kb[c], qs[c]], axis=0).astype(BF16), ks[c].astype(BF16)))

    a_cat, p_cat, cur = [], [], []
    for c in chunks:
        n_cat = -jnp.where(strict, diag_to_cat(kq[c][:HC]) * decay[c], 0.0)
        a_cat.append(jnp.where(incl, diag_to_cat(kq[c][HC:]) * decay[c], 0.0))
        p_cat.append(eye_cat + n_cat)
        cur.append(_dot(n_cat.astype(BF16), block_diag(n_cat)))
    for _ in range(4):
        for c in chunks:
            out = _dot(jnp.concatenate([cur[c], p_cat[c]], axis=0).astype(BF16), block_diag(cur[c]))
            p_cat[c] = p_cat[c] + out[C:]
            cur[c] = out[:C]
    t_cat = [p_cat[c] + _dot(p_cat[c].astype(BF16), block_diag(cur[c])) for c in chunks]
    solb = []
    for c in chunks:
        rhs = jnp.concatenate([vs[c] * bb[c], kb[c] * egc[c]], axis=1)
        solb.append(_dot(block_diag(t_cat[c]), rhs.astype(BF16)).astype(BF16))
    auw, qp, gb = [], [], []
    for c in chunks:
        auw.append(_dot(block_diag(a_cat[c]), solb[c]))
        qp.append(qs[c] * egc[c] - auw[c][:, DK:])
        glast = jnp.concatenate(
            [jnp.broadcast_to(cb[c][h * C + C - 1:h * C + C, :], (C, DK)) for h in range(H)], axis=0)
        kd_t = (ks[c] * jnp.exp(glast - cb[c])).T.astype(BF16)
        gb.append([_dot(kd_t * head_lane_masks[h], solb[c]) for h in range(H)])

    for c in chunks:
        lo = c * C
        for h in range(H):
            s_h = s_ref[:, h * DK:(h + 1) * DK]
            gq = _dot(jnp.concatenate([gb[c][h][:, DK:], qp[c][h * C:(h + 1) * C]], axis=0).astype(BF16),
                      s_h.astype(BF16))
            d_h = jnp.exp(cb[c][h * C + C - 1:h * C + C, :])
            s_ref[:, h * DK:(h + 1) * DK] = s_h * d_h - gq[:DK] + gb[c][h][:, :DK]
            oh = _rms(gq[DK:] + auw[c][h * C:(h + 1) * C, :DK], o_gain)
            zh = z[lo:lo + C, h * DK:(h + 1) * DK]
            y_ref[lo:lo + C, h * DK:(h + 1) * DK] = (oh * _silu(zh)).astype(BF16)


def _gmlp_branch(uv, z, gain_ref, ws_ref, bias_ref, y_ref, *, tt):
    P, G = GM_CHUNK, GM_GROUPS
    u = jax.nn.gelu(uv[:, :BRANCH_W])
    v = _rms(jax.nn.gelu(uv[:, BRANCH_W:]), gain_ref[...])
    vb = v.astype(BF16)
    tr = lax.broadcasted_iota(jnp.int32, (P, P), 0)
    tc = lax.broadcasted_iota(jnp.int32, (P, P), 1)
    causal = tr >= tc
    bias = bias_ref[...]
    for g in range(G):
        wg = jnp.where(causal, ws_ref[g], 0.0).astype(BF16)
        for ci in range(tt // P):
            lo = ci * P
            s = _dot(wg, vb[lo:lo + P, g * P:(g + 1) * P]) + bias[:, g * P:(g + 1) * P]
            zg = z[lo:lo + P, g * P:(g + 1) * P]
            y_ref[lo:lo + P, g * P:(g + 1) * P] = (u[lo:lo + P, g * P:(g + 1) * P] * s * _silu(zg)).astype(BF16)


def _swa_branch(sinks_ref, q, z, kv_cur, y_ref, kvprev_ref, *, tt):
    P, HD, KV = SW_BLOCK, SW_HD, SW_KV_HEADS
    G = SW_HEADS // KV
    kj = lax.broadcasted_iota(jnp.int32, (2 * P, P), 0)
    qi = lax.broadcasted_iota(jnp.int32, (2 * P, P), 1)
    dist = qi + P - kj
    window = jnp.logical_and(dist >= 0, dist < P)
    valid_mid = jnp.concatenate([window] * G, axis=1)
    has_prev = pl.program_id(1) > 0
    valid_first = jnp.concatenate([jnp.logical_and(window, jnp.logical_or(kj >= P, has_prev))] * G, axis=1)
    kv = jnp.concatenate([kvprev_ref[...], kv_cur], axis=0)
    kvprev_ref[...] = kv_cur[tt - P:, :]
    qb = (q * (HD ** -0.5)).astype(BF16)
    for kh in range(KV):
        sink = jnp.concatenate([jnp.full((1, P), sinks_ref[kh * G + g], F32) for g in range(G)], axis=1)
        k_all = kv[:, kh * HD:(kh + 1) * HD].astype(BF16)
        v_all_t = kv[:, (KV + kh) * HD:(KV + kh + 1) * HD].T.astype(BF16)
        for j in range(tt // P):
            lo = j * P
            qs = jnp.concatenate(
                [qb[lo:lo + P, (kh * G + g) * HD:(kh * G + g + 1) * HD] for g in range(G)], axis=0)
            s = jnp.where(valid_first if j == 0 else valid_mid, _dot_nt(k_all[lo:lo + 2 * P], qs), NEG_INF)
            mx = jnp.maximum(jnp.max(s, axis=0, keepdims=True), sink)
            e = jnp.exp(s - mx)
            den = jnp.sum(e, axis=0, keepdims=True) + jnp.exp(sink - mx)
            o_t = _dot(v_all_t[:, lo:lo + 2 * P], e.astype(BF16)) * (1.0 / den)
            for g2 in range(G // 2):
                h0 = kh * G + 2 * g2
                pair = jnp.concatenate(
                    [o_t[:, 2 * g2 * P:(2 * g2 + 1) * P], o_t[:, (2 * g2 + 1) * P:(2 * g2 + 2) * P]], axis=0).T
                zp = z[lo:lo + P, h0 * HD:(h0 + 2) * HD]
                y_ref[lo:lo + P, h0 * HD:(h0 + 2) * HD] = (pair * _silu(zp)).astype(BF16)


def _memattn_branch(q, z, kv_ref, y_ref):
    HD = XM_HD
    qb = q.astype(BF16)
    for h in range(XM_HEADS):
        mk = kv_ref[0, :, h * HD:(h + 1) * HD]
        mv = kv_ref[0, :, BRANCH_W + h * HD:BRANCH_W + (h + 1) * HD]
        s = _dot_nt(qb[:, h * HD:(h + 1) * HD], mk) * (HD ** -0.5)
        e = jnp.exp(s - jnp.max(s, axis=-1, keepdims=True))
        o = _dot(e.astype(BF16), mv) * (1.0 / jnp.sum(e, axis=-1, keepdims=True))
        y_ref[:, h * HD:(h + 1) * HD] = (o * _silu(z[:, h * HD:(h + 1) * HD])).astype(BF16)


def _layer_kernel(sinks_ref, x_ref, npre_ref, w_ref, wsm_ref, cw_ref, gp_ref, on_ref, gmn_ref, ws_ref, bias_ref,
                  mkv_ref, wup_ref, wout_ref, npost_ref, o_ref, tail_ref, s_ref, kvprev_ref, y_ref, *, tt):
    @pl.when(pl.program_id(1) == 0)
    def _():
        tail_ref[...] = jnp.zeros_like(tail_ref)
        s_ref[...] = jnp.zeros_like(s_ref)
        kvprev_ref[...] = jnp.zeros_like(kvprev_ref)

    x = x_ref[...]
    h = _rms(x, npre_ref[...]).astype(BF16)

    def proj(lo, width):
        return _dot(h, w_ref[:, lo:lo + width])

    _deltanet_branch(proj(OFF_AQKV, 3 * BRANCH_W), proj(OFF_AZ, BRANCH_W), _dot(h, wsm_ref[...]),
                     cw_ref, gp_ref, on_ref, y_ref.at[0], tail_ref, s_ref, tt=tt)
    _gmlp_branch(proj(OFF_BUV, 2 * BRANCH_W), proj(OFF_BZ, BRANCH_W), gmn_ref, ws_ref, bias_ref, y_ref.at[1], tt=tt)
    _swa_branch(sinks_ref, proj(OFF_CQ, BRANCH_W), proj(OFF_CZ, BRANCH_W), proj(OFF_CKV, CKV_COLS),
                y_ref.at[2], kvprev_ref, tt=tt)
    _memattn_branch(proj(OFF_MQ, BRANCH_W), proj(OFF_MZ, BRANCH_W), mkv_ref, y_ref.at[3])

    merged = None
    for n in range(N_BRANCH):
        term = _sigmoid(proj(OFF_GATES + n * D_MODEL, D_MODEL)) * _dot(y_ref[n], wup_ref[n])
        merged = term if merged is None else merged + term
    out = _dot(merged.astype(BF16), wout_ref[...])
    o_ref[...] = x + _rms(out, npost_ref[...])


def _layer_call(x2, sinks, norm_pre, w_main, w_small, conv_w, gate_params, o_norm, gm_norm, spatial_w, bias_mat,
                mem_kv, w_up, w_out, norm_post, *, batch, seq, tt):
    nt = seq // tt
    ml = mem_kv.shape[1]
    row = lambda b, t, s: (b * nt + t, 0)

    def const(shape):
        return pl.BlockSpec(shape, lambda b, t, s: (0,) * len(shape), pipeline_mode=pl.Buffered(1))

    in_specs = [
        pl.BlockSpec((tt, D_MODEL), row),
        const((1, D_MODEL)), const((D_MODEL, MAIN_COLS)), const((D_MODEL, SMALL_COLS)),
        const((CONV_W, 3 * BRANCH_W)), const((8, SMALL_COLS)), const((1, DN_DK)), const((1, BRANCH_W)),
        const((GM_GROUPS, GM_CHUNK, GM_CHUNK)), const((GM_CHUNK, BRANCH_W)),
        pl.BlockSpec((1, ml, 2 * BRANCH_W), lambda b, t, s: (b, 0, 0)),
        const((N_BRANCH, BRANCH_W, D_MODEL)), const((D_MODEL, D_MODEL)), const((1, D_MODEL)),
    ]
    return pl.pallas_call(
        functools.partial(_layer_kernel, tt=tt),
        grid_spec=pltpu.PrefetchScalarGridSpec(
            num_scalar_prefetch=1,
            grid=(batch, nt),
            in_specs=in_specs,
            out_specs=pl.BlockSpec((tt, D_MODEL), row),
            scratch_shapes=[
                pltpu.VMEM((3, 8, BRANCH_W), F32),
                pltpu.VMEM((DN_DK, DN_HEADS * DN_DK), F32),
                pltpu.VMEM((SW_BLOCK, CKV_COLS), F32),
                pltpu.VMEM((N_BRANCH, tt, BRANCH_W), BF16),
            ],
        ),
        out_shape=jax.ShapeDtypeStruct((batch * seq, D_MODEL), F32),
        compiler_params=pltpu.CompilerParams(
            dimension_semantics=("arbitrary", "arbitrary"), vmem_limit_bytes=VMEM_LIMIT),
        name="layer",
    )(sinks, x2, norm_pre, w_main, w_small, conv_w, gate_params, o_norm, gm_norm, spatial_w, bias_mat, mem_kv,
      w_up, w_out, norm_post)


def _permute_w_in(w):
    a_end = 4 * BRANCH_W
    b_lo = a_end + 2 * DN_HEADS
    b_end = b_lo + 3 * BRANCH_W
    cq_end = b_end + BRANCH_W
    ckv_end = cq_end + CKV_COLS
    cz_end = ckv_end + BRANCH_W
    m_end = cz_end + 2 * BRANCH_W
    main = jnp.concatenate(
        [w[:, m_end:], w[:, :a_end], w[:, b_lo:b_end], w[:, b_end:cq_end], w[:, ckv_end:cz_end],
         w[:, cz_end:m_end], w[:, cq_end:ckv_end]], axis=1)
    small = jnp.pad(w[:, a_end:b_lo], ((0, 0), (0, SMALL_COLS - 2 * DN_HEADS)))
    return main.astype(BF16), small.astype(BF16)


def _layer(x2, mem, p, *, batch, seq):
    w_main, w_small = _permute_w_in(p["w_in"])
    mem_kv = _memkv(mem, p["norm_mem"][None], p["w_mem_kv"].astype(BF16))
    gate_params = jnp.zeros((8, SMALL_COLS), F32)
    gate_params = gate_params.at[0, DN_HEADS:2 * DN_HEADS].set(-jnp.exp(p["a_log"]))
    gate_params = gate_params.at[1, DN_HEADS:2 * DN_HEADS].set(p["dt_bias"])
    bias_mat = jnp.repeat(p["spatial_b"].T, GM_CHUNK, axis=1)
    return _layer_call(x2, p["sinks"], p["norm_pre"][None], w_main, w_small, p["conv_w"], gate_params,
                       p["dn_norm"][None], p["gm_norm"][None], p["spatial_w"], bias_mat, mem_kv,
                       p["w_up"].astype(BF16), p["w_out"].astype(BF16), p["norm_post"][None],
                       batch=batch, seq=seq, tt=TT_LAYER)


def kernel(x, mem, norm_pre, norm_post, norm_mem, w_in, conv_w, a_log, dt_bias, dn_norm, gm_norm,
           spatial_w, spatial_b, sinks, w_mem_kv, w_up, w_out):
    batch, seq, d = x.shape
    x2 = x.reshape(batch * seq, d)
    names = ("norm_pre", "norm_post", "norm_mem", "w_in", "conv_w", "a_log", "dt_bias", "dn_norm", "gm_norm",
             "spatial_w", "spatial_b", "sinks", "w_mem_kv", "w_up", "w_out")
    stacked = (norm_pre, norm_post, norm_mem, w_in, conv_w, a_log, dt_bias, dn_norm, gm_norm,
               spatial_w, spatial_b, sinks, w_mem_kv, w_up, w_out)
    for l in range(norm_pre.shape[0]):
        p = {n: a[l] for n, a in zip(names, stacked)}
        x2 = _layer(x2, mem, p, batch=batch, seq=seq)
    return x2.reshape(batch, seq, d)
```

```python
import functools

import jax
import jax.numpy as jnp
from jax import lax
from jax.experimental import pallas as pl
from jax.experimental.pallas import tpu as pltpu

F32 = jnp.float32
BF16 = jnp.bfloat16

D_MODEL = 1024
N_BRANCH = 4
BRANCH_W = 512
DN_HEADS = 4
DN_DK = 128
DN_CHUNK = 64
CONV_W = 4
GM_GROUPS = 4
GM_CHUNK = 128
SW_HEADS = 8
SW_KV_HEADS = 2
SW_HD = 64
SW_BLOCK = 128
XM_HEADS = 4
XM_HD = 128
EPS = 1e-6
NEG_INF = -1e30
LANES = 128

CKV_COLS = 2 * SW_KV_HEADS * SW_HD
OFF_AQKV = 0
OFF_AZ = OFF_AQKV + 3 * BRANCH_W
OFF_SMALL = OFF_AZ + BRANCH_W
N_SMALL = 2 * DN_HEADS
OFF_BUV = OFF_SMALL + LANES
OFF_BZ = OFF_BUV + 2 * BRANCH_W
OFF_CQ = OFF_BZ + BRANCH_W
OFF_CKV = OFF_CQ + BRANCH_W
OFF_CZ = OFF_CKV + CKV_COLS
OFF_MQ = OFF_CZ + BRANCH_W
OFF_MZ = OFF_MQ + BRANCH_W
OFF_GATES = OFF_MZ + BRANCH_W
W_COLS = OFF_GATES + N_BRANCH * D_MODEL

VMEM_LIMIT = 60 * 1024 * 1024
TT_LAYER = 256


def _sigmoid(x):
    return 1.0 / (1.0 + jnp.exp(-x))


def _silu(x):
    return x * _sigmoid(x)


def _softplus(x):
    return jnp.maximum(x, 0.0) + jnp.log(1.0 + jnp.exp(-jnp.abs(x)))


def _rms(x, gain):
    return x * lax.rsqrt(jnp.mean(x * x, axis=-1, keepdims=True) + EPS) * gain


def _dot(a, b):
    return jnp.dot(a, b, preferred_element_type=F32)


def _dot_nt(a, b):
    return lax.dot_general(a, b, (((1,), (1,)), ((), ())), preferred_element_type=F32)


def _memkv_kernel(m_ref, g_ref, w_ref, o_ref):
    o_ref[...] = _dot(_rms(m_ref[...], g_ref[...]).astype(BF16), w_ref[...]).astype(BF16)


def _memkv(mem, gain, w):
    b, ml, _ = mem.shape
    nl = w.shape[0]
    return pl.pallas_call(
        _memkv_kernel,
        grid=(nl, b),
        in_specs=[
            pl.BlockSpec((None, ml, D_MODEL), lambda l, i: (i, 0, 0)),
            pl.BlockSpec((None, 1, D_MODEL), lambda l, i: (l, 0, 0)),
            pl.BlockSpec((None, D_MODEL, 2 * BRANCH_W), lambda l, i: (l, 0, 0)),
        ],
        out_specs=pl.BlockSpec((None, None, ml, 2 * BRANCH_W), lambda l, i: (l, i, 0, 0)),
        out_shape=jax.ShapeDtypeStruct((nl, b, ml, 2 * BRANCH_W), BF16),
        compiler_params=pltpu.CompilerParams(dimension_semantics=("arbitrary", "arbitrary")),
        name="memkv",
    )(mem, gain, w)


def _deltanet_stages(qkv, z, sm, cw_ref, gp_ref, on_ref, y_ref, tail_ref, s_ref, *, tt):
    H, C, DK = DN_HEADS, DN_CHUNK, DN_DK
    HC = H * C
    cw = cw_ref[...]

    def conv_silu(idx):
        lo = idx * BRANCH_W
        cur = qkv[:, lo:lo + BRANCH_W]
        ext = jnp.concatenate([tail_ref[idx], cur], axis=0)
        acc = cur * cw[CONV_W - 1:CONV_W, lo:lo + BRANCH_W]
        for j in range(1, CONV_W):
            shifted = pltpu.roll(ext, j, axis=0)[8:]
            acc = acc + shifted * cw[CONV_W - 1 - j:CONV_W - j, lo:lo + BRANCH_W]
        tail_ref[idx] = cur[tt - 8:]
        return _silu(acc)

    qc = conv_silu(0)
    kc = conv_silu(1)
    vc = conv_silu(2)
    yield

    def l2n(xh):
        return xh * lax.rsqrt(jnp.sum(xh * xh, axis=-1, keepdims=True) + EPS)

    ci = lax.broadcasted_iota(jnp.int32, (C, HC), 0)
    cl = lax.broadcasted_iota(jnp.int32, (C, HC), 1)
    cj = cl % C
    strict = ci > cj
    incl = ci >= cj
    eye_cat = (ci == cj).astype(F32)
    lane_head = cl // C
    br = lax.broadcasted_iota(jnp.int32, (HC, HC), 0)
    bc = lax.broadcasted_iota(jnp.int32, (HC, HC), 1)
    bd_mask = jnp.where((br // C) == (bc // C), 1.0, 0.0).astype(BF16)
    kl = lax.broadcasted_iota(jnp.int32, (DK, HC), 1) // C
    head_lane_masks = [jnp.where(kl == h, 1.0, 0.0).astype(BF16) for h in range(H)]
    l64 = lax.broadcasted_iota(jnp.int32, (C, DK), 1) < C
    tr = lax.broadcasted_iota(jnp.int32, (C, C), 0)
    tc = lax.broadcasted_iota(jnp.int32, (C, C), 1)
    ltri = jnp.where(tr >= tc, 1.0, 0.0).astype(BF16)

    neg_a = gp_ref[0:1, :]
    dt_b = gp_ref[1:2, :]
    o_gain = on_ref[...]

    def stack(x, lo):
        return jnp.concatenate([x[lo:lo + C, h * DK:(h + 1) * DK] for h in range(H)], axis=0)

    def block_diag(x_cat):
        return jnp.concatenate([x_cat.astype(BF16)] * H, axis=0) * bd_mask

    def diag_to_cat(m):
        out = m[(H - 1) * C:H * C]
        for h in range(H - 2, -1, -1):
            out = jnp.where(lane_head == h, m[h * C:(h + 1) * C], out)
        return out

    chunks = range(tt // C)
    qs, ks, vs, beta_all, gc3 = [], [], [], [], []
    for c in chunks:
        lo = c * C
        qs.append(jnp.concatenate([l2n(qc[lo:lo + C, h * DK:(h + 1) * DK]) for h in range(H)], axis=0) * (DK ** -0.5))
        ks.append(jnp.concatenate([l2n(kc[lo:lo + C, h * DK:(h + 1) * DK]) for h in range(H)], axis=0))
        vs.append(stack(vc, lo))
        smc = sm[lo:lo + C, :]
        beta_all.append(_sigmoid(smc))
        g_all = neg_a * _softplus(smc + dt_b)
        g_hi = g_all.astype(BF16)
        r1 = g_all - g_hi.astype(F32)
        g_mid = r1.astype(BF16)
        g_lo = (r1 - g_mid.astype(F32)).astype(BF16)
        gc3.append(_dot(ltri, jnp.concatenate([g_hi, g_mid, g_lo], axis=1)))
    yield

    cb, bb, decay, egc, kb, kq = [], [], [], [], [], []
    for c in chunks:
        gc_all = gc3[c][:, :LANES] + gc3[c][:, LANES:2 * LANES] + gc3[c][:, 2 * LANES:]
        gcb = [jnp.broadcast_to(gc_all[:, H + h:H + h + 1], (C, DK)) for h in range(H)]
        cb.append(jnp.concatenate(gcb, axis=0))
        bb.append(jnp.concatenate(
            [jnp.broadcast_to(beta_all[c][:, h:h + 1], (C, DK)) for h in range(H)], axis=0))
        c_cat = jnp.concatenate([jnp.where(l64, gcb[0], gcb[1]), jnp.where(l64, gcb[2], gcb[3])], axis=1)
        gct = jnp.concatenate([gc_all, gc_all], axis=0).T
        r_cat = jnp.concatenate(
            [jnp.where(l64[0:1], gct[H:H + 1], gct[H + 1:H + 2]),
             jnp.where(l64[0:1], gct[H + 2:H + 3], gct[H + 3:H + 4])], axis=1)
        decay.append(jnp.exp(jnp.where(incl, c_cat - r_cat, 0.0)))
        egc.append(jnp.exp(cb[c]))
        kb.append(ks[c] * bb[c])
        kq.append(_dot_nt(jnp.concatenate([kb[c], qs[c]], axis=0).astype(BF16), ks[c].astype(BF16)))
    yield

    a_cat, p_cat, cur = [], [], []
    for c in chunks:
        n_cat = -jnp.where(strict, diag_to_cat(kq[c][:HC]) * decay[c], 0.0)
        a_cat.append(jnp.where(incl, diag_to_cat(kq[c][HC:]) * decay[c], 0.0))
        p_cat.append(eye_cat + n_cat)
        cur.append(_dot(n_cat.astype(BF16), block_diag(n_cat)))
    yield
    for _ in range(4):
        for c in chunks:
            out = _dot(jnp.concatenate([cur[c], p_cat[c]], axis=0).astype(BF16), block_diag(cur[c]))
            p_cat[c] = p_cat[c] + out[C:]
            cur[c] = out[:C]
        yield
    t_cat = [p_cat[c] + _dot(p_cat[c].astype(BF16), block_diag(cur[c])) for c in chunks]
    yield
    solb = []
    for c in chunks:
        rhs = jnp.concatenate([vs[c] * bb[c], kb[c] * egc[c]], axis=1)
        solb.append(_dot(block_diag(t_cat[c]), rhs.astype(BF16)).astype(BF16))
    yield
    auw, qp, gb = [], [], []
    for c in chunks:
        auw.append(_dot(block_diag(a_cat[c]), solb[c]))
        qp.append(qs[c] * egc[c] - auw[c][:, DK:])
        glast = jnp.concatenate(
            [jnp.broadcast_to(cb[c][h * C + C - 1:h * C + C, :], (C, DK)) for h in range(H)], axis=0)
        kd_t = (ks[c] * jnp.exp(glast - cb[c])).T.astype(BF16)
        gb.append([_dot(kd_t * head_lane_masks[h], solb[c]) for h in range(H)])
    yield

    for c in chunks:
        lo = c * C
        for h in range(H):
            s_h = s_ref[:, h * DK:(h + 1) * DK]
            gq = _dot(jnp.concatenate([gb[c][h][:, DK:], qp[c][h * C:(h + 1) * C]], axis=0).astype(BF16),
                      s_h.astype(BF16))
            d_h = jnp.exp(cb[c][h * C + C - 1:h * C + C, :])
            s_ref[:, h * DK:(h + 1) * DK] = s_h * d_h - gq[:DK] + gb[c][h][:, :DK]
            oh = _rms(gq[DK:] + auw[c][h * C:(h + 1) * C, :DK], o_gain)
            zh = z[lo:lo + C, h * DK:(h + 1) * DK]
            y_ref[lo:lo + C, h * DK:(h + 1) * DK] = (oh * _silu(zh)).astype(BF16)
        yield


def _gmlp_branch(uv, z, gain_ref, ws_ref, bias_ref, y_ref, *, tt):
    P, G = GM_CHUNK, GM_GROUPS
    u = jax.nn.gelu(uv[:, :BRANCH_W])
    v = _rms(jax.nn.gelu(uv[:, BRANCH_W:]), gain_ref[...])
    vb = v.astype(BF16)
    tr = lax.broadcasted_iota(jnp.int32, (P, P), 0)
    tc = lax.broadcasted_iota(jnp.int32, (P, P), 1)
    causal = tr >= tc
    bias = bias_ref[...]
    for g in range(G):
        wg = jnp.where(causal, ws_ref[g], 0.0).astype(BF16)
        for ci in range(tt // P):
            lo = ci * P
            s = _dot(wg, vb[lo:lo + P, g * P:(g + 1) * P]) + bias[:, g * P:(g + 1) * P]
            zg = z[lo:lo + P, g * P:(g + 1) * P]
            y_ref[lo:lo + P, g * P:(g + 1) * P] = (u[lo:lo + P, g * P:(g + 1) * P] * s * _silu(zg)).astype(BF16)


def _swa_branch(sink_of, q, z, kv_cur, y_ref, kvprev_ref, *, tt):
    P, HD, KV = SW_BLOCK, SW_HD, SW_KV_HEADS
    G = SW_HEADS // KV
    kj = lax.broadcasted_iota(jnp.int32, (2 * P, P), 0)
    qi = lax.broadcasted_iota(jnp.int32, (2 * P, P), 1)
    dist = qi + P - kj
    window = jnp.logical_and(dist >= 0, dist < P)
    valid_mid = jnp.concatenate([window] * G, axis=1)
    has_prev = pl.program_id(1) > 0
    valid_first = jnp.concatenate([jnp.logical_and(window, jnp.logical_or(kj >= P, has_prev))] * G, axis=1)
    kv = jnp.concatenate([kvprev_ref[...], kv_cur], axis=0)
    kvprev_ref[...] = kv_cur[tt - P:, :]
    qb = (q * (HD ** -0.5)).astype(BF16)
    for kh in range(KV):
        sink = jnp.concatenate([jnp.full((1, P), sink_of(kh * G + g), F32) for g in range(G)], axis=1)
        k_all = kv[:, kh * HD:(kh + 1) * HD].astype(BF16)
        v_all_t = kv[:, (KV + kh) * HD:(KV + kh + 1) * HD].T.astype(BF16)
        for j in range(tt // P):
            lo = j * P
            qs = jnp.concatenate(
                [qb[lo:lo + P, (kh * G + g) * HD:(kh * G + g + 1) * HD] for g in range(G)], axis=0)
            s = jnp.where(valid_first if j == 0 else valid_mid, _dot_nt(k_all[lo:lo + 2 * P], qs), NEG_INF)
            mx = jnp.maximum(jnp.max(s, axis=0, keepdims=True), sink)
            e = jnp.exp(s - mx)
            den = jnp.sum(e, axis=0, keepdims=True) + jnp.exp(sink - mx)
            o_t = _dot(v_all_t[:, lo:lo + 2 * P], e.astype(BF16)) * (1.0 / den)
            for g2 in range(G // 2):
                h0 = kh * G + 2 * g2
                pair = jnp.concatenate(
                    [o_t[:, 2 * g2 * P:(2 * g2 + 1) * P], o_t[:, (2 * g2 + 1) * P:(2 * g2 + 2) * P]], axis=0).T
                zp = z[lo:lo + P, h0 * HD:(h0 + 2) * HD]
                y_ref[lo:lo + P, h0 * HD:(h0 + 2) * HD] = (pair * _silu(zp)).astype(BF16)


def _memattn_branch(q, z, kv_ref, y_ref):
    HD = XM_HD
    qb = q.astype(BF16)
    for h in range(XM_HEADS):
        mk = kv_ref[:, h * HD:(h + 1) * HD]
        mv = kv_ref[:, BRANCH_W + h * HD:BRANCH_W + (h + 1) * HD]
        s = _dot_nt(qb[:, h * HD:(h + 1) * HD], mk) * (HD ** -0.5)
        e = jnp.exp(s - jnp.max(s, axis=-1, keepdims=True))
        o = _dot(e.astype(BF16), mv) * (1.0 / jnp.sum(e, axis=-1, keepdims=True))
        y_ref[:, h * HD:(h + 1) * HD] = (o * _silu(z[:, h * HD:(h + 1) * HD])).astype(BF16)


def _layer_kernel(sinks_ref, x_ref, npre_ref, w_ref, cw_ref, gp_ref, on_ref, gmn_ref, ws_ref, bias_ref,
                  mkv_ref, wup_ref, wout_ref, npost_ref, o_ref, tail_ref, s_ref, kvprev_ref, y_ref, gate_ref,
                  *, tt, layer):
    @pl.when(pl.program_id(1) == 0)
    def _():
        tail_ref[...] = jnp.zeros_like(tail_ref)
        s_ref[...] = jnp.zeros_like(s_ref)
        kvprev_ref[...] = jnp.zeros_like(kvprev_ref)

    x = x_ref[...]
    h = _rms(x, npre_ref[...]).astype(BF16)

    def proj(lo, width):
        return _dot(h, w_ref[:, lo:lo + width])

    def gate(n):
        def run():
            gate_ref[n] = _sigmoid(proj(OFF_GATES + n * D_MODEL, D_MODEL))
        return run

    def branch_b():
        _gmlp_branch(proj(OFF_BUV, 2 * BRANCH_W), proj(OFF_BZ, BRANCH_W), gmn_ref, ws_ref, bias_ref,
                     y_ref.at[1], tt=tt)

    def branch_c():
        _swa_branch(lambda hd: sinks_ref[layer, hd], proj(OFF_CQ, BRANCH_W), proj(OFF_CZ, BRANCH_W),
                    proj(OFF_CKV, CKV_COLS), y_ref.at[2], kvprev_ref, tt=tt)

    def branch_m():
        _memattn_branch(proj(OFF_MQ, BRANCH_W), proj(OFF_MZ, BRANCH_W), mkv_ref, y_ref.at[3])

    fillers = {0: gate(0), 1: gate(1), 2: branch_b, 4: gate(2), 6: branch_c, 9: gate(3), 11: branch_m}
    stages = _deltanet_stages(proj(OFF_AQKV, 3 * BRANCH_W), proj(OFF_AZ, BRANCH_W), proj(OFF_SMALL, LANES),
                              cw_ref, gp_ref, on_ref, y_ref.at[0], tail_ref, s_ref, tt=tt)
    done = set()
    for i, _ in enumerate(stages):
        if i in fillers:
            fillers[i]()
            done.add(i)
    for i in sorted(set(fillers) - done):
        fillers[i]()

    merged = None
    for n in range(N_BRANCH):
        term = gate_ref[n] * _dot(y_ref[n], wup_ref[n])
        merged = term if merged is None else merged + term
    out = _dot(merged.astype(BF16), wout_ref[...])
    o_ref[...] = x + _rms(out, npost_ref[...])


def _layer_call(x2, layer, sinks, norm_pre, w_pad, conv_w, gate_params, o_norm, gm_norm, spatial_w, bias_mat,
                mem_kv, w_up, w_out, norm_post, *, batch, seq, tt):
    nt = seq // tt
    ml = mem_kv.shape[2]
    row = lambda b, t, s: (b * nt + t, 0)

    def const(arr):
        shape = arr.shape[1:]
        return pl.BlockSpec((None,) + shape, lambda b, t, s: (layer,) + (0,) * len(shape),
                            pipeline_mode=pl.Buffered(1))

    in_specs = [
        pl.BlockSpec((tt, D_MODEL), row),
        const(norm_pre), const(w_pad), const(conv_w), const(gate_params), const(o_norm), const(gm_norm),
        const(spatial_w), const(bias_mat),
        pl.BlockSpec((None, None, ml, 2 * BRANCH_W), lambda b, t, s: (layer, b, 0, 0)),
        const(w_up), const(w_out), const(norm_post),
    ]
    return pl.pallas_call(
        functools.partial(_layer_kernel, tt=tt, layer=layer),
        grid_spec=pltpu.PrefetchScalarGridSpec(
            num_scalar_prefetch=1,
            grid=(batch, nt),
            in_specs=in_specs,
            out_specs=pl.BlockSpec((tt, D_MODEL), row),
            scratch_shapes=[
                pltpu.VMEM((3, 8, BRANCH_W), F32),
                pltpu.VMEM((DN_DK, DN_HEADS * DN_DK), F32),
                pltpu.VMEM((SW_BLOCK, CKV_COLS), F32),
                pltpu.VMEM((N_BRANCH, tt, BRANCH_W), BF16),
                pltpu.VMEM((N_BRANCH, tt, D_MODEL), F32),
            ],
        ),
        out_shape=jax.ShapeDtypeStruct((batch * seq, D_MODEL), F32),
        compiler_params=pltpu.CompilerParams(
            dimension_semantics=("arbitrary", "arbitrary"), vmem_limit_bytes=VMEM_LIMIT),
        name="layer",
    )(sinks, x2, norm_pre, w_pad, conv_w, gate_params, o_norm, gm_norm, spatial_w, bias_mat, mem_kv,
      w_up, w_out, norm_post)


def kernel(x, mem, norm_pre, norm_post, norm_mem, w_in, conv_w, a_log, dt_bias, dn_norm, gm_norm,
           spatial_w, spatial_b, sinks, w_mem_kv, w_up, w_out):
    batch, seq, d = x.shape
    nl = w_in.shape[0]
    split = OFF_SMALL + N_SMALL
    w_pad = jnp.concatenate(
        [w_in[:, :, :split], jnp.zeros((nl, d, LANES - N_SMALL), w_in.dtype), w_in[:, :, split:]],
        axis=2).astype(BF16)
    lane_pad = ((0, 0), (DN_HEADS, LANES - 2 * DN_HEADS))
    gate_params = jnp.stack([jnp.pad(-jnp.exp(a_log), lane_pad), jnp.pad(dt_bias, lane_pad)], axis=1)
    bias_mat = jnp.repeat(jnp.swapaxes(spatial_b, 1, 2), GM_CHUNK, axis=2)
    mem_kv = _memkv(mem, norm_mem[:, None, :], w_mem_kv.astype(BF16))
    w_up_b = w_up.astype(BF16)
    w_out_b = w_out.astype(BF16)

    x2 = x.reshape(batch * seq, d)
    for l in range(nl):
        x2 = _layer_call(x2, l, sinks, norm_pre[:, None, :], w_pad, conv_w, gate_params, dn_norm[:, None, :],
                         gm_norm[:, None, :], spatial_w, bias_mat, mem_kv, w_up_b, w_out_b,
                         norm_post[:, None, :], batch=batch, seq=seq, tt=TT_LAYER)
    return x2.reshape(batch, seq, d)
```

```python
import functools

import jax
import jax.numpy as jnp
from jax import lax
from jax.experimental import pallas as pl
from jax.experimental.pallas import tpu as pltpu

F32 = jnp.float32
BF16 = jnp.bfloat16

D_MODEL = 1024
N_BRANCH = 4
BRANCH_W = 512
DN_HEADS = 4
DN_DK = 128
DN_CHUNK = 64
CONV_W = 4
GM_GROUPS = 4
GM_CHUNK = 128
SW_HEADS = 8
SW_KV_HEADS = 2
SW_HD = 64
SW_BLOCK = 128
XM_HEADS = 4
XM_HD = 128
EPS = 1e-6
NEG_INF = -1e30
LANES = 128

CKV_COLS = 2 * SW_KV_HEADS * SW_HD
OFF_AQKV = 0
OFF_AZ = OFF_AQKV + 3 * BRANCH_W
OFF_SMALL = OFF_AZ + BRANCH_W
N_SMALL = 2 * DN_HEADS
OFF_BUV = OFF_SMALL + LANES
OFF_BZ = OFF_BUV + 2 * BRANCH_W
OFF_CQ = OFF_BZ + BRANCH_W
OFF_CKV = OFF_CQ + BRANCH_W
OFF_CZ = OFF_CKV + CKV_COLS
OFF_MQ = OFF_CZ + BRANCH_W
OFF_MZ = OFF_MQ + BRANCH_W
OFF_GATES = OFF_MZ + BRANCH_W
W_COLS = OFF_GATES + N_BRANCH * D_MODEL

VMEM_LIMIT = 60 * 1024 * 1024
TT_LAYER = 256
PROJ_TILE = 256
FILL_PER_STAGE = 1
BODY_AT = (12, 20, 27)


def _sigmoid(x):
    return 1.0 / (1.0 + jnp.exp(-x))


def _silu(x):
    return x * _sigmoid(x)


def _softplus(x):
    return jnp.maximum(x, 0.0) + jnp.log(1.0 + jnp.exp(-jnp.abs(x)))


def _rms(x, gain):
    return x * lax.rsqrt(jnp.mean(x * x, axis=-1, keepdims=True) + EPS) * gain


def _dot(a, b):
    return jnp.dot(a, b, preferred_element_type=F32)


def _dot_nt(a, b):
    return lax.dot_general(a, b, (((1,), (1,)), ((), ())), preferred_element_type=F32)


def _pad_w_kernel(x_ref, o_ref):
    x = x_ref[...]
    rows = x.shape[0]
    o_ref[:, :OFF_SMALL] = x[:, :OFF_SMALL].astype(BF16)
    lane = lax.broadcasted_iota(jnp.int32, (rows, LANES), 1)
    o_ref[:, OFF_SMALL:OFF_BUV] = jnp.where(lane < N_SMALL, x[:, OFF_SMALL:OFF_BUV], 0.0).astype(BF16)
    o_ref[:, OFF_BUV:] = x[:, OFF_SMALL + N_SMALL:].astype(BF16)


def _pad_w_in(w_in, rows=256):
    nl, d, n_in = w_in.shape
    assert n_in == W_COLS - (LANES - N_SMALL) and d % rows == 0
    return pl.pallas_call(
        _pad_w_kernel,
        grid=(nl, d // rows),
        in_specs=[pl.BlockSpec((None, rows, n_in), lambda l, i: (l, i, 0))],
        out_specs=pl.BlockSpec((None, rows, W_COLS), lambda l, i: (l, i, 0)),
        out_shape=jax.ShapeDtypeStruct((nl, d, W_COLS), BF16),
        compiler_params=pltpu.CompilerParams(
            dimension_semantics=("arbitrary", "arbitrary"), vmem_limit_bytes=48 * 1024 * 1024),
        name="padw",
    )(w_in)


def _memkv_kernel(m_ref, g_ref, w_ref, o_ref):
    o_ref[...] = _dot(_rms(m_ref[...], g_ref[...]).astype(BF16), w_ref[...]).astype(BF16)


def _memkv(mem, gain, w):
    b, ml, _ = mem.shape
    nl = w.shape[0]
    return pl.pallas_call(
        _memkv_kernel,
        grid=(nl, b),
        in_specs=[
            pl.BlockSpec((None, ml, D_MODEL), lambda l, i: (i, 0, 0)),
            pl.BlockSpec((None, 1, D_MODEL), lambda l, i: (l, 0, 0)),
            pl.BlockSpec((None, D_MODEL, 2 * BRANCH_W), lambda l, i: (l, 0, 0)),
        ],
        out_specs=pl.BlockSpec((None, None, ml, 2 * BRANCH_W), lambda l, i: (l, i, 0, 0)),
        out_shape=jax.ShapeDtypeStruct((nl, b, ml, 2 * BRANCH_W), BF16),
        compiler_params=pltpu.CompilerParams(dimension_semantics=("arbitrary", "arbitrary")),
        name="memkv",
    )(mem, gain, w)


def _deltanet_stages(qkv, z, sm, cw_ref, gp_ref, on_ref, y_ref, tail_ref, s_ref, *, tt):
    H, C, DK = DN_HEADS, DN_CHUNK, DN_DK
    HC = H * C
    cw = cw_ref[...]

    def conv_silu(idx):
        lo = idx * BRANCH_W
        cur = qkv[:, lo:lo + BRANCH_W]
        ext = jnp.concatenate([tail_ref[idx], cur], axis=0)
        acc = cur * cw[CONV_W - 1:CONV_W, lo:lo + BRANCH_W]
        for j in range(1, CONV_W):
            shifted = pltpu.roll(ext, j, axis=0)[8:]
            acc = acc + shifted * cw[CONV_W - 1 - j:CONV_W - j, lo:lo + BRANCH_W]
        tail_ref[idx] = cur[tt - 8:]
        return _silu(acc)

    qc = conv_silu(0)
    yield
    kc = conv_silu(1)
    yield
    vc = conv_silu(2)
    yield

    def l2n(xh):
        return xh * lax.rsqrt(jnp.sum(xh * xh, axis=-1, keepdims=True) + EPS)

    ci = lax.broadcasted_iota(jnp.int32, (C, HC), 0)
    cl = lax.broadcasted_iota(jnp.int32, (C, HC), 1)
    cj = cl % C
    strict = ci > cj
    incl = ci >= cj
    eye_cat = (ci == cj).astype(F32)
    lane_head = cl // C
    br = lax.broadcasted_iota(jnp.int32, (HC, HC), 0)
    bc = lax.broadcasted_iota(jnp.int32, (HC, HC), 1)
    bd_mask = jnp.where((br // C) == (bc // C), 1.0, 0.0).astype(BF16)
    kl = lax.broadcasted_iota(jnp.int32, (DK, HC), 1) // C
    head_lane_masks = [jnp.where(kl == h, 1.0, 0.0).astype(BF16) for h in range(H)]
    l64 = lax.broadcasted_iota(jnp.int32, (C, DK), 1) < C
    tr = lax.broadcasted_iota(jnp.int32, (C, C), 0)
    tc = lax.broadcasted_iota(jnp.int32, (C, C), 1)
    ltri = jnp.where(tr >= tc, 1.0, 0.0).astype(BF16)

    neg_a = gp_ref[0:1, :]
    dt_b = gp_ref[1:2, :]
    o_gain = on_ref[...]

    def stack(x, lo):
        return jnp.concatenate([x[lo:lo + C, h * DK:(h + 1) * DK] for h in range(H)], axis=0)

    def block_diag(x_cat):
        return jnp.concatenate([x_cat.astype(BF16)] * H, axis=0) * bd_mask

    def diag_to_cat(m):
        out = m[(H - 1) * C:H * C]
        for h in range(H - 2, -1, -1):
            out = jnp.where(lane_head == h, m[h * C:(h + 1) * C], out)
        return out

    chunks = range(tt // C)
    qs, ks, vs, beta_all, gc3 = [], [], [], [], []
    for c in chunks:
        lo = c * C
        qs.append(jnp.concatenate([l2n(qc[lo:lo + C, h * DK:(h + 1) * DK]) for h in range(H)], axis=0) * (DK ** -0.5))
        ks.append(jnp.concatenate([l2n(kc[lo:lo + C, h * DK:(h + 1) * DK]) for h in range(H)], axis=0))
        vs.append(stack(vc, lo))
        smc = sm[lo:lo + C, :]
        beta_all.append(_sigmoid(smc))
        g_all = neg_a * _softplus(smc + dt_b)
        g_hi = g_all.astype(BF16)
        r1 = g_all - g_hi.astype(F32)
        g_mid = r1.astype(BF16)
        g_lo = (r1 - g_mid.astype(F32)).astype(BF16)
        gc3.append(_dot(ltri, jnp.concatenate([g_hi, g_mid, g_lo], axis=1)))
        yield

    cb, bb, decay, egc, kb, kq = [], [], [], [], [], []
    for c in chunks:
        gc_all = gc3[c][:, :LANES] + gc3[c][:, LANES:2 * LANES] + gc3[c][:, 2 * LANES:]
        gcb = [jnp.broadcast_to(gc_all[:, H + h:H + h + 1], (C, DK)) for h in range(H)]
        cb.append(jnp.concatenate(gcb, axis=0))
        bb.append(jnp.concatenate(
            [jnp.broadcast_to(beta_all[c][:, h:h + 1], (C, DK)) for h in range(H)], axis=0))
        c_cat = jnp.concatenate([jnp.where(l64, gcb[0], gcb[1]), jnp.where(l64, gcb[2], gcb[3])], axis=1)
        gct = jnp.concatenate([gc_all, gc_all], axis=0).T
        r_cat = jnp.concatenate(
            [jnp.where(l64[0:1], gct[H:H + 1], gct[H + 1:H + 2]),
             jnp.where(l64[0:1], gct[H + 2:H + 3], gct[H + 3:H + 4])], axis=1)
        decay.append(jnp.exp(jnp.where(incl, c_cat - r_cat, 0.0)))
        egc.append(jnp.exp(cb[c]))
        kb.append(ks[c] * bb[c])
        kq.append(_dot_nt(jnp.concatenate([kb[c], qs[c]], axis=0).astype(BF16), ks[c].astype(BF16)))
        yield

    a_cat, p_cat, cur = [], [], []
    for c in chunks:
        n_cat = -jnp.where(strict, diag_to_cat(kq[c][:HC]) * decay[c], 0.0)
        a_cat.append(jnp.where(incl, diag_to_cat(kq[c][HC:]) * decay[c], 0.0))
        p_cat.append(eye_cat + n_cat)
        cur.append(_dot(n_cat.astype(BF16), block_diag(n_cat)))
        yield
    for _ in range(4):
        for c in chunks:
            out = _dot(jnp.concatenate([cur[c], p_cat[c]], axis=0).astype(BF16), block_diag(cur[c]))
            p_cat[c] = p_cat[c] + out[C:]
            cur[c] = out[:C]
            yield
    t_cat = [p_cat[c] + _dot(p_cat[c].astype(BF16), block_diag(cur[c])) for c in chunks]
    yield
    solb = []
    for c in chunks:
        rhs = jnp.concatenate([vs[c] * bb[c], kb[c] * egc[c]], axis=1)
        solb.append(_dot(block_diag(t_cat[c]), rhs.astype(BF16)).astype(BF16))
        yield
    auw, qp, gb = [], [], []
    for c in chunks:
        auw.append(_dot(block_diag(a_cat[c]), solb[c]))
        qp.append(qs[c] * egc[c] - auw[c][:, DK:])
        glast = jnp.concatenate(
            [jnp.broadcast_to(cb[c][h * C + C - 1:h * C + C, :], (C, DK)) for h in range(H)], axis=0)
        kd_t = (ks[c] * jnp.exp(glast - cb[c])).T.astype(BF16)
        gb.append([_dot(kd_t * head_lane_masks[h], solb[c]) for h in range(H)])
        yield

    for c in chunks:
        lo = c * C
        for h in range(H):
            s_h = s_ref[:, h * DK:(h + 1) * DK]
            gq = _dot(jnp.concatenate([gb[c][h][:, DK:], qp[c][h * C:(h + 1) * C]], axis=0).astype(BF16),
                      s_h.astype(BF16))
            d_h = jnp.exp(cb[c][h * C + C - 1:h * C + C, :])
            s_ref[:, h * DK:(h + 1) * DK] = s_h * d_h - gq[:DK] + gb[c][h][:, :DK]
            oh = _rms(gq[DK:] + auw[c][h * C:(h + 1) * C, :DK], o_gain)
            zh = z[lo:lo + C, h * DK:(h + 1) * DK]
            y_ref[lo:lo + C, h * DK:(h + 1) * DK] = (oh * _silu(zh)).astype(BF16)
        yield


def _gmlp_branch(uv, z, gain_ref, ws_ref, bias_ref, y_ref, *, tt):
    P, G = GM_CHUNK, GM_GROUPS
    u = jax.nn.gelu(uv[:, :BRANCH_W])
    v = _rms(jax.nn.gelu(uv[:, BRANCH_W:]), gain_ref[...])
    vb = v.astype(BF16)
    tr = lax.broadcasted_iota(jnp.int32, (P, P), 0)
    tc = lax.broadcasted_iota(jnp.int32, (P, P), 1)
    causal = tr >= tc
    bias = bias_ref[...]
    for g in range(G):
        wg = jnp.where(causal, ws_ref[g], 0.0).astype(BF16)
        for ci in range(tt // P):
            lo = ci * P
            s = _dot(wg, vb[lo:lo + P, g * P:(g + 1) * P]) + bias[:, g * P:(g + 1) * P]
            zg = z[lo:lo + P, g * P:(g + 1) * P]
            y_ref[lo:lo + P, g * P:(g + 1) * P] = (u[lo:lo + P, g * P:(g + 1) * P] * s * _silu(zg)).astype(BF16)


def _swa_branch(sink_of, q, z, kv_cur, y_ref, kvprev_ref, *, tt):
    P, HD, KV = SW_BLOCK, SW_HD, SW_KV_HEADS
    G = SW_HEADS // KV
    kj = lax.broadcasted_iota(jnp.int32, (2 * P, P), 0)
    qi = lax.broadcasted_iota(jnp.int32, (2 * P, P), 1)
    dist = qi + P - kj
    window = jnp.logical_and(dist >= 0, dist < P)
    valid_mid = jnp.concatenate([window] * G, axis=1)
    has_prev = pl.program_id(1) > 0
    valid_first = jnp.concatenate([jnp.logical_and(window, jnp.logical_or(kj >= P, has_prev))] * G, axis=1)
    kv = jnp.concatenate([kvprev_ref[...], kv_cur], axis=0)
    kvprev_ref[...] = kv_cur[tt - P:, :]
    qb = (q * (HD ** -0.5)).astype(BF16)
    for kh in range(KV):
        sink = jnp.concatenate([jnp.full((1, P), sink_of(kh * G + g), F32) for g in range(G)], axis=1)
        k_all = kv[:, kh * HD:(kh + 1) * HD].astype(BF16)
        v_all_t = kv[:, (KV + kh) * HD:(KV + kh + 1) * HD].T.astype(BF16)
        for j in range(tt // P):
            lo = j * P
            qs = jnp.concatenate(
                [qb[lo:lo + P, (kh * G + g) * HD:(kh * G + g + 1) * HD] for g in range(G)], axis=0)
            s = jnp.where(valid_first if j == 0 else valid_mid, _dot_nt(k_all[lo:lo + 2 * P], qs), NEG_INF)
            mx = jnp.maximum(jnp.max(s, axis=0, keepdims=True), sink)
            e = jnp.exp(s - mx)
            den = jnp.sum(e, axis=0, keepdims=True) + jnp.exp(sink - mx)
            o_t = _dot(v_all_t[:, lo:lo + 2 * P], e.astype(BF16)) * (1.0 / den)
            for g2 in range(G // 2):
                h0 = kh * G + 2 * g2
                pair = jnp.concatenate(
                    [o_t[:, 2 * g2 * P:(2 * g2 + 1) * P], o_t[:, (2 * g2 + 1) * P:(2 * g2 + 2) * P]], axis=0).T
                zp = z[lo:lo + P, h0 * HD:(h0 + 2) * HD]
                y_ref[lo:lo + P, h0 * HD:(h0 + 2) * HD] = (pair * _silu(zp)).astype(BF16)


def _memattn_branch(q, z, kv_ref, y_ref):
    HD = XM_HD
    qb = q.astype(BF16)
    for h in range(XM_HEADS):
        mk = kv_ref[:, h * HD:(h + 1) * HD]
        mv = kv_ref[:, BRANCH_W + h * HD:BRANCH_W + (h + 1) * HD]
        s = _dot_nt(qb[:, h * HD:(h + 1) * HD], mk) * (HD ** -0.5)
        e = jnp.exp(s - jnp.max(s, axis=-1, keepdims=True))
        o = _dot(e.astype(BF16), mv) * (1.0 / jnp.sum(e, axis=-1, keepdims=True))
        y_ref[:, h * HD:(h + 1) * HD] = (o * _silu(z[:, h * HD:(h + 1) * HD])).astype(BF16)


def _layer_kernel(sinks_ref, x_ref, npre_ref, w_ref, cw_ref, gp_ref, on_ref, gmn_ref, ws_ref, bias_ref,
                  mkv_ref, wup_ref, wout_ref, npost_ref, o_ref, tail_ref, s_ref, kvprev_ref, y_ref, cols_ref,
                  *, tt, layer):
    @pl.when(pl.program_id(1) == 0)
    def _():
        tail_ref[...] = jnp.zeros_like(tail_ref)
        s_ref[...] = jnp.zeros_like(s_ref)
        kvprev_ref[...] = jnp.zeros_like(kvprev_ref)

    x = x_ref[...]
    h = _rms(x, npre_ref[...]).astype(BF16)

    def proj(lo, width):
        return _dot(h, w_ref[:, lo:lo + width])

    tiles = iter(range(OFF_BUV, W_COLS, PROJ_TILE))
    filled = [OFF_BUV]

    def fill(count):
        for _ in range(count):
            lo = next(tiles, None)
            if lo is not None:
                cols_ref[:, lo - OFF_BUV:lo - OFF_BUV + PROJ_TILE] = proj(lo, PROJ_TILE)
                filled[0] = lo + PROJ_TILE

    def col(lo, width):
        assert lo + width <= filled[0], "projection read before it was issued"
        return cols_ref[:, lo - OFF_BUV:lo - OFF_BUV + width]

    def branch_b():
        _gmlp_branch(col(OFF_BUV, 2 * BRANCH_W), col(OFF_BZ, BRANCH_W), gmn_ref, ws_ref, bias_ref,
                     y_ref.at[1], tt=tt)

    def branch_c():
        _swa_branch(lambda hd: sinks_ref[layer, hd], col(OFF_CQ, BRANCH_W), col(OFF_CZ, BRANCH_W),
                    col(OFF_CKV, CKV_COLS), y_ref.at[2], kvprev_ref, tt=tt)

    def branch_m():
        _memattn_branch(col(OFF_MQ, BRANCH_W), col(OFF_MZ, BRANCH_W), mkv_ref, y_ref.at[3])

    bodies = {BODY_AT[0]: branch_b, BODY_AT[1]: branch_c, BODY_AT[2]: branch_m}
    stages = _deltanet_stages(proj(OFF_AQKV, 3 * BRANCH_W), proj(OFF_AZ, BRANCH_W), proj(OFF_SMALL, LANES),
                              cw_ref, gp_ref, on_ref, y_ref.at[0], tail_ref, s_ref, tt=tt)
    for i, _ in enumerate(stages):
        fill(FILL_PER_STAGE)
        if i in bodies:
            bodies.pop(i)()
    fill(W_COLS)
    for i in sorted(bodies):
        bodies[i]()

    merged = None
    for n in range(N_BRANCH):
        term = _sigmoid(col(OFF_GATES + n * D_MODEL, D_MODEL)) * _dot(y_ref[n], wup_ref[n])
        merged = term if merged is None else merged + term
    out = _dot(merged.astype(BF16), wout_ref[...])
    o_ref[...] = x + _rms(out, npost_ref[...])


def _layer_call(x2, layer, sinks, norm_pre, w_pad, conv_w, gate_params, o_norm, gm_norm, spatial_w, bias_mat,
                mem_kv, w_up, w_out, norm_post, *, batch, seq, tt):
    nt = seq // tt
    ml = mem_kv.shape[2]
    row = lambda b, t, s: (b * nt + t, 0)

    def const(arr):
        shape = arr.shape[1:]
        return pl.BlockSpec((None,) + shape, lambda b, t, s: (layer,) + (0,) * len(shape),
                            pipeline_mode=pl.Buffered(1))

    in_specs = [
        pl.BlockSpec((tt, D_MODEL), row),
        const(norm_pre), const(w_pad), const(conv_w), const(gate_params), const(o_norm), const(gm_norm),
        const(spatial_w), const(bias_mat),
        pl.BlockSpec((None, None, ml, 2 * BRANCH_W), lambda b, t, s: (layer, b, 0, 0)),
        const(w_up), const(w_out), const(norm_post),
    ]
    return pl.pallas_call(
        functools.partial(_layer_kernel, tt=tt, layer=layer),
        grid_spec=pltpu.PrefetchScalarGridSpec(
            num_scalar_prefetch=1,
            grid=(batch, nt),
            in_specs=in_specs,
            out_specs=pl.BlockSpec((tt, D_MODEL), row),
            scratch_shapes=[
                pltpu.VMEM((3, 8, BRANCH_W), F32),
                pltpu.VMEM((DN_DK, DN_HEADS * DN_DK), F32),
                pltpu.VMEM((SW_BLOCK, CKV_COLS), F32),
                pltpu.VMEM((N_BRANCH, tt, BRANCH_W), BF16),
                pltpu.VMEM((tt, W_COLS - OFF_BUV), F32),
            ],
        ),
        out_shape=jax.ShapeDtypeStruct((batch * seq, D_MODEL), F32),
        compiler_params=pltpu.CompilerParams(
            dimension_semantics=("arbitrary", "arbitrary"), vmem_limit_bytes=VMEM_LIMIT),
        name="layer",
    )(sinks, x2, norm_pre, w_pad, conv_w, gate_params, o_norm, gm_norm, spatial_w, bias_mat, mem_kv,
      w_up, w_out, norm_post)


def kernel(x, mem, norm_pre, norm_post, norm_mem, w_in, conv_w, a_log, dt_bias, dn_norm, gm_norm,
           spatial_w, spatial_b, sinks, w_mem_kv, w_up, w_out):
    batch, seq, d = x.shape
    nl = w_in.shape[0]
    w_pad = _pad_w_in(w_in)
    lane_pad = ((0, 0), (DN_HEADS, LANES - 2 * DN_HEADS))
    gate_params = jnp.stack([jnp.pad(-jnp.exp(a_log), lane_pad), jnp.pad(dt_bias, lane_pad)], axis=1)
    bias_mat = jnp.repeat(jnp.swapaxes(spatial_b, 1, 2), GM_CHUNK, axis=2)
    mem_kv = _memkv(mem, norm_mem[:, None, :], w_mem_kv.astype(BF16))
    w_up_b = w_up.astype(BF16)
    w_out_b = w_out.astype(BF16)

    x2 = x.reshape(batch * seq, d)
    for l in range(nl):
        x2 = _layer_call(x2, l, sinks, norm_pre[:, None, :], w_pad, conv_w, gate_params, dn_norm[:, None, :],
                         gm_norm[:, None, :], spatial_w, bias_mat, mem_kv, w_up_b, w_out_b,
                         norm_post[:, None, :], batch=batch, seq=seq, tt=TT_LAYER)
    return x2.reshape(batch, seq, d)
```

```python
import collections
import functools

import jax
import jax.numpy as jnp
from jax import lax
from jax.experimental import pallas as pl
from jax.experimental.pallas import tpu as pltpu

F32 = jnp.float32
BF16 = jnp.bfloat16

D_MODEL = 1024
N_BRANCH = 4
BRANCH_W = 512
DN_HEADS = 4
DN_DK = 128
DN_CHUNK = 64
CONV_W = 4
GM_GROUPS = 4
GM_CHUNK = 128
SW_HEADS = 8
SW_KV_HEADS = 2
SW_HD = 64
SW_BLOCK = 128
XM_HEADS = 4
XM_HD = 128
EPS = 1e-6
NEG_INF = -1e30
LANES = 128

CKV_COLS = 2 * SW_KV_HEADS * SW_HD
OFF_AQKV = 0
OFF_AZ = OFF_AQKV + 3 * BRANCH_W
OFF_SMALL = OFF_AZ + BRANCH_W
N_SMALL = 2 * DN_HEADS
OFF_BUV = OFF_SMALL + LANES
OFF_BZ = OFF_BUV + 2 * BRANCH_W
OFF_CQ = OFF_BZ + BRANCH_W
OFF_CKV = OFF_CQ + BRANCH_W
OFF_CZ = OFF_CKV + CKV_COLS
OFF_MQ = OFF_CZ + BRANCH_W
OFF_MZ = OFF_MQ + BRANCH_W
OFF_GATES = OFF_MZ + BRANCH_W
W_COLS = OFF_GATES + N_BRANCH * D_MODEL

VMEM_LIMIT = 60 * 1024 * 1024
TT_LAYER = 256
PROJ_TILE = 256
BODY_AT = (12, 20, 27)


def _sigmoid(x):
    return 1.0 / (1.0 + jnp.exp(-x))


def _silu(x):
    return x * _sigmoid(x)


def _softplus(x):
    return jnp.maximum(x, 0.0) + jnp.log(1.0 + jnp.exp(-jnp.abs(x)))


def _rms(x, gain):
    return x * lax.rsqrt(jnp.mean(x * x, axis=-1, keepdims=True) + EPS) * gain


def _dot(a, b):
    return jnp.dot(a, b, preferred_element_type=F32)


def _dot_nt(a, b):
    return lax.dot_general(a, b, (((1,), (1,)), ((), ())), preferred_element_type=F32)


PADW_COLS = 1024
PAD_ROWS = LANES - N_SMALL


def _pad_w_kernel(prev_ref, cur_ref, o_ref):
    i = pl.program_id(1)
    small_blk = OFF_SMALL // PADW_COLS
    keep = PADW_COLS - PAD_ROWS

    @pl.when(i < small_blk)
    def _():
        o_ref[...] = cur_ref[...].T.astype(BF16)

    @pl.when(i == small_blk)
    def _():
        cur = cur_ref[...]
        src = jnp.concatenate([cur[:N_SMALL], jnp.zeros((PAD_ROWS, cur.shape[1]), F32), cur[N_SMALL:keep]], axis=0)
        o_ref[...] = src.T.astype(BF16)

    @pl.when(i > small_blk)
    def _():
        src = jnp.concatenate([prev_ref[LANES - PAD_ROWS:], cur_ref[:keep]], axis=0)
        o_ref[...] = src.T.astype(BF16)


def _pad_w_in(w_t):
    nl, n_in, d = w_t.shape
    assert n_in == W_COLS - PAD_ROWS and OFF_SMALL % PADW_COLS == 0 and OFF_BUV - OFF_SMALL == LANES
    sub = PADW_COLS // LANES
    return pl.pallas_call(
        _pad_w_kernel,
        grid=(nl, pl.cdiv(W_COLS, PADW_COLS)),
        in_specs=[
            pl.BlockSpec((None, LANES, d), lambda l, i: (l, jnp.maximum(i * sub - 1, 0), 0)),
            pl.BlockSpec((None, PADW_COLS, d), lambda l, i: (l, i, 0)),
        ],
        out_specs=pl.BlockSpec((None, d, PADW_COLS), lambda l, i: (l, 0, i)),
        out_shape=jax.ShapeDtypeStruct((nl, d, W_COLS), BF16),
        compiler_params=pltpu.CompilerParams(
            dimension_semantics=("arbitrary", "arbitrary"), vmem_limit_bytes=48 * 1024 * 1024),
        name="padw",
    )(w_t, w_t)


def _memkv_kernel(m_ref, g_ref, w_ref, o_ref):
    o_ref[...] = _dot(_rms(m_ref[...], g_ref[...]).astype(BF16), w_ref[...]).astype(BF16)


def _memkv(mem, gain, w):
    b, ml, _ = mem.shape
    nl = w.shape[0]
    return pl.pallas_call(
        _memkv_kernel,
        grid=(nl, b),
        in_specs=[
            pl.BlockSpec((None, ml, D_MODEL), lambda l, i: (i, 0, 0)),
            pl.BlockSpec((None, 1, D_MODEL), lambda l, i: (l, 0, 0)),
            pl.BlockSpec((None, D_MODEL, 2 * BRANCH_W), lambda l, i: (l, 0, 0)),
        ],
        out_specs=pl.BlockSpec((None, None, ml, 2 * BRANCH_W), lambda l, i: (l, i, 0, 0)),
        out_shape=jax.ShapeDtypeStruct((nl, b, ml, 2 * BRANCH_W), BF16),
        compiler_params=pltpu.CompilerParams(dimension_semantics=("arbitrary", "arbitrary")),
        name="memkv",
    )(mem, gain, w)


def _deltanet_stages(qkv, z, sm, cw_ref, gp_ref, on_ref, y_ref, tail_ref, s_ref, *, tt):
    H, C, DK = DN_HEADS, DN_CHUNK, DN_DK
    HC = H * C
    cw = cw_ref[...]

    def conv_silu(idx):
        lo = idx * BRANCH_W
        cur = qkv[:, lo:lo + BRANCH_W]
        ext = jnp.concatenate([tail_ref[idx], cur], axis=0)
        acc = cur * cw[CONV_W - 1:CONV_W, lo:lo + BRANCH_W]
        for j in range(1, CONV_W):
            shifted = pltpu.roll(ext, j, axis=0)[8:]
            acc = acc + shifted * cw[CONV_W - 1 - j:CONV_W - j, lo:lo + BRANCH_W]
        tail_ref[idx] = cur[tt - 8:]
        return _silu(acc)

    qc = conv_silu(0)
    yield
    kc = conv_silu(1)
    yield
    vc = conv_silu(2)
    yield

    def l2n(xh):
        return xh * lax.rsqrt(jnp.sum(xh * xh, axis=-1, keepdims=True) + EPS)

    ci = lax.broadcasted_iota(jnp.int32, (C, HC), 0)
    cl = lax.broadcasted_iota(jnp.int32, (C, HC), 1)
    cj = cl % C
    strict = ci > cj
    incl = ci >= cj
    eye_cat = (ci == cj).astype(F32)
    lane_head = cl // C
    br = lax.broadcasted_iota(jnp.int32, (HC, HC), 0)
    bc = lax.broadcasted_iota(jnp.int32, (HC, HC), 1)
    bd_mask = jnp.where((br // C) == (bc // C), 1.0, 0.0).astype(BF16)
    kl = lax.broadcasted_iota(jnp.int32, (DK, HC), 1) // C
    head_lane_masks = [jnp.where(kl == h, 1.0, 0.0).astype(BF16) for h in range(H)]
    l64 = lax.broadcasted_iota(jnp.int32, (C, DK), 1) < C
    tr = lax.broadcasted_iota(jnp.int32, (C, C), 0)
    tc = lax.broadcasted_iota(jnp.int32, (C, C), 1)
    ltri = jnp.where(tr >= tc, 1.0, 0.0).astype(BF16)

    neg_a = gp_ref[0:1, :]
    dt_b = gp_ref[1:2, :]
    o_gain = on_ref[...]

    def stack(x, lo):
        return jnp.concatenate([x[lo:lo + C, h * DK:(h + 1) * DK] for h in range(H)], axis=0)

    def block_diag(x_cat):
        return jnp.concatenate([x_cat.astype(BF16)] * H, axis=0) * bd_mask

    def diag_to_cat(m):
        out = m[(H - 1) * C:H * C]
        for h in range(H - 2, -1, -1):
            out = jnp.where(lane_head == h, m[h * C:(h + 1) * C], out)
        return out

    chunks = range(tt // C)
    qs, ks, vs, beta_all, gc3 = [], [], [], [], []
    for c in chunks:
        lo = c * C
        qs.append(jnp.concatenate([l2n(qc[lo:lo + C, h * DK:(h + 1) * DK]) for h in range(H)], axis=0) * (DK ** -0.5))
        ks.append(jnp.concatenate([l2n(kc[lo:lo + C, h * DK:(h + 1) * DK]) for h in range(H)], axis=0))
        vs.append(stack(vc, lo))
        smc = sm[lo:lo + C, :]
        beta_all.append(_sigmoid(smc))
        g_all = neg_a * _softplus(smc + dt_b)
        g_hi = g_all.astype(BF16)
        r1 = g_all - g_hi.astype(F32)
        g_mid = r1.astype(BF16)
        g_lo = (r1 - g_mid.astype(F32)).astype(BF16)
        gc3.append(_dot(ltri, jnp.concatenate([g_hi, g_mid, g_lo], axis=1)))
        yield

    cb, bb, decay, egc, kb, kq = [], [], [], [], [], []
    for c in chunks:
        gc_all = gc3[c][:, :LANES] + gc3[c][:, LANES:2 * LANES] + gc3[c][:, 2 * LANES:]
        gcb = [jnp.broadcast_to(gc_all[:, H + h:H + h + 1], (C, DK)) for h in range(H)]
        cb.append(jnp.concatenate(gcb, axis=0))
        bb.append(jnp.concatenate(
            [jnp.broadcast_to(beta_all[c][:, h:h + 1], (C, DK)) for h in range(H)], axis=0))
        c_cat = jnp.concatenate([jnp.where(l64, gcb[0], gcb[1]), jnp.where(l64, gcb[2], gcb[3])], axis=1)
        gct = jnp.concatenate([gc_all, gc_all], axis=0).T
        r_cat = jnp.concatenate(
            [jnp.where(l64[0:1], gct[H:H + 1], gct[H + 1:H + 2]),
             jnp.where(l64[0:1], gct[H + 2:H + 3], gct[H + 3:H + 4])], axis=1)
        decay.append(jnp.exp(jnp.where(incl, c_cat - r_cat, 0.0)))
        egc.append(jnp.exp(cb[c]))
        kb.append(ks[c] * bb[c])
        kq.append(_dot_nt(jnp.concatenate([kb[c], qs[c]], axis=0).astype(BF16), ks[c].astype(BF16)))
        yield

    a_cat, p_cat, cur = [], [], []
    for c in chunks:
        n_cat = -jnp.where(strict, diag_to_cat(kq[c][:HC]) * decay[c], 0.0)
        a_cat.append(jnp.where(incl, diag_to_cat(kq[c][HC:]) * decay[c], 0.0))
        p_cat.append(eye_cat + n_cat)
        cur.append(_dot(n_cat.astype(BF16), block_diag(n_cat)))
        yield
    for _ in range(4):
        for c in chunks:
            out = _dot(jnp.concatenate([cur[c], p_cat[c]], axis=0).astype(BF16), block_diag(cur[c]))
            p_cat[c] = p_cat[c] + out[C:]
            cur[c] = out[:C]
            yield
    t_cat = [p_cat[c] + _dot(p_cat[c].astype(BF16), block_diag(cur[c])) for c in chunks]
    yield
    solb = []
    for c in chunks:
        rhs = jnp.concatenate([vs[c] * bb[c], kb[c] * egc[c]], axis=1)
        solb.append(_dot(block_diag(t_cat[c]), rhs.astype(BF16)).astype(BF16))
        yield
    auw, qp, gb = [], [], []
    for c in chunks:
        auw.append(_dot(block_diag(a_cat[c]), solb[c]))
        qp.append(qs[c] * egc[c] - auw[c][:, DK:])
        glast = jnp.concatenate(
            [jnp.broadcast_to(cb[c][h * C + C - 1:h * C + C, :], (C, DK)) for h in range(H)], axis=0)
        kd_t = (ks[c] * jnp.exp(glast - cb[c])).T.astype(BF16)
        gb.append([_dot(kd_t * head_lane_masks[h], solb[c]) for h in range(H)])
        yield

    for c in chunks:
        lo = c * C
        for h in range(H):
            s_h = s_ref[:, h * DK:(h + 1) * DK]
            gq = _dot(jnp.concatenate([gb[c][h][:, DK:], qp[c][h * C:(h + 1) * C]], axis=0).astype(BF16),
                      s_h.astype(BF16))
            d_h = jnp.exp(cb[c][h * C + C - 1:h * C + C, :])
            s_ref[:, h * DK:(h + 1) * DK] = s_h * d_h - gq[:DK] + gb[c][h][:, :DK]
            oh = _rms(gq[DK:] + auw[c][h * C:(h + 1) * C, :DK], o_gain)
            zh = z[lo:lo + C, h * DK:(h + 1) * DK]
            y_ref[lo:lo + C, h * DK:(h + 1) * DK] = (oh * _silu(zh)).astype(BF16)
        yield


def _gmlp_branch(uv, z, gain_ref, ws_ref, bias_ref, y_ref, tick, *, tt):
    P, G = GM_CHUNK, GM_GROUPS
    u = jax.nn.gelu(uv[:, :BRANCH_W])
    tick()
    v = _rms(jax.nn.gelu(uv[:, BRANCH_W:]), gain_ref[...])
    tick()
    vb = v.astype(BF16)
    tr = lax.broadcasted_iota(jnp.int32, (P, P), 0)
    tc = lax.broadcasted_iota(jnp.int32, (P, P), 1)
    causal = tr >= tc
    bias = bias_ref[...]
    for g in range(G):
        wg = jnp.where(causal, ws_ref[g], 0.0).astype(BF16)
        for ci in range(tt // P):
            lo = ci * P
            s = _dot(wg, vb[lo:lo + P, g * P:(g + 1) * P]) + bias[:, g * P:(g + 1) * P]
            zg = z[lo:lo + P, g * P:(g + 1) * P]
            y_ref[lo:lo + P, g * P:(g + 1) * P] = (u[lo:lo + P, g * P:(g + 1) * P] * s * _silu(zg)).astype(BF16)
        tick()


def _swa_branch(sink_of, q, z, kv_cur, y_ref, kvprev_ref, tick, *, tt):
    P, HD, KV = SW_BLOCK, SW_HD, SW_KV_HEADS
    G = SW_HEADS // KV
    kj = lax.broadcasted_iota(jnp.int32, (2 * P, P), 0)
    qi = lax.broadcasted_iota(jnp.int32, (2 * P, P), 1)
    dist = qi + P - kj
    window = jnp.logical_and(dist >= 0, dist < P)
    valid_mid = jnp.concatenate([window] * G, axis=1)
    has_prev = pl.program_id(1) > 0
    valid_first = jnp.concatenate([jnp.logical_and(window, jnp.logical_or(kj >= P, has_prev))] * G, axis=1)
    kv = jnp.concatenate([kvprev_ref[...], kv_cur], axis=0)
    kvprev_ref[...] = kv_cur[tt - P:, :]
    qb = (q * (HD ** -0.5)).astype(BF16)
    for kh in range(KV):
        sink = jnp.concatenate([jnp.full((1, P), sink_of(kh * G + g), F32) for g in range(G)], axis=1)
        k_all = kv[:, kh * HD:(kh + 1) * HD].astype(BF16)
        v_all_t = kv[:, (KV + kh) * HD:(KV + kh + 1) * HD].T.astype(BF16)
        for j in range(tt // P):
            lo = j * P
            qs = jnp.concatenate(
                [qb[lo:lo + P, (kh * G + g) * HD:(kh * G + g + 1) * HD] for g in range(G)], axis=0)
            s = jnp.where(valid_first if j == 0 else valid_mid, _dot_nt(k_all[lo:lo + 2 * P], qs), NEG_INF)
            mx = jnp.maximum(jnp.max(s, axis=0, keepdims=True), sink)
            e = jnp.exp(s - mx)
            den = jnp.sum(e, axis=0, keepdims=True) + jnp.exp(sink - mx)
            o_t = _dot(v_all_t[:, lo:lo + 2 * P], e.astype(BF16)) * (1.0 / den)
            for g2 in range(G // 2):
                h0 = kh * G + 2 * g2
                pair = jnp.concatenate(
                    [o_t[:, 2 * g2 * P:(2 * g2 + 1) * P], o_t[:, (2 * g2 + 1) * P:(2 * g2 + 2) * P]], axis=0).T
                zp = z[lo:lo + P, h0 * HD:(h0 + 2) * HD]
                y_ref[lo:lo + P, h0 * HD:(h0 + 2) * HD] = (pair * _silu(zp)).astype(BF16)
            tick()


def _memattn_branch(q, z, kv_ref, y_ref, tick):
    HD = XM_HD
    qb = q.astype(BF16)
    for h in range(XM_HEADS):
        mk = kv_ref[:, h * HD:(h + 1) * HD]
        mv = kv_ref[:, BRANCH_W + h * HD:BRANCH_W + (h + 1) * HD]
        s = _dot_nt(qb[:, h * HD:(h + 1) * HD], mk) * (HD ** -0.5)
        e = jnp.exp(s - jnp.max(s, axis=-1, keepdims=True))
        o = _dot(e.astype(BF16), mv) * (1.0 / jnp.sum(e, axis=-1, keepdims=True))
        y_ref[:, h * HD:(h + 1) * HD] = (o * _silu(z[:, h * HD:(h + 1) * HD])).astype(BF16)
        tick()


def _layer_kernel(sinks_ref, x_ref, npre_ref, w_ref, cw_ref, gp_ref, on_ref, gmn_ref, ws_ref, bias_ref,
                  mkv_ref, wup_ref, wout_ref, npost_ref, o_ref, tail_ref, s_ref, kvprev_ref, y_ref, cols_ref,
                  merged_ref,
                  *, tt, layer):
    @pl.when(pl.program_id(1) == 0)
    def _():
        tail_ref[...] = jnp.zeros_like(tail_ref)
        s_ref[...] = jnp.zeros_like(s_ref)
        kvprev_ref[...] = jnp.zeros_like(kvprev_ref)

    x = x_ref[...]
    h = _rms(x, npre_ref[...]).astype(BF16)

    def proj(lo, width):
        return _dot(h, w_ref[:, lo:lo + width])

    filled = [OFF_BUV]
    merged_started = set()

    def proj_item(lo):
        def run():
            cols_ref[:, lo - OFF_BUV:lo - OFF_BUV + PROJ_TILE] = proj(lo, PROJ_TILE)
            filled[0] = lo + PROJ_TILE
        return run

    def col(lo, width):
        assert lo + width <= filled[0], "projection read before it was issued"
        return cols_ref[:, lo - OFF_BUV:lo - OFF_BUV + width]

    def merge_item(n, lo):
        def run():
            g = _sigmoid(col(OFF_GATES + n * D_MODEL + lo, PROJ_TILE))
            term = g * _dot(y_ref[n], wup_ref[n, :, lo:lo + PROJ_TILE])
            if lo in merged_started:
                merged_ref[:, lo:lo + PROJ_TILE] += term
            else:
                merged_ref[:, lo:lo + PROJ_TILE] = term
                merged_started.add(lo)
        return run

    queue = collections.deque(proj_item(lo) for lo in range(OFF_BUV, W_COLS, PROJ_TILE))

    def tick():
        if queue:
            queue.popleft()()

    def branch_done(n):
        queue.extend(merge_item(n, lo) for lo in range(0, D_MODEL, PROJ_TILE))

    def branch_b():
        _gmlp_branch(col(OFF_BUV, 2 * BRANCH_W), col(OFF_BZ, BRANCH_W), gmn_ref, ws_ref, bias_ref,
                     y_ref.at[1], tick, tt=tt)
        branch_done(1)

    def branch_c():
        _swa_branch(lambda hd: sinks_ref[layer, hd], col(OFF_CQ, BRANCH_W), col(OFF_CZ, BRANCH_W),
                    col(OFF_CKV, CKV_COLS), y_ref.at[2], kvprev_ref, tick, tt=tt)
        branch_done(2)

    def branch_m():
        _memattn_branch(col(OFF_MQ, BRANCH_W), col(OFF_MZ, BRANCH_W), mkv_ref, y_ref.at[3], tick)
        branch_done(3)

    bodies = {BODY_AT[0]: branch_b, BODY_AT[1]: branch_c, BODY_AT[2]: branch_m}
    stages = _deltanet_stages(proj(OFF_AQKV, 3 * BRANCH_W), proj(OFF_AZ, BRANCH_W), proj(OFF_SMALL, LANES),
                              cw_ref, gp_ref, on_ref, y_ref.at[0], tail_ref, s_ref, tt=tt)
    for i, _ in enumerate(stages):
        tick()
        if i in bodies:
            bodies.pop(i)()
    assert not bodies, "stage list shorter than BODY_AT"
    branch_done(0)
    while queue:
        tick()

    out = _dot(merged_ref[...].astype(BF16), wout_ref[...])
    o_ref[...] = x + _rms(out, npost_ref[...])


def _layer_call(x2, layer, sinks, norm_pre, w_pad, conv_w, gate_params, o_norm, gm_norm, spatial_w, bias_mat,
                mem_kv, w_up, w_out, norm_post, *, batch, seq, tt):
    nt = seq // tt
    ml = mem_kv.shape[2]
    row = lambda b, t, s: (b * nt + t, 0)

    def const(arr):
        shape = arr.shape[1:]
        return pl.BlockSpec((None,) + shape, lambda b, t, s: (layer,) + (0,) * len(shape),
                            pipeline_mode=pl.Buffered(1))

    in_specs = [
        pl.BlockSpec((tt, D_MODEL), row),
        const(norm_pre), const(w_pad), const(conv_w), const(gate_params), const(o_norm), const(gm_norm),
        const(spatial_w), const(bias_mat),
        pl.BlockSpec((None, None, ml, 2 * BRANCH_W), lambda b, t, s: (layer, b, 0, 0)),
        const(w_up), const(w_out), const(norm_post),
    ]
    return pl.pallas_call(
        functools.partial(_layer_kernel, tt=tt, layer=layer),
        grid_spec=pltpu.PrefetchScalarGridSpec(
            num_scalar_prefetch=1,
            grid=(batch, nt),
            in_specs=in_specs,
            out_specs=pl.BlockSpec((tt, D_MODEL), row),
            scratch_shapes=[
                pltpu.VMEM((3, 8, BRANCH_W), F32),
                pltpu.VMEM((DN_DK, DN_HEADS * DN_DK), F32),
                pltpu.VMEM((SW_BLOCK, CKV_COLS), F32),
                pltpu.VMEM((N_BRANCH, tt, BRANCH_W), BF16),
                pltpu.VMEM((tt, W_COLS - OFF_BUV), F32),
                pltpu.VMEM((tt, D_MODEL), F32),
            ],
        ),
        out_shape=jax.ShapeDtypeStruct((batch * seq, D_MODEL), F32),
        compiler_params=pltpu.CompilerParams(
            dimension_semantics=("arbitrary", "arbitrary"), vmem_limit_bytes=VMEM_LIMIT),
        name="layer",
    )(sinks, x2, norm_pre, w_pad, conv_w, gate_params, o_norm, gm_norm, spatial_w, bias_mat, mem_kv,
      w_up, w_out, norm_post)


def kernel(x, mem, norm_pre, norm_post, norm_mem, w_in, conv_w, a_log, dt_bias, dn_norm, gm_norm,
           spatial_w, spatial_b, sinks, w_mem_kv, w_up, w_out):
    batch, seq, d = x.shape
    nl = w_in.shape[0]
    w_pad = _pad_w_in(jnp.swapaxes(w_in, 1, 2))
    lane_pad = ((0, 0), (DN_HEADS, LANES - 2 * DN_HEADS))
    gate_params = jnp.stack([jnp.pad(-jnp.exp(a_log), lane_pad), jnp.pad(dt_bias, lane_pad)], axis=1)
    bias_mat = jnp.repeat(jnp.swapaxes(spatial_b, 1, 2), GM_CHUNK, axis=2)
    mem_kv = _memkv(mem, norm_mem[:, None, :], w_mem_kv.astype(BF16))
    w_up_b = w_up.astype(BF16)
    w_out_b = w_out.astype(BF16)

    x2 = x.reshape(batch * seq, d)
    for l in range(nl):
        x2 = _layer_call(x2, l, sinks, norm_pre[:, None, :], w_pad, conv_w, gate_params, dn_norm[:, None, :],
                         gm_norm[:, None, :], spatial_w, bias_mat, mem_kv, w_up_b, w_out_b,
                         norm_post[:, None, :], batch=batch, seq=seq, tt=TT_LAYER)
    return x2.reshape(batch, seq, d)
```

```python
import collections
import functools

import jax
import jax.numpy as jnp
from jax import lax
from jax.experimental import pallas as pl
from jax.experimental.pallas import tpu as pltpu

F32 = jnp.float32
BF16 = jnp.bfloat16

D_MODEL = 1024
N_BRANCH = 4
BRANCH_W = 512
DN_HEADS = 4
DN_DK = 128
DN_CHUNK = 64
CONV_W = 4
GM_GROUPS = 4
GM_CHUNK = 128
SW_HEADS = 8
SW_KV_HEADS = 2
SW_HD = 64
SW_BLOCK = 128
XM_HEADS = 4
XM_HD = 128
EPS = 1e-6
NEG_INF = -1e30
LANES = 128

CKV_COLS = 2 * SW_KV_HEADS * SW_HD
OFF_AQKV = 0
OFF_AZ = OFF_AQKV + 3 * BRANCH_W
OFF_SMALL = OFF_AZ + BRANCH_W
N_SMALL = 2 * DN_HEADS
OFF_BUV = OFF_SMALL + LANES
OFF_BZ = OFF_BUV + 2 * BRANCH_W
OFF_CQ = OFF_BZ + BRANCH_W
OFF_CKV = OFF_CQ + BRANCH_W
OFF_CZ = OFF_CKV + CKV_COLS
OFF_MQ = OFF_CZ + BRANCH_W
OFF_MZ = OFF_MQ + BRANCH_W
OFF_GATES = OFF_MZ + BRANCH_W
W_COLS = OFF_GATES + N_BRANCH * D_MODEL

VMEM_LIMIT = 60 * 1024 * 1024
TT_LAYER = 256
STEP_ROWS = 512
PROJ_TILE = 256
BODY_AT = (12, 20, 27)
ITEMS_AT = {}


def _sigmoid(x):
    return 1.0 / (1.0 + jnp.exp(-x))


def _silu(x):
    return x * _sigmoid(x)


def _softplus(x):
    return jnp.maximum(x, 0.0) + jnp.log(1.0 + jnp.exp(-jnp.abs(x)))


def _rms(x, gain):
    return x * lax.rsqrt(jnp.mean(x * x, axis=-1, keepdims=True) + EPS) * gain


def _dot(a, b):
    return jnp.dot(a, b, preferred_element_type=F32)


def _dot_nt(a, b):
    return lax.dot_general(a, b, (((1,), (1,)), ((), ())), preferred_element_type=F32)


PADW_COLS = 1024
PAD_ROWS = LANES - N_SMALL


def _pad_w_kernel(prev_ref, cur_ref, o_ref):
    i = pl.program_id(1)
    small_blk = OFF_SMALL // PADW_COLS
    keep = PADW_COLS - PAD_ROWS

    @pl.when(i < small_blk)
    def _():
        o_ref[...] = cur_ref[...].T.astype(BF16)

    @pl.when(i == small_blk)
    def _():
        cur = cur_ref[...]
        src = jnp.concatenate([cur[:N_SMALL], jnp.zeros((PAD_ROWS, cur.shape[1]), F32), cur[N_SMALL:keep]], axis=0)
        o_ref[...] = src.T.astype(BF16)

    @pl.when(i > small_blk)
    def _():
        src = jnp.concatenate([prev_ref[LANES - PAD_ROWS:], cur_ref[:keep]], axis=0)
        o_ref[...] = src.T.astype(BF16)


def _pad_w_in(w_t):
    nl, n_in, d = w_t.shape
    assert n_in == W_COLS - PAD_ROWS and OFF_SMALL % PADW_COLS == 0 and OFF_BUV - OFF_SMALL == LANES
    sub = PADW_COLS // LANES
    return pl.pallas_call(
        _pad_w_kernel,
        grid=(nl, pl.cdiv(W_COLS, PADW_COLS)),
        in_specs=[
            pl.BlockSpec((None, LANES, d), lambda l, i: (l, jnp.maximum(i * sub - 1, 0), 0)),
            pl.BlockSpec((None, PADW_COLS, d), lambda l, i: (l, i, 0)),
        ],
        out_specs=pl.BlockSpec((None, d, PADW_COLS), lambda l, i: (l, 0, i)),
        out_shape=jax.ShapeDtypeStruct((nl, d, W_COLS), BF16),
        compiler_params=pltpu.CompilerParams(
            dimension_semantics=("arbitrary", "arbitrary"), vmem_limit_bytes=48 * 1024 * 1024),
        name="padw",
    )(w_t, w_t)


def _memkv_kernel(m_ref, g_ref, w_ref, o_ref):
    o_ref[...] = _dot(_rms(m_ref[...], g_ref[...]).astype(BF16), w_ref[...]).astype(BF16)


def _memkv(mem, gain, w):
    b, ml, _ = mem.shape
    nl = w.shape[0]
    return pl.pallas_call(
        _memkv_kernel,
        grid=(nl, b),
        in_specs=[
            pl.BlockSpec((None, ml, D_MODEL), lambda l, i: (i, 0, 0)),
            pl.BlockSpec((None, 1, D_MODEL), lambda l, i: (l, 0, 0)),
            pl.BlockSpec((None, D_MODEL, 2 * BRANCH_W), lambda l, i: (l, 0, 0)),
        ],
        out_specs=pl.BlockSpec((None, None, ml, 2 * BRANCH_W), lambda l, i: (l, i, 0, 0)),
        out_shape=jax.ShapeDtypeStruct((nl, b, ml, 2 * BRANCH_W), BF16),
        compiler_params=pltpu.CompilerParams(dimension_semantics=("arbitrary", "arbitrary")),
        name="memkv",
    )(mem, gain, w)


def _deltanet_stages(qkv, z, sm, cw_ref, gp_ref, on_ref, y_ref, tail_ref, s_ref, *, tt):
    H, C, DK = DN_HEADS, DN_CHUNK, DN_DK
    HC = H * C
    cw = cw_ref[...]

    def conv_silu(idx):
        lo = idx * BRANCH_W
        cur = qkv[:, lo:lo + BRANCH_W]
        ext = jnp.concatenate([tail_ref[idx], cur], axis=0)
        acc = cur * cw[CONV_W - 1:CONV_W, lo:lo + BRANCH_W]
        for j in range(1, CONV_W):
            shifted = pltpu.roll(ext, j, axis=0)[8:]
            acc = acc + shifted * cw[CONV_W - 1 - j:CONV_W - j, lo:lo + BRANCH_W]
        tail_ref[idx] = cur[tt - 8:]
        return _silu(acc)

    qc = conv_silu(0)
    yield
    kc = conv_silu(1)
    yield
    vc = conv_silu(2)
    yield

    def l2n(xh):
        return xh * lax.rsqrt(jnp.sum(xh * xh, axis=-1, keepdims=True) + EPS)

    ci = lax.broadcasted_iota(jnp.int32, (C, HC), 0)
    cl = lax.broadcasted_iota(jnp.int32, (C, HC), 1)
    cj = cl % C
    strict = ci > cj
    incl = ci >= cj
    eye_cat = (ci == cj).astype(F32)
    lane_head = cl // C
    br = lax.broadcasted_iota(jnp.int32, (HC, HC), 0)
    bc = lax.broadcasted_iota(jnp.int32, (HC, HC), 1)
    bd_mask = jnp.where((br // C) == (bc // C), 1.0, 0.0).astype(BF16)
    kl = lax.broadcasted_iota(jnp.int32, (DK, HC), 1) // C
    head_lane_masks = [jnp.where(kl == h, 1.0, 0.0).astype(BF16) for h in range(H)]
    l64 = lax.broadcasted_iota(jnp.int32, (C, DK), 1) < C
    tr = lax.broadcasted_iota(jnp.int32, (C, C), 0)
    tc = lax.broadcasted_iota(jnp.int32, (C, C), 1)
    ltri = jnp.where(tr >= tc, 1.0, 0.0).astype(BF16)

    neg_a = gp_ref[0:1, :]
    dt_b = gp_ref[1:2, :]
    o_gain = on_ref[...]

    def stack(x, lo):
        return jnp.concatenate([x[lo:lo + C, h * DK:(h + 1) * DK] for h in range(H)], axis=0)

    def block_diag(x_cat):
        return jnp.concatenate([x_cat.astype(BF16)] * H, axis=0) * bd_mask

    def diag_to_cat(m):
        out = m[(H - 1) * C:H * C]
        for h in range(H - 2, -1, -1):
            out = jnp.where(lane_head == h, m[h * C:(h + 1) * C], out)
        return out

    chunks = range(tt // C)
    qs, ks, vs, beta_all, gc3 = [], [], [], [], []
    for c in chunks:
        lo = c * C
        qs.append(jnp.concatenate([l2n(qc[lo:lo + C, h * DK:(h + 1) * DK]) for h in range(H)], axis=0) * (DK ** -0.5))
        ks.append(jnp.concatenate([l2n(kc[lo:lo + C, h * DK:(h + 1) * DK]) for h in range(H)], axis=0))
        vs.append(stack(vc, lo))
        smc = sm[lo:lo + C, :]
        beta_all.append(_sigmoid(smc))
        g_all = neg_a * _softplus(smc + dt_b)
        g_hi = g_all.astype(BF16)
        r1 = g_all - g_hi.astype(F32)
        g_mid = r1.astype(BF16)
        g_lo = (r1 - g_mid.astype(F32)).astype(BF16)
        gc3.append(_dot(ltri, jnp.concatenate([g_hi, g_mid, g_lo], axis=1)))
        yield

    cb, bb, decay, egc, kb, kq = [], [], [], [], [], []
    for c in chunks:
        gc_all = gc3[c][:, :LANES] + gc3[c][:, LANES:2 * LANES] + gc3[c][:, 2 * LANES:]
        gcb = [jnp.broadcast_to(gc_all[:, H + h:H + h + 1], (C, DK)) for h in range(H)]
        cb.append(jnp.concatenate(gcb, axis=0))
        bb.append(jnp.concatenate(
            [jnp.broadcast_to(beta_all[c][:, h:h + 1], (C, DK)) for h in range(H)], axis=0))
        c_cat = jnp.concatenate([jnp.where(l64, gcb[0], gcb[1]), jnp.where(l64, gcb[2], gcb[3])], axis=1)
        gct = jnp.concatenate([gc_all, gc_all], axis=0).T
        r_cat = jnp.concatenate(
            [jnp.where(l64[0:1], gct[H:H + 1], gct[H + 1:H + 2]),
             jnp.where(l64[0:1], gct[H + 2:H + 3], gct[H + 3:H + 4])], axis=1)
        decay.append(jnp.exp(jnp.where(incl, c_cat - r_cat, 0.0)))
        egc.append(jnp.exp(cb[c]))
        kb.append(ks[c] * bb[c])
        kq.append(_dot_nt(jnp.concatenate([kb[c], qs[c]], axis=0).astype(BF16), ks[c].astype(BF16)))
        yield

    a_cat, p_cat, cur = [], [], []
    for c in chunks:
        n_cat = -jnp.where(strict, diag_to_cat(kq[c][:HC]) * decay[c], 0.0)
        a_cat.append(jnp.where(incl, diag_to_cat(kq[c][HC:]) * decay[c], 0.0))
        p_cat.append(eye_cat + n_cat)
        cur.append(_dot(n_cat.astype(BF16), block_diag(n_cat)))
        yield
    for _ in range(4):
        for c in chunks:
            out = _dot(jnp.concatenate([cur[c], p_cat[c]], axis=0).astype(BF16), block_diag(cur[c]))
            p_cat[c] = p_cat[c] + out[C:]
            cur[c] = out[:C]
            yield
    t_cat = [p_cat[c] + _dot(p_cat[c].astype(BF16), block_diag(cur[c])) for c in chunks]
    yield
    solb = []
    for c in chunks:
        rhs = jnp.concatenate([vs[c] * bb[c], kb[c] * egc[c]], axis=1)
        solb.append(_dot(block_diag(t_cat[c]), rhs.astype(BF16)).astype(BF16))
        yield
    auw, qp, gb = [], [], []
    for c in chunks:
        auw.append(_dot(block_diag(a_cat[c]), solb[c]))
        qp.append(qs[c] * egc[c] - auw[c][:, DK:])
        glast = jnp.concatenate(
            [jnp.broadcast_to(cb[c][h * C + C - 1:h * C + C, :], (C, DK)) for h in range(H)], axis=0)
        kd_t = (ks[c] * jnp.exp(glast - cb[c])).T.astype(BF16)
        gb.append([_dot(kd_t * head_lane_masks[h], solb[c]) for h in range(H)])
        yield

    for c in chunks:
        lo = c * C
        for h in range(H):
            s_h = s_ref[:, h * DK:(h + 1) * DK]
            gq = _dot(jnp.concatenate([gb[c][h][:, DK:], qp[c][h * C:(h + 1) * C]], axis=0).astype(BF16),
                      s_h.astype(BF16))
            d_h = jnp.exp(cb[c][h * C + C - 1:h * C + C, :])
            s_ref[:, h * DK:(h + 1) * DK] = s_h * d_h - gq[:DK] + gb[c][h][:, :DK]
            oh = _rms(gq[DK:] + auw[c][h * C:(h + 1) * C, :DK], o_gain)
            zh = z[lo:lo + C, h * DK:(h + 1) * DK]
            y_ref[lo:lo + C, h * DK:(h + 1) * DK] = (oh * _silu(zh)).astype(BF16)
        yield


def _gmlp_branch(uv, z, gain_ref, ws_ref, bias_ref, y_ref, tick, *, tt):
    P, G = GM_CHUNK, GM_GROUPS
    u = jax.nn.gelu(uv[:, :BRANCH_W])
    tick()
    v = _rms(jax.nn.gelu(uv[:, BRANCH_W:]), gain_ref[...])
    tick()
    vb = v.astype(BF16)
    tr = lax.broadcasted_iota(jnp.int32, (P, P), 0)
    tc = lax.broadcasted_iota(jnp.int32, (P, P), 1)
    causal = tr >= tc
    bias = bias_ref[...]
    for g in range(G):
        wg = jnp.where(causal, ws_ref[g], 0.0).astype(BF16)
        for ci in range(tt // P):
            lo = ci * P
            s = _dot(wg, vb[lo:lo + P, g * P:(g + 1) * P]) + bias[:, g * P:(g + 1) * P]
            zg = z[lo:lo + P, g * P:(g + 1) * P]
            y_ref[lo:lo + P, g * P:(g + 1) * P] = (u[lo:lo + P, g * P:(g + 1) * P] * s * _silu(zg)).astype(BF16)
        tick()


def _swa_branch(sink_of, q, z, kv_cur, y_ref, kvprev_ref, tick, has_prev, *, tt):
    P, HD, KV = SW_BLOCK, SW_HD, SW_KV_HEADS
    G = SW_HEADS // KV
    kj = lax.broadcasted_iota(jnp.int32, (2 * P, P), 0)
    qi = lax.broadcasted_iota(jnp.int32, (2 * P, P), 1)
    dist = qi + P - kj
    window = jnp.logical_and(dist >= 0, dist < P)
    valid_mid = jnp.concatenate([window] * G, axis=1)
    valid_first = jnp.concatenate([jnp.logical_and(window, jnp.logical_or(kj >= P, has_prev))] * G, axis=1)
    kv = jnp.concatenate([kvprev_ref[...], kv_cur], axis=0)
    kvprev_ref[...] = kv_cur[tt - P:, :]
    qb = (q * (HD ** -0.5)).astype(BF16)
    for kh in range(KV):
        sink = jnp.concatenate([jnp.full((1, P), sink_of(kh * G + g), F32) for g in range(G)], axis=1)
        k_all = kv[:, kh * HD:(kh + 1) * HD].astype(BF16)
        v_all_t = kv[:, (KV + kh) * HD:(KV + kh + 1) * HD].T.astype(BF16)
        for j in range(tt // P):
            lo = j * P
            qs = jnp.concatenate(
                [qb[lo:lo + P, (kh * G + g) * HD:(kh * G + g + 1) * HD] for g in range(G)], axis=0)
            s = jnp.where(valid_first if j == 0 else valid_mid, _dot_nt(k_all[lo:lo + 2 * P], qs), NEG_INF)
            mx = jnp.maximum(jnp.max(s, axis=0, keepdims=True), sink)
            e = jnp.exp(s - mx)
            den = jnp.sum(e, axis=0, keepdims=True) + jnp.exp(sink - mx)
            o_t = _dot(v_all_t[:, lo:lo + 2 * P], e.astype(BF16)) * (1.0 / den)
            for g2 in range(G // 2):
                h0 = kh * G + 2 * g2
                pair = jnp.concatenate(
                    [o_t[:, 2 * g2 * P:(2 * g2 + 1) * P], o_t[:, (2 * g2 + 1) * P:(2 * g2 + 2) * P]], axis=0).T
                zp = z[lo:lo + P, h0 * HD:(h0 + 2) * HD]
                y_ref[lo:lo + P, h0 * HD:(h0 + 2) * HD] = (pair * _silu(zp)).astype(BF16)
            tick()


def _memattn_branch(q, z, kv_ref, y_ref, tick):
    HD = XM_HD
    qb = q.astype(BF16)
    for h in range(XM_HEADS):
        mk = kv_ref[:, h * HD:(h + 1) * HD]
        mv = kv_ref[:, BRANCH_W + h * HD:BRANCH_W + (h + 1) * HD]
        s = _dot_nt(qb[:, h * HD:(h + 1) * HD], mk) * (HD ** -0.5)
        e = jnp.exp(s - jnp.max(s, axis=-1, keepdims=True))
        o = _dot(e.astype(BF16), mv) * (1.0 / jnp.sum(e, axis=-1, keepdims=True))
        y_ref[:, h * HD:(h + 1) * HD] = (o * _silu(z[:, h * HD:(h + 1) * HD])).astype(BF16)
        tick()


def _layer_kernel(sinks_ref, x_ref, npre_ref, w_ref, cw_ref, gp_ref, on_ref, gmn_ref, ws_ref, bias_ref,
                  mkv_ref, wup_ref, wout_ref, npost_ref, o_ref, tail_ref, s_ref, kvprev_ref, y_ref, cols_ref,
                  merged_ref, *, step_rows, tt, layer):
    @pl.when(pl.program_id(1) == 0)
    def _():
        tail_ref[...] = jnp.zeros_like(tail_ref)
        s_ref[...] = jnp.zeros_like(s_ref)
        kvprev_ref[...] = jnp.zeros_like(kvprev_ref)

    for si in range(step_rows // tt):
        rows = pl.ds(si * tt, tt)
        has_prev = pl.program_id(1) > 0 if si == 0 else True
        _layer_tile(sinks_ref, x_ref.at[rows], npre_ref, w_ref, cw_ref, gp_ref, on_ref, gmn_ref, ws_ref, bias_ref,
                    mkv_ref, wup_ref, wout_ref, npost_ref, o_ref.at[rows], tail_ref, s_ref, kvprev_ref, y_ref,
                    cols_ref, merged_ref, has_prev, tt=tt, layer=layer)


def _layer_tile(sinks_ref, x_ref, npre_ref, w_ref, cw_ref, gp_ref, on_ref, gmn_ref, ws_ref, bias_ref,
                mkv_ref, wup_ref, wout_ref, npost_ref, o_ref, tail_ref, s_ref, kvprev_ref, y_ref, cols_ref,
                merged_ref, has_prev, *, tt, layer):
    x = x_ref[...]
    h = _rms(x, npre_ref[...]).astype(BF16)

    def proj(lo, width):
        return _dot(h, w_ref[:, lo:lo + width])

    filled = [OFF_BUV]
    merged_started = set()

    def proj_item(lo):
        def run():
            cols_ref[:, lo - OFF_BUV:lo - OFF_BUV + PROJ_TILE] = proj(lo, PROJ_TILE)
            filled[0] = lo + PROJ_TILE
        return run

    def col(lo, width):
        assert lo + width <= filled[0], "projection read before it was issued"
        return cols_ref[:, lo - OFF_BUV:lo - OFF_BUV + width]

    def merge_item(n, lo):
        def run():
            g = _sigmoid(col(OFF_GATES + n * D_MODEL + lo, PROJ_TILE))
            term = g * _dot(y_ref[n], wup_ref[n, :, lo:lo + PROJ_TILE])
            if lo in merged_started:
                merged_ref[:, lo:lo + PROJ_TILE] += term
            else:
                merged_ref[:, lo:lo + PROJ_TILE] = term
                merged_started.add(lo)
        return run

    queue = collections.deque(proj_item(lo) for lo in range(OFF_BUV, W_COLS, PROJ_TILE))

    def tick(count=1):
        for _ in range(count):
            if queue:
                queue.popleft()()

    def branch_done(n):
        queue.extend(merge_item(n, lo) for lo in range(0, D_MODEL, PROJ_TILE))

    def branch_b():
        _gmlp_branch(col(OFF_BUV, 2 * BRANCH_W), col(OFF_BZ, BRANCH_W), gmn_ref, ws_ref, bias_ref,
                     y_ref.at[1], tick, tt=tt)
        branch_done(1)

    def branch_c():
        _swa_branch(lambda hd: sinks_ref[layer, hd], col(OFF_CQ, BRANCH_W), col(OFF_CZ, BRANCH_W),
                    col(OFF_CKV, CKV_COLS), y_ref.at[2], kvprev_ref, tick, has_prev, tt=tt)
        branch_done(2)

    def branch_m():
        _memattn_branch(col(OFF_MQ, BRANCH_W), col(OFF_MZ, BRANCH_W), mkv_ref, y_ref.at[3], tick)
        branch_done(3)

    bodies = {BODY_AT[0]: branch_b, BODY_AT[1]: branch_c, BODY_AT[2]: branch_m}
    stages = _deltanet_stages(proj(OFF_AQKV, 3 * BRANCH_W), proj(OFF_AZ, BRANCH_W), proj(OFF_SMALL, LANES),
                              cw_ref, gp_ref, on_ref, y_ref.at[0], tail_ref, s_ref, tt=tt)
    for i, _ in enumerate(stages):
        tick(ITEMS_AT.get(i, 1))
        if i in bodies:
            bodies.pop(i)()
    assert not bodies, "stage list shorter than BODY_AT"
    branch_done(0)
    while queue:
        tick()

    out = _dot(merged_ref[...].astype(BF16), wout_ref[...])
    o_ref[...] = x + _rms(out, npost_ref[...])


def _layer_call(x2, layer, sinks, norm_pre, w_pad, conv_w, gate_params, o_norm, gm_norm, spatial_w, bias_mat,
                mem_kv, w_up, w_out, norm_post, *, batch, seq, step_rows, tt):
    nt = seq // step_rows
    ml = mem_kv.shape[2]
    row = lambda b, t, s: (b * nt + t, 0)

    def const(arr):
        shape = arr.shape[1:]
        return pl.BlockSpec((None,) + shape, lambda b, t, s: (layer,) + (0,) * len(shape),
                            pipeline_mode=pl.Buffered(1))

    in_specs = [
        pl.BlockSpec((step_rows, D_MODEL), row),
        const(norm_pre), const(w_pad), const(conv_w), const(gate_params), const(o_norm), const(gm_norm),
        const(spatial_w), const(bias_mat),
        pl.BlockSpec((None, None, ml, 2 * BRANCH_W), lambda b, t, s: (layer, b, 0, 0)),
        const(w_up), const(w_out), const(norm_post),
    ]
    return pl.pallas_call(
        functools.partial(_layer_kernel, step_rows=step_rows, tt=tt, layer=layer),
        grid_spec=pltpu.PrefetchScalarGridSpec(
            num_scalar_prefetch=1,
            grid=(batch, nt),
            in_specs=in_specs,
            out_specs=pl.BlockSpec((step_rows, D_MODEL), row),
            scratch_shapes=[
                pltpu.VMEM((3, 8, BRANCH_W), F32),
                pltpu.VMEM((DN_DK, DN_HEADS * DN_DK), F32),
                pltpu.VMEM((SW_BLOCK, CKV_COLS), F32),
                pltpu.VMEM((N_BRANCH, tt, BRANCH_W), BF16),
                pltpu.VMEM((tt, W_COLS - OFF_BUV), F32),
                pltpu.VMEM((tt, D_MODEL), F32),
            ],
        ),
        out_shape=jax.ShapeDtypeStruct((batch * seq, D_MODEL), F32),
        compiler_params=pltpu.CompilerParams(
            dimension_semantics=("arbitrary", "arbitrary"), vmem_limit_bytes=VMEM_LIMIT),
        name="layer",
    )(sinks, x2, norm_pre, w_pad, conv_w, gate_params, o_norm, gm_norm, spatial_w, bias_mat, mem_kv,
      w_up, w_out, norm_post)


def kernel(x, mem, norm_pre, norm_post, norm_mem, w_in, conv_w, a_log, dt_bias, dn_norm, gm_norm,
           spatial_w, spatial_b, sinks, w_mem_kv, w_up, w_out):
    batch, seq, d = x.shape
    nl = w_in.shape[0]
    w_pad = _pad_w_in(jnp.swapaxes(w_in, 1, 2))
    lane_pad = ((0, 0), (DN_HEADS, LANES - 2 * DN_HEADS))
    gate_params = jnp.stack([jnp.pad(-jnp.exp(a_log), lane_pad), jnp.pad(dt_bias, lane_pad)], axis=1)
    bias_mat = jnp.repeat(jnp.swapaxes(spatial_b, 1, 2), GM_CHUNK, axis=2)
    mem_kv = _memkv(mem, norm_mem[:, None, :], w_mem_kv.astype(BF16))
    w_up_b = w_up.astype(BF16)
    w_out_b = w_out.astype(BF16)

    x2 = x.reshape(batch * seq, d)
    for l in range(nl):
        x2 = _layer_call(x2, l, sinks, norm_pre[:, None, :], w_pad, conv_w, gate_params, dn_norm[:, None, :],
                         gm_norm[:, None, :], spatial_w, bias_mat, mem_kv, w_up_b, w_out_b,
                         norm_post[:, None, :], batch=batch, seq=seq, step_rows=STEP_ROWS, tt=TT_LAYER)
    return x2.reshape(batch, seq, d)
```

```python
import collections
import functools

import jax
import jax.numpy as jnp
from jax import lax
from jax.experimental import pallas as pl
from jax.experimental.pallas import tpu as pltpu

F32 = jnp.float32
BF16 = jnp.bfloat16

D_MODEL = 1024
N_BRANCH = 4
BRANCH_W = 512
DN_HEADS = 4
DN_DK = 128
DN_CHUNK = 64
CONV_W = 4
GM_GROUPS = 4
GM_CHUNK = 128
SW_HEADS = 8
SW_KV_HEADS = 2
SW_HD = 64
SW_BLOCK = 128
XM_HEADS = 4
XM_HD = 128
EPS = 1e-6
NEG_INF = -1e30
LANES = 128

CKV_COLS = 2 * SW_KV_HEADS * SW_HD
OFF_AQKV = 0
OFF_AZ = OFF_AQKV + 3 * BRANCH_W
OFF_SMALL = OFF_AZ + BRANCH_W
N_SMALL = 2 * DN_HEADS
OFF_BUV = OFF_SMALL + LANES
OFF_BZ = OFF_BUV + 2 * BRANCH_W
OFF_CQ = OFF_BZ + BRANCH_W
OFF_CKV = OFF_CQ + BRANCH_W
OFF_CZ = OFF_CKV + CKV_COLS
OFF_MQ = OFF_CZ + BRANCH_W
OFF_MZ = OFF_MQ + BRANCH_W
OFF_GATES = OFF_MZ + BRANCH_W
W_COLS = OFF_GATES + N_BRANCH * D_MODEL

VMEM_LIMIT = 60 * 1024 * 1024
TT_LAYER = 256
STEP_ROWS = 256
PROJ_TILE = 256
BODY_AT = (6, 20, 30)
ITEMS_AT = {}


def _sigmoid(x):
    return 0.5 * jnp.tanh(0.5 * x) + 0.5


def _silu(x):
    return x * _sigmoid(x)


def _softplus(x):
    return jnp.maximum(x, 0.0) + jnp.log(1.0 + jnp.exp(-jnp.abs(x)))


def _rms(x, gain):
    return x * lax.rsqrt(jnp.mean(x * x, axis=-1, keepdims=True) + EPS) * gain


def _dot(a, b):
    return jnp.dot(a, b, preferred_element_type=F32)


def _dot_nt(a, b):
    return lax.dot_general(a, b, (((1,), (1,)), ((), ())), preferred_element_type=F32)


PADW_COLS = 1024
PAD_ROWS = LANES - N_SMALL


def _pad_w_kernel(prev_ref, cur_ref, o_ref):
    i = pl.program_id(1)
    small_blk = OFF_SMALL // PADW_COLS
    keep = PADW_COLS - PAD_ROWS

    @pl.when(i < small_blk)
    def _():
        o_ref[...] = cur_ref[...].T.astype(BF16)

    @pl.when(i == small_blk)
    def _():
        cur = cur_ref[...]
        src = jnp.concatenate([cur[:N_SMALL], jnp.zeros((PAD_ROWS, cur.shape[1]), F32), cur[N_SMALL:keep]], axis=0)
        o_ref[...] = src.T.astype(BF16)

    @pl.when(i > small_blk)
    def _():
        src = jnp.concatenate([prev_ref[LANES - PAD_ROWS:], cur_ref[:keep]], axis=0)
        o_ref[...] = src.T.astype(BF16)


def _pad_w_in(w_t):
    nl, n_in, d = w_t.shape
    assert n_in == W_COLS - PAD_ROWS and OFF_SMALL % PADW_COLS == 0 and OFF_BUV - OFF_SMALL == LANES
    sub = PADW_COLS // LANES
    return pl.pallas_call(
        _pad_w_kernel,
        grid=(nl, pl.cdiv(W_COLS, PADW_COLS)),
        in_specs=[
            pl.BlockSpec((None, LANES, d), lambda l, i: (l, jnp.maximum(i * sub - 1, 0), 0)),
            pl.BlockSpec((None, PADW_COLS, d), lambda l, i: (l, i, 0)),
        ],
        out_specs=pl.BlockSpec((None, d, PADW_COLS), lambda l, i: (l, 0, i)),
        out_shape=jax.ShapeDtypeStruct((nl, d, W_COLS), BF16),
        compiler_params=pltpu.CompilerParams(
            dimension_semantics=("arbitrary", "arbitrary"), vmem_limit_bytes=48 * 1024 * 1024),
        name="padw",
    )(w_t, w_t)


def _memkv_kernel(m_ref, g_ref, w_ref, o_ref):
    wb = w_ref[...].astype(BF16)
    rows = 512
    for lo in range(0, m_ref.shape[0], rows):
        o_ref[lo:lo + rows] = _dot(_rms(m_ref[lo:lo + rows], g_ref[...]).astype(BF16), wb).astype(BF16)


def _memkv(mem, gain, w):
    b, ml, _ = mem.shape
    nl = w.shape[0]
    out = pl.pallas_call(
        _memkv_kernel,
        grid=(nl,),
        in_specs=[
            pl.BlockSpec((b * ml, D_MODEL), lambda l: (0, 0)),
            pl.BlockSpec((None, 1, D_MODEL), lambda l: (l, 0, 0)),
            pl.BlockSpec((None, D_MODEL, 2 * BRANCH_W), lambda l: (l, 0, 0)),
        ],
        out_specs=pl.BlockSpec((None, b * ml, 2 * BRANCH_W), lambda l: (l, 0, 0)),
        out_shape=jax.ShapeDtypeStruct((nl, b * ml, 2 * BRANCH_W), BF16),
        compiler_params=pltpu.CompilerParams(
            dimension_semantics=("arbitrary",), vmem_limit_bytes=48 * 1024 * 1024),
        name="memkv",
    )(mem.reshape(b * ml, D_MODEL), gain, w)
    return out.reshape(nl, b, ml, 2 * BRANCH_W)


def _deltanet_stages(qkv, z, sm, cw_ref, gp_ref, on_ref, y_ref, tail_ref, s_ref, *, tt):
    H, C, DK = DN_HEADS, DN_CHUNK, DN_DK
    HC = H * C
    cw = cw_ref[...]

    def conv_silu(idx):
        lo = idx * BRANCH_W
        cur = qkv[:, lo:lo + BRANCH_W]
        ext = jnp.concatenate([tail_ref[idx], cur], axis=0)
        acc = cur * cw[CONV_W - 1:CONV_W, lo:lo + BRANCH_W]
        for j in range(1, CONV_W):
            shifted = pltpu.roll(ext, j, axis=0)[8:]
            acc = acc + shifted * cw[CONV_W - 1 - j:CONV_W - j, lo:lo + BRANCH_W]
        tail_ref[idx] = cur[tt - 8:]
        return _silu(acc)

    qc = conv_silu(0)
    yield
    kc = conv_silu(1)
    yield
    vc = conv_silu(2)
    yield

    def l2n(xh):
        return xh * lax.rsqrt(jnp.sum(xh * xh, axis=-1, keepdims=True) + EPS)

    ci = lax.broadcasted_iota(jnp.int32, (C, HC), 0)
    cl = lax.broadcasted_iota(jnp.int32, (C, HC), 1)
    cj = cl % C
    strict = ci > cj
    incl = ci >= cj
    eye_cat = (ci == cj).astype(F32)
    lane_head = cl // C
    br = lax.broadcasted_iota(jnp.int32, (HC, HC), 0)
    bc = lax.broadcasted_iota(jnp.int32, (HC, HC), 1)
    bd_mask = jnp.where((br // C) == (bc // C), 1.0, 0.0).astype(BF16)
    kl = lax.broadcasted_iota(jnp.int32, (DK, HC), 1) // C
    head_lane_masks = [jnp.where(kl == h, 1.0, 0.0).astype(BF16) for h in range(H)]
    l64 = lax.broadcasted_iota(jnp.int32, (C, DK), 1) < C
    tr = lax.broadcasted_iota(jnp.int32, (C, C), 0)
    tc = lax.broadcasted_iota(jnp.int32, (C, C), 1)
    ltri = jnp.where(tr >= tc, 1.0, 0.0).astype(BF16)

    neg_a = gp_ref[0:1, :]
    dt_b = gp_ref[1:2, :]
    o_gain = on_ref[...]

    def stack(x, lo):
        return jnp.concatenate([x[lo:lo + C, h * DK:(h + 1) * DK] for h in range(H)], axis=0)

    def block_diag(x_cat):
        return jnp.concatenate([x_cat.astype(BF16)] * H, axis=0) * bd_mask

    def diag_to_cat(m):
        out = m[(H - 1) * C:H * C]
        for h in range(H - 2, -1, -1):
            out = jnp.where(lane_head == h, m[h * C:(h + 1) * C], out)
        return out

    chunks = range(tt // C)
    qs, ks, vs, beta_all, gc3 = [], [], [], [], []
    for c in chunks:
        lo = c * C
        qs.append(jnp.concatenate([l2n(qc[lo:lo + C, h * DK:(h + 1) * DK]) for h in range(H)], axis=0) * (DK ** -0.5))
        ks.append(jnp.concatenate([l2n(kc[lo:lo + C, h * DK:(h + 1) * DK]) for h in range(H)], axis=0))
        vs.append(stack(vc, lo))
        smc = sm[lo:lo + C, :]
        beta_all.append(_sigmoid(smc))
        g_all = neg_a * _softplus(smc + dt_b)
        g_hi = g_all.astype(BF16)
        r1 = g_all - g_hi.astype(F32)
        g_mid = r1.astype(BF16)
        g_lo = (r1 - g_mid.astype(F32)).astype(BF16)
        gc3.append(_dot(ltri, jnp.concatenate([g_hi, g_mid, g_lo], axis=1)))
        yield

    cb, bb, decay, egc, kb, kq = [], [], [], [], [], []
    for c in chunks:
        gc_all = gc3[c][:, :LANES] + gc3[c][:, LANES:2 * LANES] + gc3[c][:, 2 * LANES:]
        gcb = [jnp.broadcast_to(gc_all[:, H + h:H + h + 1], (C, DK)) for h in range(H)]
        cb.append(jnp.concatenate(gcb, axis=0))
        bb.append(jnp.concatenate(
            [jnp.broadcast_to(beta_all[c][:, h:h + 1], (C, DK)) for h in range(H)], axis=0))
        c_cat = jnp.concatenate([jnp.where(l64, gcb[0], gcb[1]), jnp.where(l64, gcb[2], gcb[3])], axis=1)
        gct = jnp.concatenate([gc_all, gc_all], axis=0).T
        r_cat = jnp.concatenate(
            [jnp.where(l64[0:1], gct[H:H + 1], gct[H + 1:H + 2]),
             jnp.where(l64[0:1], gct[H + 2:H + 3], gct[H + 3:H + 4])], axis=1)
        decay.append(jnp.exp(jnp.where(incl, c_cat - r_cat, 0.0)))
        egc.append(jnp.exp(cb[c]))
        kb.append(ks[c] * bb[c])
        kq.append(_dot_nt(jnp.concatenate([kb[c], qs[c]], axis=0).astype(BF16), ks[c].astype(BF16)))
        yield

    a_cat, p_cat, cur = [], [], []
    for c in chunks:
        n_cat = -jnp.where(strict, diag_to_cat(kq[c][:HC]) * decay[c], 0.0)
        a_cat.append(jnp.where(incl, diag_to_cat(kq[c][HC:]) * decay[c], 0.0))
        p_cat.append(eye_cat + n_cat)
        cur.append(_dot(n_cat.astype(BF16), block_diag(n_cat)))
        yield
    for _ in range(4):
        for c in chunks:
            out = _dot(jnp.concatenate([cur[c], p_cat[c]], axis=0).astype(BF16), block_diag(cur[c]))
            p_cat[c] = p_cat[c] + out[C:]
            cur[c] = out[:C]
            yield
    t_cat = [p_cat[c] + _dot(p_cat[c].astype(BF16), block_diag(cur[c])) for c in chunks]
    yield
    solb = []
    for c in chunks:
        rhs = jnp.concatenate([vs[c] * bb[c], kb[c] * egc[c]], axis=1)
        solb.append(_dot(block_diag(t_cat[c]), rhs.astype(BF16)).astype(BF16))
        yield
    auw, qp, gb = [], [], []
    for c in chunks:
        auw.append(_dot(block_diag(a_cat[c]), solb[c]))
        qp.append(qs[c] * egc[c] - auw[c][:, DK:])
        glast = jnp.concatenate(
            [jnp.broadcast_to(cb[c][h * C + C - 1:h * C + C, :], (C, DK)) for h in range(H)], axis=0)
        kd_t = (ks[c] * jnp.exp(glast - cb[c])).T.astype(BF16)
        gb.append([_dot(kd_t * head_lane_masks[h], solb[c]) for h in range(H)])
        yield

    for c in chunks:
        lo = c * C
        for h in range(H):
            s_h = s_ref[:, h * DK:(h + 1) * DK]
            gq = _dot(jnp.concatenate([gb[c][h][:, DK:], qp[c][h * C:(h + 1) * C]], axis=0).astype(BF16),
                      s_h.astype(BF16))
            d_h = jnp.exp(cb[c][h * C + C - 1:h * C + C, :])
            s_ref[:, h * DK:(h + 1) * DK] = s_h * d_h - gq[:DK] + gb[c][h][:, :DK]
            oh = _rms(gq[DK:] + auw[c][h * C:(h + 1) * C, :DK], o_gain)
            zh = z[lo:lo + C, h * DK:(h + 1) * DK]
            y_ref[lo:lo + C, h * DK:(h + 1) * DK] = (oh * _silu(zh)).astype(BF16)
        yield


def _gmlp_branch(uv, z, gain_ref, ws_ref, bias_ref, y_ref, tick, *, tt):
    P, G = GM_CHUNK, GM_GROUPS
    u = jax.nn.gelu(uv[:, :BRANCH_W])
    tick()
    v = _rms(jax.nn.gelu(uv[:, BRANCH_W:]), gain_ref[...])
    tick()
    vb = v.astype(BF16)
    tr = lax.broadcasted_iota(jnp.int32, (P, P), 0)
    tc = lax.broadcasted_iota(jnp.int32, (P, P), 1)
    causal = tr >= tc
    bias = bias_ref[...]
    for g in range(G):
        wg = jnp.where(causal, ws_ref[g], 0.0).astype(BF16)
        for ci in range(tt // P):
            lo = ci * P
            s = _dot(wg, vb[lo:lo + P, g * P:(g + 1) * P]) + bias[:, g * P:(g + 1) * P]
            zg = z[lo:lo + P, g * P:(g + 1) * P]
            y_ref[lo:lo + P, g * P:(g + 1) * P] = (u[lo:lo + P, g * P:(g + 1) * P] * s * _silu(zg)).astype(BF16)
        tick()


def _swa_branch(sink_of, q, z, kv_cur, y_ref, kvprev_ref, tick, has_prev, *, tt):
    P, HD, KV = SW_BLOCK, SW_HD, SW_KV_HEADS
    G = SW_HEADS // KV
    kj = lax.broadcasted_iota(jnp.int32, (2 * P, P), 0)
    qi = lax.broadcasted_iota(jnp.int32, (2 * P, P), 1)
    dist = qi + P - kj
    window = jnp.logical_and(dist >= 0, dist < P)
    valid_mid = jnp.concatenate([window] * G, axis=1)
    valid_first = jnp.concatenate([jnp.logical_and(window, jnp.logical_or(kj >= P, has_prev))] * G, axis=1)
    kv = jnp.concatenate([kvprev_ref[...], kv_cur], axis=0)
    kvprev_ref[...] = kv_cur[tt - P:, :]
    qb = (q * (HD ** -0.5)).astype(BF16)
    for kh in range(KV):
        sink = jnp.concatenate([jnp.full((1, P), sink_of(kh * G + g), F32) for g in range(G)], axis=1)
        k_all = kv[:, kh * HD:(kh + 1) * HD].astype(BF16)
        v_all_t = kv[:, (KV + kh) * HD:(KV + kh + 1) * HD].T.astype(BF16)
        for j in range(tt // P):
            lo = j * P
            qs = jnp.concatenate(
                [qb[lo:lo + P, (kh * G + g) * HD:(kh * G + g + 1) * HD] for g in range(G)], axis=0)
            s = jnp.where(valid_first if j == 0 else valid_mid, _dot_nt(k_all[lo:lo + 2 * P], qs), NEG_INF)
            mx = jnp.maximum(jnp.max(s, axis=0, keepdims=True), sink)
            e = jnp.exp(s - mx)
            den = jnp.sum(e, axis=0, keepdims=True) + jnp.exp(sink - mx)
            o_t = _dot(v_all_t[:, lo:lo + 2 * P], e.astype(BF16)) * (1.0 / den)
            for g2 in range(G // 2):
                h0 = kh * G + 2 * g2
                pair = jnp.concatenate(
                    [o_t[:, 2 * g2 * P:(2 * g2 + 1) * P], o_t[:, (2 * g2 + 1) * P:(2 * g2 + 2) * P]], axis=0).T
                zp = z[lo:lo + P, h0 * HD:(h0 + 2) * HD]
                y_ref[lo:lo + P, h0 * HD:(h0 + 2) * HD] = (pair * _silu(zp)).astype(BF16)
            tick()


def _memattn_branch(q, z, kv_ref, y_ref, tick):
    HD = XM_HD
    qb = q.astype(BF16)
    for h in range(XM_HEADS):
        mk = kv_ref[:, h * HD:(h + 1) * HD]
        mv = kv_ref[:, BRANCH_W + h * HD:BRANCH_W + (h + 1) * HD]
        s = _dot_nt(qb[:, h * HD:(h + 1) * HD], mk) * (HD ** -0.5)
        e = jnp.exp(s - jnp.max(s, axis=-1, keepdims=True))
        o = _dot(e.astype(BF16), mv) * (1.0 / jnp.sum(e, axis=-1, keepdims=True))
        y_ref[:, h * HD:(h + 1) * HD] = (o * _silu(z[:, h * HD:(h + 1) * HD])).astype(BF16)
        tick()


def _layer_kernel(sinks_ref, x_ref, npre_ref, w_ref, cw_ref, gp_ref, on_ref, gmn_ref, ws_ref, bias_ref,
                  mkv_ref, wup_ref, wout_ref, npost_ref, o_ref, tail_ref, s_ref, kvprev_ref, y_ref, cols_ref,
                  merged_ref, *, step_rows, tt, layer):
    @pl.when(pl.program_id(1) == 0)
    def _():
        tail_ref[...] = jnp.zeros_like(tail_ref)
        s_ref[...] = jnp.zeros_like(s_ref)
        kvprev_ref[...] = jnp.zeros_like(kvprev_ref)

    for si in range(step_rows // tt):
        rows = pl.ds(si * tt, tt)
        has_prev = pl.program_id(1) > 0 if si == 0 else True
        _layer_tile(sinks_ref, x_ref.at[rows], npre_ref, w_ref, cw_ref, gp_ref, on_ref, gmn_ref, ws_ref, bias_ref,
                    mkv_ref, wup_ref, wout_ref, npost_ref, o_ref.at[rows], tail_ref, s_ref, kvprev_ref, y_ref,
                    cols_ref, merged_ref, has_prev, tt=tt, layer=layer)


def _layer_tile(sinks_ref, x_ref, npre_ref, w_ref, cw_ref, gp_ref, on_ref, gmn_ref, ws_ref, bias_ref,
                mkv_ref, wup_ref, wout_ref, npost_ref, o_ref, tail_ref, s_ref, kvprev_ref, y_ref, cols_ref,
                merged_ref, has_prev, *, tt, layer):
    x = x_ref[...]
    h = _rms(x, npre_ref[...]).astype(BF16)

    def proj(lo, width):
        return _dot(h, w_ref[:, lo:lo + width])

    filled = [OFF_BUV]
    merged_started = set()

    def proj_item(lo):
        def run():
            cols_ref[:, lo - OFF_BUV:lo - OFF_BUV + PROJ_TILE] = proj(lo, PROJ_TILE)
            filled[0] = lo + PROJ_TILE
        return run

    def col(lo, width):
        assert lo + width <= filled[0], "projection read before it was issued"
        return cols_ref[:, lo - OFF_BUV:lo - OFF_BUV + width]

    def merge_item(n, lo):
        def run():
            g = _sigmoid(col(OFF_GATES + n * D_MODEL + lo, PROJ_TILE))
            term = g * _dot(y_ref[n], wup_ref[n, :, lo:lo + PROJ_TILE])
            if lo in merged_started:
                merged_ref[:, lo:lo + PROJ_TILE] += term
            else:
                merged_ref[:, lo:lo + PROJ_TILE] = term
                merged_started.add(lo)
        return run

    queue = collections.deque(proj_item(lo) for lo in range(OFF_BUV, W_COLS, PROJ_TILE))

    def tick(count=1):
        for _ in range(count):
            if queue:
                queue.popleft()()

    def branch_done(n):
        queue.extend(merge_item(n, lo) for lo in range(0, D_MODEL, PROJ_TILE))

    def branch_b():
        _gmlp_branch(col(OFF_BUV, 2 * BRANCH_W), col(OFF_BZ, BRANCH_W), gmn_ref, ws_ref, bias_ref,
                     y_ref.at[1], tick, tt=tt)
        branch_done(1)

    def branch_c():
        _swa_branch(lambda hd: sinks_ref[layer, hd], col(OFF_CQ, BRANCH_W), col(OFF_CZ, BRANCH_W),
                    col(OFF_CKV, CKV_COLS), y_ref.at[2], kvprev_ref, tick, has_prev, tt=tt)
        branch_done(2)

    def branch_m():
        _memattn_branch(col(OFF_MQ, BRANCH_W), col(OFF_MZ, BRANCH_W), mkv_ref, y_ref.at[3], tick)
        branch_done(3)

    bodies = {BODY_AT[0]: branch_b, BODY_AT[1]: branch_c, BODY_AT[2]: branch_m}
    stages = _deltanet_stages(proj(OFF_AQKV, 3 * BRANCH_W), proj(OFF_AZ, BRANCH_W), proj(OFF_SMALL, LANES),
                              cw_ref, gp_ref, on_ref, y_ref.at[0], tail_ref, s_ref, tt=tt)
    for i, _ in enumerate(stages):
        tick(ITEMS_AT.get(i, 1))
        if i in bodies:
            bodies.pop(i)()
    assert not bodies, "stage list shorter than BODY_AT"
    branch_done(0)
    while queue:
        tick()

    out = _dot(merged_ref[...].astype(BF16), wout_ref[...])
    o_ref[...] = x + _rms(out, npost_ref[...])


def _layer_call(x2, layer, sinks, norm_pre, w_pad, conv_w, gate_params, o_norm, gm_norm, spatial_w, bias_mat,
                mem_kv, w_up, w_out, norm_post, *, batch, seq, step_rows, tt):
    nt = seq // step_rows
    ml = mem_kv.shape[2]
    row = lambda b, t, s: (b * nt + t, 0)

    def const(arr):
        shape = arr.shape[1:]
        return pl.BlockSpec((None,) + shape, lambda b, t, s: (layer,) + (0,) * len(shape),
                            pipeline_mode=pl.Buffered(1))

    in_specs = [
        pl.BlockSpec((step_rows, D_MODEL), row),
        const(norm_pre), const(w_pad), const(conv_w), const(gate_params), const(o_norm), const(gm_norm),
        const(spatial_w), const(bias_mat),
        pl.BlockSpec((None, None, ml, 2 * BRANCH_W), lambda b, t, s: (layer, b, 0, 0)),
        const(w_up), const(w_out), const(norm_post),
    ]
    return pl.pallas_call(
        functools.partial(_layer_kernel, step_rows=step_rows, tt=tt, layer=layer),
        grid_spec=pltpu.PrefetchScalarGridSpec(
            num_scalar_prefetch=1,
            grid=(batch, nt),
            in_specs=in_specs,
            out_specs=pl.BlockSpec((step_rows, D_MODEL), row),
            scratch_shapes=[
                pltpu.VMEM((3, 8, BRANCH_W), F32),
                pltpu.VMEM((DN_DK, DN_HEADS * DN_DK), F32),
                pltpu.VMEM((SW_BLOCK, CKV_COLS), F32),
                pltpu.VMEM((N_BRANCH, tt, BRANCH_W), BF16),
                pltpu.VMEM((tt, W_COLS - OFF_BUV), F32),
                pltpu.VMEM((tt, D_MODEL), F32),
            ],
        ),
        out_shape=jax.ShapeDtypeStruct((batch * seq, D_MODEL), F32),
        compiler_params=pltpu.CompilerParams(
            dimension_semantics=("arbitrary", "arbitrary"), vmem_limit_bytes=VMEM_LIMIT),
        name="layer",
    )(sinks, x2, norm_pre, w_pad, conv_w, gate_params, o_norm, gm_norm, spatial_w, bias_mat, mem_kv,
      w_up, w_out, norm_post)


def kernel(x, mem, norm_pre, norm_post, norm_mem, w_in, conv_w, a_log, dt_bias, dn_norm, gm_norm,
           spatial_w, spatial_b, sinks, w_mem_kv, w_up, w_out):
    batch, seq, d = x.shape
    nl = w_in.shape[0]
    w_pad = _pad_w_in(jnp.swapaxes(w_in, 1, 2))
    lane_pad = ((0, 0), (DN_HEADS, LANES - 2 * DN_HEADS))
    gate_params = jnp.stack([jnp.pad(-jnp.exp(a_log), lane_pad), jnp.pad(dt_bias, lane_pad)], axis=1)
    bias_mat = jnp.repeat(jnp.swapaxes(spatial_b, 1, 2), GM_CHUNK, axis=2)
    mem_kv = _memkv(mem, norm_mem[:, None, :], w_mem_kv)
    w_up_b = w_up.astype(BF16)
    w_out_b = w_out.astype(BF16)

    x2 = x.reshape(batch * seq, d)
    for l in range(nl):
        x2 = _layer_call(x2, l, sinks, norm_pre[:, None, :], w_pad, conv_w, gate_params, dn_norm[:, None, :],
                         gm_norm[:, None, :], spatial_w, bias_mat, mem_kv, w_up_b, w_out_b,
                         norm_post[:, None, :], batch=batch, seq=seq, step_rows=STEP_ROWS, tt=TT_LAYER)
    return x2.reshape(batch, seq, d)
```

```python
import collections
import functools

import jax
import jax.numpy as jnp
from jax import lax
from jax.experimental import pallas as pl
from jax.experimental.pallas import tpu as pltpu

F32 = jnp.float32
BF16 = jnp.bfloat16

D_MODEL = 1024
N_BRANCH = 4
BRANCH_W = 512
DN_HEADS = 4
DN_DK = 128
DN_CHUNK = 64
CONV_W = 4
GM_GROUPS = 4
GM_CHUNK = 128
SW_HEADS = 8
SW_KV_HEADS = 2
SW_HD = 64
SW_BLOCK = 128
XM_HEADS = 4
XM_HD = 128
EPS = 1e-6
NEG_INF = -1e30
LANES = 128

CKV_COLS = 2 * SW_KV_HEADS * SW_HD
OFF_AQKV = 0
OFF_AZ = OFF_AQKV + 3 * BRANCH_W
OFF_SMALL = OFF_AZ + BRANCH_W
N_SMALL = 2 * DN_HEADS
OFF_BUV = OFF_SMALL + LANES
OFF_BZ = OFF_BUV + 2 * BRANCH_W
OFF_CQ = OFF_BZ + BRANCH_W
OFF_CKV = OFF_CQ + BRANCH_W
OFF_CZ = OFF_CKV + CKV_COLS
OFF_MQ = OFF_CZ + BRANCH_W
OFF_MZ = OFF_MQ + BRANCH_W
OFF_GATES = OFF_MZ + BRANCH_W
W_COLS = OFF_GATES + N_BRANCH * D_MODEL

VMEM_LIMIT = 60 * 1024 * 1024
TT_LAYER = 256
STEP_ROWS = 256
PROJ_TILE = 256
BODY_AT = (6, 20, 30)
ITEMS_AT = {0: 2, 3: 2}


def _sigmoid(x):
    return 0.5 * jnp.tanh(0.5 * x) + 0.5


def _silu(x):
    return x * _sigmoid(x)


def _softplus(x):
    return jnp.maximum(x, 0.0) + jnp.log(1.0 + jnp.exp(-jnp.abs(x)))


def _rms(x, gain):
    return x * lax.rsqrt(jnp.mean(x * x, axis=-1, keepdims=True) + EPS) * gain


def _dot(a, b):
    return jnp.dot(a, b, preferred_element_type=F32)


def _dot_nt(a, b):
    return lax.dot_general(a, b, (((1,), (1,)), ((), ())), preferred_element_type=F32)


PADW_COLS = 1024
PAD_ROWS = LANES - N_SMALL


def _pad_w_kernel(prev_ref, cur_ref, o_ref):
    i = pl.program_id(1)
    small_blk = OFF_SMALL // PADW_COLS
    keep = PADW_COLS - PAD_ROWS

    @pl.when(i < small_blk)
    def _():
        o_ref[...] = cur_ref[...].T.astype(BF16)

    @pl.when(i == small_blk)
    def _():
        cur = cur_ref[...]
        src = jnp.concatenate([cur[:N_SMALL], jnp.zeros((PAD_ROWS, cur.shape[1]), F32), cur[N_SMALL:keep]], axis=0)
        o_ref[...] = src.T.astype(BF16)

    @pl.when(i > small_blk)
    def _():
        src = jnp.concatenate([prev_ref[LANES - PAD_ROWS:], cur_ref[:keep]], axis=0)
        o_ref[...] = src.T.astype(BF16)


def _pad_w_in(w_t):
    nl, n_in, d = w_t.shape
    assert n_in == W_COLS - PAD_ROWS and OFF_SMALL % PADW_COLS == 0 and OFF_BUV - OFF_SMALL == LANES
    sub = PADW_COLS // LANES
    return pl.pallas_call(
        _pad_w_kernel,
        grid=(nl, pl.cdiv(W_COLS, PADW_COLS)),
        in_specs=[
            pl.BlockSpec((None, LANES, d), lambda l, i: (l, jnp.maximum(i * sub - 1, 0), 0)),
            pl.BlockSpec((None, PADW_COLS, d), lambda l, i: (l, i, 0)),
        ],
        out_specs=pl.BlockSpec((None, d, PADW_COLS), lambda l, i: (l, 0, i)),
        out_shape=jax.ShapeDtypeStruct((nl, d, W_COLS), BF16),
        compiler_params=pltpu.CompilerParams(
            dimension_semantics=("arbitrary", "arbitrary"), vmem_limit_bytes=48 * 1024 * 1024),
        name="padw",
    )(w_t, w_t)


def _memkv_kernel(m_ref, g_ref, w_ref, o_ref):
    wb = w_ref[...].astype(BF16)
    rows = 512
    for lo in range(0, m_ref.shape[0], rows):
        o_ref[lo:lo + rows] = _dot(_rms(m_ref[lo:lo + rows], g_ref[...]).astype(BF16), wb).astype(BF16)


def _memkv(mem, gain, w):
    b, ml, _ = mem.shape
    nl = w.shape[0]
    out = pl.pallas_call(
        _memkv_kernel,
        grid=(nl,),
        in_specs=[
            pl.BlockSpec((b * ml, D_MODEL), lambda l: (0, 0)),
            pl.BlockSpec((None, 1, D_MODEL), lambda l: (l, 0, 0)),
            pl.BlockSpec((None, D_MODEL, 2 * BRANCH_W), lambda l: (l, 0, 0)),
        ],
        out_specs=pl.BlockSpec((None, b * ml, 2 * BRANCH_W), lambda l: (l, 0, 0)),
        out_shape=jax.ShapeDtypeStruct((nl, b * ml, 2 * BRANCH_W), BF16),
        compiler_params=pltpu.CompilerParams(
            dimension_semantics=("arbitrary",), vmem_limit_bytes=48 * 1024 * 1024),
        name="memkv",
    )(mem.reshape(b * ml, D_MODEL), gain, w)
    return out.reshape(nl, b, ml, 2 * BRANCH_W)


def _deltanet_stages(qkv, z, sm, cw_ref, gp_ref, on_ref, y_ref, tail_ref, s_ref, *, tt):
    H, C, DK = DN_HEADS, DN_CHUNK, DN_DK
    HC = H * C
    cw = cw_ref[...]

    def conv_silu(idx):
        lo = idx * BRANCH_W
        cur = qkv[:, lo:lo + BRANCH_W]
        ext = jnp.concatenate([tail_ref[idx], cur], axis=0)
        acc = cur * cw[CONV_W - 1:CONV_W, lo:lo + BRANCH_W]
        for j in range(1, CONV_W):
            shifted = pltpu.roll(ext, j, axis=0)[8:]
            acc = acc + shifted * cw[CONV_W - 1 - j:CONV_W - j, lo:lo + BRANCH_W]
        tail_ref[idx] = cur[tt - 8:]
        return _silu(acc)

    qc = conv_silu(0)
    yield
    kc = conv_silu(1)
    yield
    vc = conv_silu(2)
    yield

    def l2n(xh):
        return xh * lax.rsqrt(jnp.sum(xh * xh, axis=-1, keepdims=True) + EPS)

    ci = lax.broadcasted_iota(jnp.int32, (C, HC), 0)
    cl = lax.broadcasted_iota(jnp.int32, (C, HC), 1)
    cj = cl % C
    strict = ci > cj
    incl = ci >= cj
    eye_cat = (ci == cj).astype(F32)
    lane_head = cl // C
    br = lax.broadcasted_iota(jnp.int32, (HC, HC), 0)
    bc = lax.broadcasted_iota(jnp.int32, (HC, HC), 1)
    bd_mask = jnp.where((br // C) == (bc // C), 1.0, 0.0).astype(BF16)
    kl = lax.broadcasted_iota(jnp.int32, (DK, HC), 1) // C
    head_lane_masks = [jnp.where(kl == h, 1.0, 0.0).astype(BF16) for h in range(H)]
    l64 = lax.broadcasted_iota(jnp.int32, (C, DK), 1) < C
    tr = lax.broadcasted_iota(jnp.int32, (C, C), 0)
    tc = lax.broadcasted_iota(jnp.int32, (C, C), 1)
    ltri = jnp.where(tr >= tc, 1.0, 0.0).astype(BF16)

    neg_a = gp_ref[0:1, :]
    dt_b = gp_ref[1:2, :]
    o_gain = on_ref[...]

    def stack(x, lo):
        return jnp.concatenate([x[lo:lo + C, h * DK:(h + 1) * DK] for h in range(H)], axis=0)

    def block_diag(x_cat):
        return jnp.concatenate([x_cat.astype(BF16)] * H, axis=0) * bd_mask

    def diag_to_cat(m):
        out = m[(H - 1) * C:H * C]
        for h in range(H - 2, -1, -1):
            out = jnp.where(lane_head == h, m[h * C:(h + 1) * C], out)
        return out

    chunks = range(tt // C)
    qs, ks, vs, beta_all, gc3 = [], [], [], [], []
    for c in chunks:
        lo = c * C
        qs.append(jnp.concatenate([l2n(qc[lo:lo + C, h * DK:(h + 1) * DK]) for h in range(H)], axis=0) * (DK ** -0.5))
        ks.append(jnp.concatenate([l2n(kc[lo:lo + C, h * DK:(h + 1) * DK]) for h in range(H)], axis=0))
        vs.append(stack(vc, lo))
        smc = sm[lo:lo + C, :]
        beta_all.append(_sigmoid(smc))
        g_all = neg_a * _softplus(smc + dt_b)
        g_hi = g_all.astype(BF16)
        r1 = g_all - g_hi.astype(F32)
        g_mid = r1.astype(BF16)
        g_lo = (r1 - g_mid.astype(F32)).astype(BF16)
        gc3.append(_dot(ltri, jnp.concatenate([g_hi, g_mid, g_lo], axis=1)))
        yield

    cb, bb, decay, egc, kb, kq = [], [], [], [], [], []
    for c in chunks:
        gc_all = gc3[c][:, :LANES] + gc3[c][:, LANES:2 * LANES] + gc3[c][:, 2 * LANES:]
        gcb = [jnp.broadcast_to(gc_all[:, H + h:H + h + 1], (C, DK)) for h in range(H)]
        cb.append(jnp.concatenate(gcb, axis=0))
        bb.append(jnp.concatenate(
            [jnp.broadcast_to(beta_all[c][:, h:h + 1], (C, DK)) for h in range(H)], axis=0))
        c_cat = jnp.concatenate([jnp.where(l64, gcb[0], gcb[1]), jnp.where(l64, gcb[2], gcb[3])], axis=1)
        gct = jnp.concatenate([gc_all, gc_all], axis=0).T
        r_cat = jnp.concatenate(
            [jnp.where(l64[0:1], gct[H:H + 1], gct[H + 1:H + 2]),
             jnp.where(l64[0:1], gct[H + 2:H + 3], gct[H + 3:H + 4])], axis=1)
        decay.append(jnp.exp(jnp.where(incl, c_cat - r_cat, 0.0)))
        egc.append(jnp.exp(cb[c]))
        kb.append(ks[c] * bb[c])
        kq.append(_dot_nt(jnp.concatenate([kb[c], qs[c]], axis=0).astype(BF16), ks[c].astype(BF16)))
        yield

    a_cat, p_cat, cur = [], [], []
    for c in chunks:
        n_cat = -jnp.where(strict, diag_to_cat(kq[c][:HC]) * decay[c], 0.0)
        a_cat.append(jnp.where(incl, diag_to_cat(kq[c][HC:]) * decay[c], 0.0))
        p_cat.append(eye_cat + n_cat)
        cur.append(_dot(n_cat.astype(BF16), block_diag(n_cat)))
        yield
    for _ in range(4):
        for c in chunks:
            out = _dot(jnp.concatenate([cur[c], p_cat[c]], axis=0).astype(BF16), block_diag(cur[c]))
            p_cat[c] = p_cat[c] + out[C:]
            cur[c] = out[:C]
            yield
    t_cat = [p_cat[c] + _dot(p_cat[c].astype(BF16), block_diag(cur[c])) for c in chunks]
    yield
    solb = []
    for c in chunks:
        rhs = jnp.concatenate([vs[c] * bb[c], kb[c] * egc[c]], axis=1)
        solb.append(_dot(block_diag(t_cat[c]), rhs.astype(BF16)).astype(BF16))
        yield
    auw, qp, gb = [], [], []
    for c in chunks:
        auw.append(_dot(block_diag(a_cat[c]), solb[c]))
        qp.append(qs[c] * egc[c] - auw[c][:, DK:])
        glast = jnp.concatenate(
            [jnp.broadcast_to(cb[c][h * C + C - 1:h * C + C, :], (C, DK)) for h in range(H)], axis=0)
        kd_t = (ks[c] * jnp.exp(glast - cb[c])).T.astype(BF16)
        gb.append([_dot(kd_t * head_lane_masks[h], solb[c]) for h in range(H)])
        yield

    for c in chunks:
        lo = c * C
        for h in range(H):
            s_h = s_ref[:, h * DK:(h + 1) * DK]
            gq = _dot(jnp.concatenate([gb[c][h][:, DK:], qp[c][h * C:(h + 1) * C]], axis=0).astype(BF16),
                      s_h.astype(BF16))
            d_h = jnp.exp(cb[c][h * C + C - 1:h * C + C, :])
            s_ref[:, h * DK:(h + 1) * DK] = s_h * d_h - gq[:DK] + gb[c][h][:, :DK]
            oh = _rms(gq[DK:] + auw[c][h * C:(h + 1) * C, :DK], o_gain)
            zh = z[lo:lo + C, h * DK:(h + 1) * DK]
            y_ref[lo:lo + C, h * DK:(h + 1) * DK] = (oh * _silu(zh)).astype(BF16)
        yield


def _gmlp_branch(uv, z, gain_ref, ws_ref, bias_ref, y_ref, tick, *, tt):
    P, G = GM_CHUNK, GM_GROUPS
    u = jax.nn.gelu(uv[:, :BRANCH_W])
    tick()
    v = _rms(jax.nn.gelu(uv[:, BRANCH_W:]), gain_ref[...])
    tick()
    vb = v.astype(BF16)
    tr = lax.broadcasted_iota(jnp.int32, (P, P), 0)
    tc = lax.broadcasted_iota(jnp.int32, (P, P), 1)
    causal = tr >= tc
    bias = bias_ref[...]
    for g in range(G):
        wg = jnp.where(causal, ws_ref[g], 0.0).astype(BF16)
        for ci in range(tt // P):
            lo = ci * P
            s = _dot(wg, vb[lo:lo + P, g * P:(g + 1) * P]) + bias[:, g * P:(g + 1) * P]
            zg = z[lo:lo + P, g * P:(g + 1) * P]
            y_ref[lo:lo + P, g * P:(g + 1) * P] = (u[lo:lo + P, g * P:(g + 1) * P] * s * _silu(zg)).astype(BF16)
        tick()


def _swa_branch(sink_of, q, z, kv_cur, y_ref, kvprev_ref, tick, has_prev, *, tt):
    P, HD, KV = SW_BLOCK, SW_HD, SW_KV_HEADS
    G = SW_HEADS // KV
    kj = lax.broadcasted_iota(jnp.int32, (2 * P, P), 0)
    qi = lax.broadcasted_iota(jnp.int32, (2 * P, P), 1)
    dist = qi + P - kj
    window = jnp.logical_and(dist >= 0, dist < P)
    valid_mid = jnp.concatenate([window] * G, axis=1)
    valid_first = jnp.concatenate([jnp.logical_and(window, jnp.logical_or(kj >= P, has_prev))] * G, axis=1)
    kv = jnp.concatenate([kvprev_ref[...], kv_cur], axis=0)
    kvprev_ref[...] = kv_cur[tt - P:, :]
    qb = (q * (HD ** -0.5)).astype(BF16)
    for kh in range(KV):
        sink = jnp.concatenate([jnp.full((1, P), sink_of(kh * G + g), F32) for g in range(G)], axis=1)
        k_all = kv[:, kh * HD:(kh + 1) * HD].astype(BF16)
        v_all_t = kv[:, (KV + kh) * HD:(KV + kh + 1) * HD].T.astype(BF16)
        for j in range(tt // P):
            lo = j * P
            qs = jnp.concatenate(
                [qb[lo:lo + P, (kh * G + g) * HD:(kh * G + g + 1) * HD] for g in range(G)], axis=0)
            s = jnp.where(valid_first if j == 0 else valid_mid, _dot_nt(k_all[lo:lo + 2 * P], qs), NEG_INF)
            mx = jnp.maximum(jnp.max(s, axis=0, keepdims=True), sink)
            e = jnp.exp(s - mx)
            den = jnp.sum(e, axis=0, keepdims=True) + jnp.exp(sink - mx)
            o_t = _dot(v_all_t[:, lo:lo + 2 * P], e.astype(BF16)) * (1.0 / den)
            for g2 in range(G // 2):
                h0 = kh * G + 2 * g2
                pair = jnp.concatenate(
                    [o_t[:, 2 * g2 * P:(2 * g2 + 1) * P], o_t[:, (2 * g2 + 1) * P:(2 * g2 + 2) * P]], axis=0).T
                zp = z[lo:lo + P, h0 * HD:(h0 + 2) * HD]
                y_ref[lo:lo + P, h0 * HD:(h0 + 2) * HD] = (pair * _silu(zp)).astype(BF16)
            tick()


def _memattn_branch(q, z, kv_ref, y_ref, tick):
    HD = XM_HD
    qb = q.astype(BF16)
    for h in range(XM_HEADS):
        mk = kv_ref[:, h * HD:(h + 1) * HD]
        mv = kv_ref[:, BRANCH_W + h * HD:BRANCH_W + (h + 1) * HD]
        s = _dot_nt(qb[:, h * HD:(h + 1) * HD], mk) * (HD ** -0.5)
        e = jnp.exp(s - jnp.max(s, axis=-1, keepdims=True))
        o = _dot(e.astype(BF16), mv) * (1.0 / jnp.sum(e, axis=-1, keepdims=True))
        y_ref[:, h * HD:(h + 1) * HD] = (o * _silu(z[:, h * HD:(h + 1) * HD])).astype(BF16)
        tick()


def _layer_kernel(sinks_ref, x_ref, npre_ref, w_ref, cw_ref, gp_ref, on_ref, gmn_ref, ws_ref, bias_ref,
                  mkv_ref, wup_ref, wout_ref, npost_ref, o_ref, tail_ref, s_ref, kvprev_ref, y_ref, cols_ref,
                  merged_ref, *, step_rows, tt, layer):
    @pl.when(pl.program_id(1) == 0)
    def _():
        tail_ref[...] = jnp.zeros_like(tail_ref)
        s_ref[...] = jnp.zeros_like(s_ref)
        kvprev_ref[...] = jnp.zeros_like(kvprev_ref)

    for si in range(step_rows // tt):
        rows = pl.ds(si * tt, tt)
        has_prev = pl.program_id(1) > 0 if si == 0 else True
        _layer_tile(sinks_ref, x_ref.at[rows], npre_ref, w_ref, cw_ref, gp_ref, on_ref, gmn_ref, ws_ref, bias_ref,
                    mkv_ref, wup_ref, wout_ref, npost_ref, o_ref.at[rows], tail_ref, s_ref, kvprev_ref, y_ref,
                    cols_ref, merged_ref, has_prev, tt=tt, layer=layer)


def _layer_tile(sinks_ref, x_ref, npre_ref, w_ref, cw_ref, gp_ref, on_ref, gmn_ref, ws_ref, bias_ref,
                mkv_ref, wup_ref, wout_ref, npost_ref, o_ref, tail_ref, s_ref, kvprev_ref, y_ref, cols_ref,
                merged_ref, has_prev, *, tt, layer):
    x = x_ref[...]
    h = _rms(x, npre_ref[...]).astype(BF16)

    def proj(lo, width):
        return _dot(h, w_ref[:, lo:lo + width])

    filled = [OFF_BUV]
    merged_started = set()

    def proj_item(lo):
        def run():
            cols_ref[:, lo - OFF_BUV:lo - OFF_BUV + PROJ_TILE] = proj(lo, PROJ_TILE)
            filled[0] = lo + PROJ_TILE
        return run

    def col(lo, width):
        assert lo + width <= filled[0], "projection read before it was issued"
        return cols_ref[:, lo - OFF_BUV:lo - OFF_BUV + width]

    def merge_item(n, lo):
        def run():
            g = _sigmoid(col(OFF_GATES + n * D_MODEL + lo, PROJ_TILE))
            term = g * _dot(y_ref[n], wup_ref[n, :, lo:lo + PROJ_TILE])
            if lo in merged_started:
                merged_ref[:, lo:lo + PROJ_TILE] += term
            else:
                merged_ref[:, lo:lo + PROJ_TILE] = term
                merged_started.add(lo)
        return run

    queue = collections.deque(proj_item(lo) for lo in range(OFF_BUV, W_COLS, PROJ_TILE))

    def tick(count=1):
        for _ in range(count):
            if queue:
                queue.popleft()()

    def branch_done(n):
        queue.extend(merge_item(n, lo) for lo in range(0, D_MODEL, PROJ_TILE))

    def branch_b():
        _gmlp_branch(col(OFF_BUV, 2 * BRANCH_W), col(OFF_BZ, BRANCH_W), gmn_ref, ws_ref, bias_ref,
                     y_ref.at[1], tick, tt=tt)
        branch_done(1)

    def branch_c():
        _swa_branch(lambda hd: sinks_ref[layer, hd], col(OFF_CQ, BRANCH_W), col(OFF_CZ, BRANCH_W),
                    col(OFF_CKV, CKV_COLS), y_ref.at[2], kvprev_ref, tick, has_prev, tt=tt)
        branch_done(2)

    def branch_m():
        _memattn_branch(col(OFF_MQ, BRANCH_W), col(OFF_MZ, BRANCH_W), mkv_ref, y_ref.at[3], tick)
        branch_done(3)

    bodies = {BODY_AT[0]: branch_b, BODY_AT[1]: branch_c, BODY_AT[2]: branch_m}
    stages = _deltanet_stages(proj(OFF_AQKV, 3 * BRANCH_W), proj(OFF_AZ, BRANCH_W), proj(OFF_SMALL, LANES),
                              cw_ref, gp_ref, on_ref, y_ref.at[0], tail_ref, s_ref, tt=tt)
    for i, _ in enumerate(stages):
        tick(ITEMS_AT.get(i, 1))
        if i in bodies:
            bodies.pop(i)()
    assert not bodies, "stage list shorter than BODY_AT"
    branch_done(0)
    while queue:
        tick()

    out = _dot(merged_ref[...].astype(BF16), wout_ref[...])
    o_ref[...] = x + _rms(out, npost_ref[...])


def _layer_call(x2, layer, sinks, norm_pre, w_pad, conv_w, gate_params, o_norm, gm_norm, spatial_w, bias_mat,
                mem_kv, w_up, w_out, norm_post, *, batch, seq, step_rows, tt):
    nt = seq // step_rows
    ml = mem_kv.shape[2]
    row = lambda b, t, s: (b * nt + t, 0)

    def const(arr):
        shape = arr.shape[1:]
        return pl.BlockSpec((None,) + shape, lambda b, t, s: (layer,) + (0,) * len(shape),
                            pipeline_mode=pl.Buffered(1))

    in_specs = [
        pl.BlockSpec((step_rows, D_MODEL), row),
        const(norm_pre), const(w_pad), const(conv_w), const(gate_params), const(o_norm), const(gm_norm),
        const(spatial_w), const(bias_mat),
        pl.BlockSpec((None, None, ml, 2 * BRANCH_W), lambda b, t, s: (layer, b, 0, 0)),
        const(w_up), const(w_out), const(norm_post),
    ]
    return pl.pallas_call(
        functools.partial(_layer_kernel, step_rows=step_rows, tt=tt, layer=layer),
        grid_spec=pltpu.PrefetchScalarGridSpec(
            num_scalar_prefetch=1,
            grid=(batch, nt),
            in_specs=in_specs,
            out_specs=pl.BlockSpec((step_rows, D_MODEL), row),
            scratch_shapes=[
                pltpu.VMEM((3, 8, BRANCH_W), F32),
                pltpu.VMEM((DN_DK, DN_HEADS * DN_DK), F32),
                pltpu.VMEM((SW_BLOCK, CKV_COLS), F32),
                pltpu.VMEM((N_BRANCH, tt, BRANCH_W), BF16),
                pltpu.VMEM((tt, W_COLS - OFF_BUV), F32),
                pltpu.VMEM((tt, D_MODEL), F32),
            ],
        ),
        out_shape=jax.ShapeDtypeStruct((batch * seq, D_MODEL), F32),
        compiler_params=pltpu.CompilerParams(
            dimension_semantics=("arbitrary", "arbitrary"), vmem_limit_bytes=VMEM_LIMIT),
        name="layer",
    )(sinks, x2, norm_pre, w_pad, conv_w, gate_params, o_norm, gm_norm, spatial_w, bias_mat, mem_kv,
      w_up, w_out, norm_post)


def kernel(x, mem, norm_pre, norm_post, norm_mem, w_in, conv_w, a_log, dt_bias, dn_norm, gm_norm,
           spatial_w, spatial_b, sinks, w_mem_kv, w_up, w_out):
    batch, seq, d = x.shape
    nl = w_in.shape[0]
    assert d == D_MODEL and seq % STEP_ROWS == 0 and STEP_ROWS % TT_LAYER == 0
    assert TT_LAYER % SW_BLOCK == 0 and TT_LAYER % GM_CHUNK == 0 and TT_LAYER % DN_CHUNK == 0
    assert w_in.shape[1:] == (D_MODEL, W_COLS - PAD_ROWS) and mem.shape[0] == batch and mem.shape[2] == D_MODEL
    w_pad = _pad_w_in(jnp.swapaxes(w_in, 1, 2))
    lane_pad = ((0, 0), (DN_HEADS, LANES - 2 * DN_HEADS))
    gate_params = jnp.stack([jnp.pad(-jnp.exp(a_log), lane_pad), jnp.pad(dt_bias, lane_pad)], axis=1)
    bias_mat = jnp.repeat(jnp.swapaxes(spatial_b, 1, 2), GM_CHUNK, axis=2)
    mem_kv = _memkv(mem, norm_mem[:, None, :], w_mem_kv)
    w_up_b = w_up.astype(BF16)
    w_out_b = w_out.astype(BF16)

    x2 = x.reshape(batch * seq, d)
    for l in range(nl):
        x2 = _layer_call(x2, l, sinks, norm_pre[:, None, :], w_pad, conv_w, gate_params, dn_norm[:, None, :],
                         gm_norm[:, None, :], spatial_w, bias_mat, mem_kv, w_up_b, w_out_b,
                         norm_post[:, None, :], batch=batch, seq=seq, step_rows=STEP_ROWS, tt=TT_LAYER)
    return x2.reshape(batch, seq, d)
```

```python
import collections
import functools

import jax
import jax.numpy as jnp
from jax import lax
from jax.experimental import pallas as pl
from jax.experimental.pallas import tpu as pltpu

F32 = jnp.float32
BF16 = jnp.bfloat16

D_MODEL = 1024
N_BRANCH = 4
BRANCH_W = 512
DN_HEADS = 4
DN_DK = 128
DN_CHUNK = 64
CONV_W = 4
GM_GROUPS = 4
GM_CHUNK = 128
SW_HEADS = 8
SW_KV_HEADS = 2
SW_HD = 64
SW_BLOCK = 128
XM_HEADS = 4
XM_HD = 128
EPS = 1e-6
NEG_INF = -1e30
LANES = 128

CKV_COLS = 2 * SW_KV_HEADS * SW_HD
OFF_AQKV = 0
OFF_AZ = OFF_AQKV + 3 * BRANCH_W
OFF_SMALL = OFF_AZ + BRANCH_W
N_SMALL = 2 * DN_HEADS
OFF_BUV = OFF_SMALL + LANES
OFF_BZ = OFF_BUV + 2 * BRANCH_W
OFF_CQ = OFF_BZ + BRANCH_W
OFF_CKV = OFF_CQ + BRANCH_W
OFF_CZ = OFF_CKV + CKV_COLS
OFF_MQ = OFF_CZ + BRANCH_W
OFF_MZ = OFF_MQ + BRANCH_W
OFF_GATES = OFF_MZ + BRANCH_W
W_COLS = OFF_GATES + N_BRANCH * D_MODEL

VMEM_LIMIT = 60 * 1024 * 1024
TT_LAYER = 256
STEP_ROWS = 256
PROJ_TILE = 256
BODY_AT = (6, 20, 30)
ITEMS_AT = {0: 2, 3: 2}


def _sigmoid(x):
    return 0.5 * jnp.tanh(0.5 * x) + 0.5


def _silu(x):
    u = 0.5 * x
    return u + u * jnp.tanh(u)


def _softplus(x):
    return jnp.maximum(x, 0.0) + jnp.log(1.0 + jnp.exp(-jnp.abs(x)))


def _rms(x, gain):
    return x * lax.rsqrt(jnp.mean(x * x, axis=-1, keepdims=True) + EPS) * gain


def _dot(a, b):
    return jnp.dot(a, b, preferred_element_type=F32)


def _dot_nt(a, b):
    return lax.dot_general(a, b, (((1,), (1,)), ((), ())), preferred_element_type=F32)


PADW_COLS = 1024
PAD_ROWS = LANES - N_SMALL


def _pad_w_kernel(prev_ref, cur_ref, o_ref):
    i = pl.program_id(1)
    small_blk = OFF_SMALL // PADW_COLS
    keep = PADW_COLS - PAD_ROWS

    @pl.when(i < small_blk)
    def _():
        o_ref[...] = cur_ref[...].T.astype(BF16)

    @pl.when(i == small_blk)
    def _():
        cur = cur_ref[...]
        src = jnp.concatenate([cur[:N_SMALL], jnp.zeros((PAD_ROWS, cur.shape[1]), F32), cur[N_SMALL:keep]], axis=0)
        o_ref[...] = src.T.astype(BF16)

    @pl.when(i > small_blk)
    def _():
        src = jnp.concatenate([prev_ref[LANES - PAD_ROWS:], cur_ref[:keep]], axis=0)
        o_ref[...] = src.T.astype(BF16)


def _pad_w_in(w_t):
    nl, n_in, d = w_t.shape
    assert n_in == W_COLS - PAD_ROWS and OFF_SMALL % PADW_COLS == 0 and OFF_BUV - OFF_SMALL == LANES
    sub = PADW_COLS // LANES
    return pl.pallas_call(
        _pad_w_kernel,
        grid=(nl, pl.cdiv(W_COLS, PADW_COLS)),
        in_specs=[
            pl.BlockSpec((None, LANES, d), lambda l, i: (l, jnp.maximum(i * sub - 1, 0), 0)),
            pl.BlockSpec((None, PADW_COLS, d), lambda l, i: (l, i, 0)),
        ],
        out_specs=pl.BlockSpec((None, d, PADW_COLS), lambda l, i: (l, 0, i)),
        out_shape=jax.ShapeDtypeStruct((nl, d, W_COLS), BF16),
        compiler_params=pltpu.CompilerParams(
            dimension_semantics=("arbitrary", "arbitrary"), vmem_limit_bytes=48 * 1024 * 1024),
        name="padw",
    )(w_t, w_t)


def _memkv_kernel(m_ref, g_ref, w_ref, o_ref):
    wb = w_ref[...].astype(BF16)
    rows = 512
    for lo in range(0, m_ref.shape[0], rows):
        o_ref[lo:lo + rows] = _dot(_rms(m_ref[lo:lo + rows], g_ref[...]).astype(BF16), wb).astype(BF16)


def _memkv(mem, gain, w):
    b, ml, _ = mem.shape
    nl = w.shape[0]
    out = pl.pallas_call(
        _memkv_kernel,
        grid=(nl,),
        in_specs=[
            pl.BlockSpec((b * ml, D_MODEL), lambda l: (0, 0)),
            pl.BlockSpec((None, 1, D_MODEL), lambda l: (l, 0, 0)),
            pl.BlockSpec((None, D_MODEL, 2 * BRANCH_W), lambda l: (l, 0, 0)),
        ],
        out_specs=pl.BlockSpec((None, b * ml, 2 * BRANCH_W), lambda l: (l, 0, 0)),
        out_shape=jax.ShapeDtypeStruct((nl, b * ml, 2 * BRANCH_W), BF16),
        compiler_params=pltpu.CompilerParams(
            dimension_semantics=("arbitrary",), vmem_limit_bytes=48 * 1024 * 1024),
        name="memkv",
    )(mem.reshape(b * ml, D_MODEL), gain, w)
    return out.reshape(nl, b, ml, 2 * BRANCH_W)


def _deltanet_stages(qkv, z, sm, cw_ref, gp_ref, on_ref, y_ref, tail_ref, s_ref, *, tt):
    H, C, DK = DN_HEADS, DN_CHUNK, DN_DK
    HC = H * C
    cw = cw_ref[...]

    def conv_silu(idx):
        lo = idx * BRANCH_W
        cur = qkv[:, lo:lo + BRANCH_W]
        ext = jnp.concatenate([tail_ref[idx], cur], axis=0)
        acc = cur * cw[CONV_W - 1:CONV_W, lo:lo + BRANCH_W]
        for j in range(1, CONV_W):
            shifted = pltpu.roll(ext, j, axis=0)[8:]
            acc = acc + shifted * cw[CONV_W - 1 - j:CONV_W - j, lo:lo + BRANCH_W]
        tail_ref[idx] = cur[tt - 8:]
        return _silu(acc)

    qc = conv_silu(0)
    yield
    kc = conv_silu(1)
    yield
    vc = conv_silu(2)
    yield

    def l2n(xh):
        return xh * lax.rsqrt(jnp.sum(xh * xh, axis=-1, keepdims=True) + EPS)

    ci = lax.broadcasted_iota(jnp.int32, (C, HC), 0)
    cl = lax.broadcasted_iota(jnp.int32, (C, HC), 1)
    cj = cl % C
    strict = ci > cj
    incl = ci >= cj
    eye_cat = (ci == cj).astype(F32)
    br = lax.broadcasted_iota(jnp.int32, (HC, HC), 0)
    bc = lax.broadcasted_iota(jnp.int32, (HC, HC), 1)
    bd_mask = jnp.where((br // C) == (bc // C), 1.0, 0.0).astype(BF16)
    kl = lax.broadcasted_iota(jnp.int32, (DK, HC), 1) // C
    head_lane_masks = [jnp.where(kl == h, 1.0, 0.0).astype(BF16) for h in range(H)]
    kr = lax.broadcasted_iota(jnp.int32, (HC, H * DK), 0) // C
    kc_ = lax.broadcasted_iota(jnp.int32, (HC, H * DK), 1) // DK
    kbd_mask = jnp.where(kr == kc_, 1.0, 0.0).astype(BF16)
    l64 = lax.broadcasted_iota(jnp.int32, (C, DK), 1) < C
    tr = lax.broadcasted_iota(jnp.int32, (C, C), 0)
    tc = lax.broadcasted_iota(jnp.int32, (C, C), 1)
    ltri = jnp.where(tr >= tc, 1.0, 0.0).astype(BF16)

    neg_a = gp_ref[0:1, :]
    dt_b = gp_ref[1:2, :]
    o_gain = on_ref[...]

    def stack(x, lo):
        return jnp.concatenate([x[lo:lo + C, h * DK:(h + 1) * DK] for h in range(H)], axis=0)

    def block_diag(x_cat):
        return jnp.concatenate([x_cat.astype(BF16)] * H, axis=0) * bd_mask

    def unstack(x_st):
        return jnp.concatenate([x_st[h * C:(h + 1) * C] for h in range(H)], axis=1)

    chunks = range(tt // C)
    qs, ks, vs, beta_all, gc3 = [], [], [], [], []
    for c in chunks:
        lo = c * C
        qs.append(jnp.concatenate([l2n(qc[lo:lo + C, h * DK:(h + 1) * DK]) for h in range(H)], axis=0) * (DK ** -0.5))
        ks.append(jnp.concatenate([l2n(kc[lo:lo + C, h * DK:(h + 1) * DK]) for h in range(H)], axis=0))
        vs.append(stack(vc, lo))
        smc = sm[lo:lo + C, :]
        beta_all.append(_sigmoid(smc))
        g_all = neg_a * _softplus(smc + dt_b)
        g_hi = g_all.astype(BF16)
        r1 = g_all - g_hi.astype(F32)
        g_mid = r1.astype(BF16)
        g_lo = (r1 - g_mid.astype(F32)).astype(BF16)
        gc3.append(_dot(ltri, jnp.concatenate([g_hi, g_mid, g_lo], axis=1)))
        yield

    cb, bb, decay, egc, kb, kq = [], [], [], [], [], []
    for c in chunks:
        gc_all = gc3[c][:, :LANES] + gc3[c][:, LANES:2 * LANES] + gc3[c][:, 2 * LANES:]
        gcb = [jnp.broadcast_to(gc_all[:, H + h:H + h + 1], (C, DK)) for h in range(H)]
        cb.append(jnp.concatenate(gcb, axis=0))
        bb.append(jnp.concatenate(
            [jnp.broadcast_to(beta_all[c][:, h:h + 1], (C, DK)) for h in range(H)], axis=0))
        c_cat = jnp.concatenate([jnp.where(l64, gcb[0], gcb[1]), jnp.where(l64, gcb[2], gcb[3])], axis=1)
        gct = jnp.concatenate([gc_all, gc_all], axis=0).T
        r_cat = jnp.concatenate(
            [jnp.where(l64[0:1], gct[H:H + 1], gct[H + 1:H + 2]),
             jnp.where(l64[0:1], gct[H + 2:H + 3], gct[H + 3:H + 4])], axis=1)
        decay.append(jnp.exp(jnp.where(incl, c_cat - r_cat, 0.0)))
        egc.append(jnp.exp(cb[c]))
        kb.append(ks[c] * bb[c])
        k_bd = jnp.concatenate([ks[c].astype(BF16)] * H, axis=1) * kbd_mask
        kq.append(_dot_nt(jnp.concatenate([unstack(kb[c]), unstack(qs[c])], axis=0).astype(BF16), k_bd))
        yield

    a_cat, p_cat, cur = [], [], []
    for c in chunks:
        n_cat = -jnp.where(strict, kq[c][:C] * decay[c], 0.0)
        a_cat.append(jnp.where(incl, kq[c][C:] * decay[c], 0.0))
        p_cat.append(eye_cat + n_cat)
        cur.append(_dot(n_cat.astype(BF16), block_diag(n_cat)))
        yield
    for _ in range(4):
        for c in chunks:
            out = _dot(jnp.concatenate([cur[c], p_cat[c]], axis=0).astype(BF16), block_diag(cur[c]))
            p_cat[c] = p_cat[c] + out[C:]
            cur[c] = out[:C]
            yield
    t_cat = [p_cat[c] + _dot(p_cat[c].astype(BF16), block_diag(cur[c])) for c in chunks]
    yield
    solb = []
    for c in chunks:
        rhs = jnp.concatenate([vs[c] * bb[c], kb[c] * egc[c]], axis=1)
        solb.append(_dot(block_diag(t_cat[c]), rhs.astype(BF16)).astype(BF16))
        yield
    auw, qp, gb = [], [], []
    for c in chunks:
        auw.append(_dot(block_diag(a_cat[c]), solb[c]))
        qp.append(qs[c] * egc[c] - auw[c][:, DK:])
        glast = jnp.concatenate(
            [jnp.broadcast_to(cb[c][h * C + C - 1:h * C + C, :], (C, DK)) for h in range(H)], axis=0)
        kd_t = (ks[c] * jnp.exp(glast - cb[c])).T.astype(BF16)
        gb.append([_dot(kd_t * head_lane_masks[h], solb[c]) for h in range(H)])
        yield

    for c in chunks:
        lo = c * C
        for h in range(H):
            s_h = s_ref[:, h * DK:(h + 1) * DK]
            gq = _dot(jnp.concatenate([gb[c][h][:, DK:], qp[c][h * C:(h + 1) * C]], axis=0).astype(BF16),
                      s_h.astype(BF16))
            d_h = jnp.exp(cb[c][h * C + C - 1:h * C + C, :])
            s_ref[:, h * DK:(h + 1) * DK] = s_h * d_h - gq[:DK] + gb[c][h][:, :DK]
            oh = _rms(gq[DK:] + auw[c][h * C:(h + 1) * C, :DK], o_gain)
            zh = z[lo:lo + C, h * DK:(h + 1) * DK]
            y_ref[lo:lo + C, h * DK:(h + 1) * DK] = (oh * _silu(zh)).astype(BF16)
        yield


def _gmlp_branch(uv, z, gain_ref, ws_ref, bias_ref, y_ref, tick, *, tt):
    P, G = GM_CHUNK, GM_GROUPS
    u = jax.nn.gelu(uv[:, :BRANCH_W])
    tick()
    v = _rms(jax.nn.gelu(uv[:, BRANCH_W:]), gain_ref[...])
    tick()
    vb = v.astype(BF16)
    tr = lax.broadcasted_iota(jnp.int32, (P, P), 0)
    tc = lax.broadcasted_iota(jnp.int32, (P, P), 1)
    causal = tr >= tc
    bias = bias_ref[...]
    for g in range(G):
        wg = jnp.where(causal, ws_ref[g], 0.0).astype(BF16)
        for ci in range(tt // P):
            lo = ci * P
            s = _dot(wg, vb[lo:lo + P, g * P:(g + 1) * P]) + bias[:, g * P:(g + 1) * P]
            zg = z[lo:lo + P, g * P:(g + 1) * P]
            y_ref[lo:lo + P, g * P:(g + 1) * P] = (u[lo:lo + P, g * P:(g + 1) * P] * s * _silu(zg)).astype(BF16)
        tick()


def _swa_branch(sink_of, q, z, kv_cur, y_ref, kvprev_ref, tick, has_prev, *, tt):
    P, HD, KV = SW_BLOCK, SW_HD, SW_KV_HEADS
    G = SW_HEADS // KV
    kj = lax.broadcasted_iota(jnp.int32, (2 * P, P), 0)
    qi = lax.broadcasted_iota(jnp.int32, (2 * P, P), 1)
    dist = qi + P - kj
    window = jnp.logical_and(dist >= 0, dist < P)
    valid_mid = jnp.concatenate([window] * G, axis=1)
    valid_first = jnp.concatenate([jnp.logical_and(window, jnp.logical_or(kj >= P, has_prev))] * G, axis=1)
    kv = jnp.concatenate([kvprev_ref[...], kv_cur], axis=0)
    kvprev_ref[...] = kv_cur[tt - P:, :]
    qb = (q * (HD ** -0.5)).astype(BF16)
    for kh in range(KV):
        sink = jnp.concatenate([jnp.full((1, P), sink_of(kh * G + g), F32) for g in range(G)], axis=1)
        k_all = kv[:, kh * HD:(kh + 1) * HD].astype(BF16)
        v_all_t = kv[:, (KV + kh) * HD:(KV + kh + 1) * HD].T.astype(BF16)
        for j in range(tt // P):
            lo = j * P
            qs = jnp.concatenate(
                [qb[lo:lo + P, (kh * G + g) * HD:(kh * G + g + 1) * HD] for g in range(G)], axis=0)
            s = jnp.where(valid_first if j == 0 else valid_mid, _dot_nt(k_all[lo:lo + 2 * P], qs), NEG_INF)
            mx = jnp.maximum(jnp.max(s, axis=0, keepdims=True), sink)
            e = jnp.exp(s - mx)
            den = jnp.sum(e, axis=0, keepdims=True) + jnp.exp(sink - mx)
            o_t = _dot(v_all_t[:, lo:lo + 2 * P], e.astype(BF16)) * (1.0 / den)
            for g2 in range(G // 2):
                h0 = kh * G + 2 * g2
                pair = jnp.concatenate(
                    [o_t[:, 2 * g2 * P:(2 * g2 + 1) * P], o_t[:, (2 * g2 + 1) * P:(2 * g2 + 2) * P]], axis=0).T
                zp = z[lo:lo + P, h0 * HD:(h0 + 2) * HD]
                y_ref[lo:lo + P, h0 * HD:(h0 + 2) * HD] = (pair * _silu(zp)).astype(BF16)
            tick()


def _memattn_branch(q, z, kv_ref, y_ref, tick):
    HD = XM_HD
    qb = q.astype(BF16)
    for h in range(XM_HEADS):
        mk = kv_ref[:, h * HD:(h + 1) * HD]
        mv = kv_ref[:, BRANCH_W + h * HD:BRANCH_W + (h + 1) * HD]
        s = _dot_nt(qb[:, h * HD:(h + 1) * HD], mk) * (HD ** -0.5)
        e = jnp.exp(s - jnp.max(s, axis=-1, keepdims=True))
        o = _dot(e.astype(BF16), mv) * (1.0 / jnp.sum(e, axis=-1, keepdims=True))
        y_ref[:, h * HD:(h + 1) * HD] = (o * _silu(z[:, h * HD:(h + 1) * HD])).astype(BF16)
        tick()


def _layer_kernel(sinks_ref, x_ref, npre_ref, w_ref, cw_ref, gp_ref, on_ref, gmn_ref, ws_ref, bias_ref,
                  mkv_ref, wup_ref, wout_ref, npost_ref, o_ref, tail_ref, s_ref, kvprev_ref, y_ref, cols_ref,
                  merged_ref, *, step_rows, tt, layer):
    @pl.when(pl.program_id(1) == 0)
    def _():
        tail_ref[...] = jnp.zeros_like(tail_ref)
        s_ref[...] = jnp.zeros_like(s_ref)
        kvprev_ref[...] = jnp.zeros_like(kvprev_ref)

    for si in range(step_rows // tt):
        rows = pl.ds(si * tt, tt)
        has_prev = pl.program_id(1) > 0 if si == 0 else True
        _layer_tile(sinks_ref, x_ref.at[rows], npre_ref, w_ref, cw_ref, gp_ref, on_ref, gmn_ref, ws_ref, bias_ref,
                    mkv_ref, wup_ref, wout_ref, npost_ref, o_ref.at[rows], tail_ref, s_ref, kvprev_ref, y_ref,
                    cols_ref, merged_ref, has_prev, tt=tt, layer=layer)


def _layer_tile(sinks_ref, x_ref, npre_ref, w_ref, cw_ref, gp_ref, on_ref, gmn_ref, ws_ref, bias_ref,
                mkv_ref, wup_ref, wout_ref, npost_ref, o_ref, tail_ref, s_ref, kvprev_ref, y_ref, cols_ref,
                merged_ref, has_prev, *, tt, layer):
    x = x_ref[...]
    h = _rms(x, npre_ref[...]).astype(BF16)

    def proj(lo, width):
        return _dot(h, w_ref[:, lo:lo + width])

    filled = [OFF_BUV]
    merged_started = set()

    def proj_item(lo):
        def run():
            cols_ref[:, lo - OFF_BUV:lo - OFF_BUV + PROJ_TILE] = proj(lo, PROJ_TILE)
            filled[0] = lo + PROJ_TILE
        return run

    def col(lo, width):
        assert lo + width <= filled[0], "projection read before it was issued"
        return cols_ref[:, lo - OFF_BUV:lo - OFF_BUV + width]

    def merge_item(n, lo):
        def run():
            g = _sigmoid(col(OFF_GATES + n * D_MODEL + lo, PROJ_TILE))
            term = g * _dot(y_ref[n], wup_ref[n, :, lo:lo + PROJ_TILE])
            if lo in merged_started:
                merged_ref[:, lo:lo + PROJ_TILE] += term
            else:
                merged_ref[:, lo:lo + PROJ_TILE] = term
                merged_started.add(lo)
        return run

    queue = collections.deque(proj_item(lo) for lo in range(OFF_BUV, W_COLS, PROJ_TILE))

    def tick(count=1):
        for _ in range(count):
            if queue:
                queue.popleft()()

    def branch_done(n):
        queue.extend(merge_item(n, lo) for lo in range(0, D_MODEL, PROJ_TILE))

    def branch_b():
        _gmlp_branch(col(OFF_BUV, 2 * BRANCH_W), col(OFF_BZ, BRANCH_W), gmn_ref, ws_ref, bias_ref,
                     y_ref.at[1], tick, tt=tt)
        branch_done(1)

    def branch_c():
        _swa_branch(lambda hd: sinks_ref[layer, hd], col(OFF_CQ, BRANCH_W), col(OFF_CZ, BRANCH_W),
                    col(OFF_CKV, CKV_COLS), y_ref.at[2], kvprev_ref, tick, has_prev, tt=tt)
        branch_done(2)

    def branch_m():
        _memattn_branch(col(OFF_MQ, BRANCH_W), col(OFF_MZ, BRANCH_W), mkv_ref, y_ref.at[3], tick)
        branch_done(3)

    bodies = {BODY_AT[0]: branch_b, BODY_AT[1]: branch_c, BODY_AT[2]: branch_m}
    stages = _deltanet_stages(proj(OFF_AQKV, 3 * BRANCH_W), proj(OFF_AZ, BRANCH_W), proj(OFF_SMALL, LANES),
                              cw_ref, gp_ref, on_ref, y_ref.at[0], tail_ref, s_ref, tt=tt)
    for i, _ in enumerate(stages):
        tick(ITEMS_AT.get(i, 1))
        if i in bodies:
            bodies.pop(i)()
    assert not bodies, "stage list shorter than BODY_AT"
    branch_done(0)
    while queue:
        tick()

    out = _dot(merged_ref[...].astype(BF16), wout_ref[...])
    o_ref[...] = x + _rms(out, npost_ref[...])


def _layer_call(x2, layer, sinks, norm_pre, w_pad, conv_w, gate_params, o_norm, gm_norm, spatial_w, bias_mat,
                mem_kv, w_up, w_out, norm_post, *, batch, seq, step_rows, tt):
    nt = seq // step_rows
    ml = mem_kv.shape[2]
    row = lambda b, t, s: (b * nt + t, 0)

    def const(arr):
        shape = arr.shape[1:]
        return pl.BlockSpec((None,) + shape, lambda b, t, s: (layer,) + (0,) * len(shape),
                            pipeline_mode=pl.Buffered(1))

    in_specs = [
        pl.BlockSpec((step_rows, D_MODEL), row),
        const(norm_pre), const(w_pad), const(conv_w), const(gate_params), const(o_norm), const(gm_norm),
        const(spatial_w), const(bias_mat),
        pl.BlockSpec((None, None, ml, 2 * BRANCH_W), lambda b, t, s: (layer, b, 0, 0)),
        const(w_up), const(w_out), const(norm_post),
    ]
    return pl.pallas_call(
        functools.partial(_layer_kernel, step_rows=step_rows, tt=tt, layer=layer),
        grid_spec=pltpu.PrefetchScalarGridSpec(
            num_scalar_prefetch=1,
            grid=(batch, nt),
            in_specs=in_specs,
            out_specs=pl.BlockSpec((step_rows, D_MODEL), row),
            scratch_shapes=[
                pltpu.VMEM((3, 8, BRANCH_W), F32),
                pltpu.VMEM((DN_DK, DN_HEADS * DN_DK), F32),
                pltpu.VMEM((SW_BLOCK, CKV_COLS), F32),
                pltpu.VMEM((N_BRANCH, tt, BRANCH_W), BF16),
                pltpu.VMEM((tt, W_COLS - OFF_BUV), F32),
                pltpu.VMEM((tt, D_MODEL), F32),
            ],
        ),
        out_shape=jax.ShapeDtypeStruct((batch * seq, D_MODEL), F32),
        compiler_params=pltpu.CompilerParams(
            dimension_semantics=("arbitrary", "arbitrary"), vmem_limit_bytes=VMEM_LIMIT),
        name="layer",
    )(sinks, x2, norm_pre, w_pad, conv_w, gate_params, o_norm, gm_norm, spatial_w, bias_mat, mem_kv,
      w_up, w_out, norm_post)


def kernel(x, mem, norm_pre, norm_post, norm_mem, w_in, conv_w, a_log, dt_bias, dn_norm, gm_norm,
           spatial_w, spatial_b, sinks, w_mem_kv, w_up, w_out):
    batch, seq, d = x.shape
    nl = w_in.shape[0]
    assert d == D_MODEL and seq % STEP_ROWS == 0 and STEP_ROWS % TT_LAYER == 0
    assert TT_LAYER % SW_BLOCK == 0 and TT_LAYER % GM_CHUNK == 0 and TT_LAYER % DN_CHUNK == 0
    assert w_in.shape[1:] == (D_MODEL, W_COLS - PAD_ROWS) and mem.shape[0] == batch and mem.shape[2] == D_MODEL
    w_pad = _pad_w_in(jnp.swapaxes(w_in, 1, 2))
    lane_pad = ((0, 0), (DN_HEADS, LANES - 2 * DN_HEADS))
    gate_params = jnp.stack([jnp.pad(-jnp.exp(a_log), lane_pad), jnp.pad(dt_bias, lane_pad)], axis=1)
    bias_mat = jnp.repeat(jnp.swapaxes(spatial_b, 1, 2), GM_CHUNK, axis=2)
    mem_kv = _memkv(mem, norm_mem[:, None, :], w_mem_kv)
    w_up_b = w_up.astype(BF16)
    w_out_b = w_out.astype(BF16)

    x2 = x.reshape(batch * seq, d)
    for l in range(nl):
        x2 = _layer_call(x2, l, sinks, norm_pre[:, None, :], w_pad, conv_w, gate_params, dn_norm[:, None, :],
                         gm_norm[:, None, :], spatial_w, bias_mat, mem_kv, w_up_b, w_out_b,
                         norm_post[:, None, :], batch=batch, seq=seq, step_rows=STEP_ROWS, tt=TT_LAYER)
    return x2.reshape(batch, seq, d)
```

```python
import collections
import functools

import jax
import jax.numpy as jnp
from jax import lax
from jax.experimental import pallas as pl
from jax.experimental.pallas import tpu as pltpu

F32 = jnp.float32
BF16 = jnp.bfloat16

D_MODEL = 1024
N_BRANCH = 4
BRANCH_W = 512
DN_HEADS = 4
DN_DK = 128
DN_CHUNK = 64
CONV_W = 4
GM_GROUPS = 4
GM_CHUNK = 128
SW_HEADS = 8
SW_KV_HEADS = 2
SW_HD = 64
SW_BLOCK = 128
XM_HEADS = 4
XM_HD = 128
EPS = 1e-6
NEG_INF = -1e30
LANES = 128
SMALL_ROWS = 32

CKV_COLS = 2 * SW_KV_HEADS * SW_HD
OFF_AQKV = 0
OFF_AZ = OFF_AQKV + 3 * BRANCH_W
OFF_SMALL = OFF_AZ + BRANCH_W
N_SMALL = 2 * DN_HEADS
OFF_BUV = OFF_SMALL + LANES
OFF_BZ = OFF_BUV + 2 * BRANCH_W
OFF_CQ = OFF_BZ + BRANCH_W
OFF_CKV = OFF_CQ + BRANCH_W
OFF_CZ = OFF_CKV + CKV_COLS
OFF_MQ = OFF_CZ + BRANCH_W
OFF_MZ = OFF_MQ + BRANCH_W
OFF_GATES = OFF_MZ + BRANCH_W
W_COLS = OFF_GATES + N_BRANCH * D_MODEL

VMEM_LIMIT = 60 * 1024 * 1024
TT_LAYER = 256
STEP_ROWS = 256
PROJ_TILE = 256
BODY_AT = (6, 20, 30)
ITEMS_AT = {0: 2, 3: 2}


def _sigmoid(x):
    return 0.5 * jnp.tanh(0.5 * x) + 0.5


def _silu(x):
    u = 0.5 * x
    return u + u * jnp.tanh(u)


def _softplus(x):
    return jnp.maximum(x, 0.0) + jnp.log(1.0 + jnp.exp(-jnp.abs(x)))


def _rms(x, gain):
    return x * lax.rsqrt(jnp.mean(x * x, axis=-1, keepdims=True) + EPS) * gain


def _dot(a, b):
    return jnp.dot(a, b, preferred_element_type=F32)


def _dot_nt(a, b):
    return lax.dot_general(a, b, (((1,), (1,)), ((), ())), preferred_element_type=F32)


PADW_COLS = 1024
PAD_ROWS = LANES - N_SMALL


def _pad_w_kernel(prev_ref, cur_ref, o_ref):
    i = pl.program_id(1)
    small_blk = OFF_SMALL // PADW_COLS
    keep = PADW_COLS - PAD_ROWS

    @pl.when(i < small_blk)
    def _():
        o_ref[...] = cur_ref[...].T.astype(BF16)

    @pl.when(i == small_blk)
    def _():
        cur = cur_ref[...]
        src = jnp.concatenate([cur[:N_SMALL], jnp.zeros((PAD_ROWS, cur.shape[1]), F32), cur[N_SMALL:keep]], axis=0)
        o_ref[...] = src.T.astype(BF16)

    @pl.when(i > small_blk)
    def _():
        src = jnp.concatenate([prev_ref[LANES - PAD_ROWS:], cur_ref[:keep]], axis=0)
        o_ref[...] = src.T.astype(BF16)


def _pad_w_in(w_t):
    nl, n_in, d = w_t.shape
    assert n_in == W_COLS - PAD_ROWS and OFF_SMALL % PADW_COLS == 0 and OFF_BUV - OFF_SMALL == LANES
    sub = PADW_COLS // LANES
    return pl.pallas_call(
        _pad_w_kernel,
        grid=(nl, pl.cdiv(W_COLS, PADW_COLS)),
        in_specs=[
            pl.BlockSpec((None, LANES, d), lambda l, i: (l, jnp.maximum(i * sub - 1, 0), 0)),
            pl.BlockSpec((None, PADW_COLS, d), lambda l, i: (l, i, 0)),
        ],
        out_specs=pl.BlockSpec((None, d, PADW_COLS), lambda l, i: (l, 0, i)),
        out_shape=jax.ShapeDtypeStruct((nl, d, W_COLS), BF16),
        compiler_params=pltpu.CompilerParams(
            dimension_semantics=("arbitrary", "arbitrary"), vmem_limit_bytes=48 * 1024 * 1024),
        name="padw",
    )(w_t, w_t)


def _memkv_kernel(m_ref, g_ref, w_ref, o_ref):
    wb = w_ref[...].astype(BF16)
    rows = 512
    for lo in range(0, m_ref.shape[0], rows):
        o_ref[lo:lo + rows] = _dot(_rms(m_ref[lo:lo + rows], g_ref[...]).astype(BF16), wb).astype(BF16)


def _memkv(mem, gain, w):
    b, ml, _ = mem.shape
    nl = w.shape[0]
    out = pl.pallas_call(
        _memkv_kernel,
        grid=(nl,),
        in_specs=[
            pl.BlockSpec((b * ml, D_MODEL), lambda l: (0, 0)),
            pl.BlockSpec((None, 1, D_MODEL), lambda l: (l, 0, 0)),
            pl.BlockSpec((None, D_MODEL, 2 * BRANCH_W), lambda l: (l, 0, 0)),
        ],
        out_specs=pl.BlockSpec((None, b * ml, 2 * BRANCH_W), lambda l: (l, 0, 0)),
        out_shape=jax.ShapeDtypeStruct((nl, b * ml, 2 * BRANCH_W), BF16),
        compiler_params=pltpu.CompilerParams(
            dimension_semantics=("arbitrary",), vmem_limit_bytes=48 * 1024 * 1024),
        name="memkv",
    )(mem.reshape(b * ml, D_MODEL), gain, w)
    return out.reshape(nl, b, ml, 2 * BRANCH_W)


def _deltanet_stages(qkv, z, sm, cw_ref, gp_ref, on_ref, y_ref, tail_ref, s_ref, *, tt):
    H, C, DK = DN_HEADS, DN_CHUNK, DN_DK
    HC = H * C
    cw = cw_ref[0:CONV_W, :]

    def conv_silu(idx):
        lo = idx * BRANCH_W
        cur = qkv[:, lo:lo + BRANCH_W]
        ext = jnp.concatenate([tail_ref[idx], cur], axis=0)
        acc = cur * cw[CONV_W - 1:CONV_W, lo:lo + BRANCH_W]
        for j in range(1, CONV_W):
            shifted = pltpu.roll(ext, j, axis=0)[8:]
            acc = acc + shifted * cw[CONV_W - 1 - j:CONV_W - j, lo:lo + BRANCH_W]
        tail_ref[idx] = cur[tt - 8:]
        return _silu(acc)

    qc = conv_silu(0)
    yield
    kc = conv_silu(1)
    yield
    vc = conv_silu(2)
    yield

    def l2n(xh):
        return xh * lax.rsqrt(jnp.sum(xh * xh, axis=-1, keepdims=True) + EPS)

    ci = lax.broadcasted_iota(jnp.int32, (C, HC), 0)
    cl = lax.broadcasted_iota(jnp.int32, (C, HC), 1)
    cj = cl % C
    strict = ci > cj
    incl = ci >= cj
    eye_cat = (ci == cj).astype(F32)
    br = lax.broadcasted_iota(jnp.int32, (HC, HC), 0)
    bc = lax.broadcasted_iota(jnp.int32, (HC, HC), 1)
    bd_mask = jnp.where((br // C) == (bc // C), 1.0, 0.0).astype(BF16)
    kl = lax.broadcasted_iota(jnp.int32, (DK, HC), 1) // C
    head_lane_masks = [jnp.where(kl == h, 1.0, 0.0).astype(BF16) for h in range(H)]
    kr = lax.broadcasted_iota(jnp.int32, (HC, H * DK), 0) // C
    kc_ = lax.broadcasted_iota(jnp.int32, (HC, H * DK), 1) // DK
    kbd_mask = jnp.where(kr == kc_, 1.0, 0.0).astype(BF16)
    l64 = lax.broadcasted_iota(jnp.int32, (C, DK), 1) < C
    tr = lax.broadcasted_iota(jnp.int32, (C, C), 0)
    tc = lax.broadcasted_iota(jnp.int32, (C, C), 1)
    ltri = jnp.where(tr >= tc, 1.0, 0.0).astype(BF16)

    neg_a = gp_ref[0:1, :]
    dt_b = gp_ref[1:2, :]
    o_gain = on_ref[0:1, :]

    def stack(x, lo):
        return jnp.concatenate([x[lo:lo + C, h * DK:(h + 1) * DK] for h in range(H)], axis=0)

    def block_diag(x_cat):
        return jnp.concatenate([x_cat.astype(BF16)] * H, axis=0) * bd_mask

    def unstack(x_st):
        return jnp.concatenate([x_st[h * C:(h + 1) * C] for h in range(H)], axis=1)

    chunks = range(tt // C)
    qs, ks, vs, beta_all, gc3 = [], [], [], [], []
    for c in chunks:
        lo = c * C
        qs.append(jnp.concatenate([l2n(qc[lo:lo + C, h * DK:(h + 1) * DK]) for h in range(H)], axis=0) * (DK ** -0.5))
        ks.append(jnp.concatenate([l2n(kc[lo:lo + C, h * DK:(h + 1) * DK]) for h in range(H)], axis=0))
        vs.append(stack(vc, lo))
        smc = sm[lo:lo + C, :]
        beta_all.append(_sigmoid(smc))
        g_all = neg_a * _softplus(smc + dt_b)
        g_hi = g_all.astype(BF16)
        r1 = g_all - g_hi.astype(F32)
        g_mid = r1.astype(BF16)
        g_lo = (r1 - g_mid.astype(F32)).astype(BF16)
        gc3.append(_dot(ltri, jnp.concatenate([g_hi, g_mid, g_lo], axis=1)))
        yield

    cb, bb, decay, egc, kb, kq = [], [], [], [], [], []
    for c in chunks:
        gc_all = gc3[c][:, :LANES] + gc3[c][:, LANES:2 * LANES] + gc3[c][:, 2 * LANES:]
        gcb = [jnp.broadcast_to(gc_all[:, H + h:H + h + 1], (C, DK)) for h in range(H)]
        cb.append(jnp.concatenate(gcb, axis=0))
        bb.append(jnp.concatenate(
            [jnp.broadcast_to(beta_all[c][:, h:h + 1], (C, DK)) for h in range(H)], axis=0))
        c_cat = jnp.concatenate([jnp.where(l64, gcb[0], gcb[1]), jnp.where(l64, gcb[2], gcb[3])], axis=1)
        gct = jnp.concatenate([gc_all, gc_all], axis=0).T
        r_cat = jnp.concatenate(
            [jnp.where(l64[0:1], gct[H:H + 1], gct[H + 1:H + 2]),
             jnp.where(l64[0:1], gct[H + 2:H + 3], gct[H + 3:H + 4])], axis=1)
        decay.append(jnp.exp(jnp.where(incl, c_cat - r_cat, 0.0)))
        egc.append(jnp.exp(cb[c]))
        kb.append(ks[c] * bb[c])
        k_bd = jnp.concatenate([ks[c].astype(BF16)] * H, axis=1) * kbd_mask
        kq.append(_dot_nt(jnp.concatenate([unstack(kb[c]), unstack(qs[c])], axis=0).astype(BF16), k_bd))
        yield

    a_cat, p_cat, cur = [], [], []
    for c in chunks:
        n_cat = -jnp.where(strict, kq[c][:C] * decay[c], 0.0)
        a_cat.append(jnp.where(incl, kq[c][C:] * decay[c], 0.0))
        p_cat.append(eye_cat + n_cat)
        cur.append(_dot(n_cat.astype(BF16), block_diag(n_cat)))
        yield
    for _ in range(4):
        for c in chunks:
            out = _dot(jnp.concatenate([cur[c], p_cat[c]], axis=0).astype(BF16), block_diag(cur[c]))
            p_cat[c] = p_cat[c] + out[C:]
            cur[c] = out[:C]
            yield
    t_cat = [p_cat[c] + _dot(p_cat[c].astype(BF16), block_diag(cur[c])) for c in chunks]
    yield
    solb = []
    for c in chunks:
        rhs = jnp.concatenate([vs[c] * bb[c], kb[c] * egc[c]], axis=1)
        solb.append(_dot(block_diag(t_cat[c]), rhs.astype(BF16)).astype(BF16))
        yield
    auw, qp, gb = [], [], []
    for c in chunks:
        auw.append(_dot(block_diag(a_cat[c]), solb[c]))
        qp.append(qs[c] * egc[c] - auw[c][:, DK:])
        glast = jnp.concatenate(
            [jnp.broadcast_to(cb[c][h * C + C - 1:h * C + C, :], (C, DK)) for h in range(H)], axis=0)
        kd_t = (ks[c] * jnp.exp(glast - cb[c])).T.astype(BF16)
        gb.append([_dot(kd_t * head_lane_masks[h], solb[c]) for h in range(H)])
        yield

    for c in chunks:
        lo = c * C
        for h in range(H):
            s_h = s_ref[:, h * DK:(h + 1) * DK]
            gq = _dot(jnp.concatenate([gb[c][h][:, DK:], qp[c][h * C:(h + 1) * C]], axis=0).astype(BF16),
                      s_h.astype(BF16))
            d_h = jnp.exp(cb[c][h * C + C - 1:h * C + C, :])
            s_ref[:, h * DK:(h + 1) * DK] = s_h * d_h - gq[:DK] + gb[c][h][:, :DK]
            oh = _rms(gq[DK:] + auw[c][h * C:(h + 1) * C, :DK], o_gain)
            zh = z[lo:lo + C, h * DK:(h + 1) * DK]
            y_ref[lo:lo + C, h * DK:(h + 1) * DK] = (oh * _silu(zh)).astype(BF16)
        yield


def _gmlp_branch(uv, z, gain_ref, ws_ref, bias_ref, y_ref, tick, *, tt):
    P, G = GM_CHUNK, GM_GROUPS
    u = jax.nn.gelu(uv[:, :BRANCH_W])
    tick()
    v = _rms(jax.nn.gelu(uv[:, BRANCH_W:]), gain_ref[0:1, :])
    tick()
    vb = v.astype(BF16)
    tr = lax.broadcasted_iota(jnp.int32, (P, P), 0)
    tc = lax.broadcasted_iota(jnp.int32, (P, P), 1)
    causal = tr >= tc
    bias = bias_ref[...]
    for g in range(G):
        wg = jnp.where(causal, ws_ref[g], 0.0).astype(BF16)
        for ci in range(tt // P):
            lo = ci * P
            s = _dot(wg, vb[lo:lo + P, g * P:(g + 1) * P]) + bias[:, g * P:(g + 1) * P]
            zg = z[lo:lo + P, g * P:(g + 1) * P]
            y_ref[lo:lo + P, g * P:(g + 1) * P] = (u[lo:lo + P, g * P:(g + 1) * P] * s * _silu(zg)).astype(BF16)
        tick()


def _swa_branch(sink_of, q, z, kv_cur, y_ref, kvprev_ref, tick, has_prev, *, tt):
    P, HD, KV = SW_BLOCK, SW_HD, SW_KV_HEADS
    G = SW_HEADS // KV
    kj = lax.broadcasted_iota(jnp.int32, (2 * P, P), 0)
    qi = lax.broadcasted_iota(jnp.int32, (2 * P, P), 1)
    dist = qi + P - kj
    window = jnp.logical_and(dist >= 0, dist < P)
    valid_mid = jnp.concatenate([window] * G, axis=1)
    valid_first = jnp.concatenate([jnp.logical_and(window, jnp.logical_or(kj >= P, has_prev))] * G, axis=1)
    kv = jnp.concatenate([kvprev_ref[...], kv_cur], axis=0)
    kvprev_ref[...] = kv_cur[tt - P:, :]
    qb = (q * (HD ** -0.5)).astype(BF16)
    for kh in range(KV):
        sink = jnp.concatenate([jnp.full((1, P), sink_of(kh * G + g), F32) for g in range(G)], axis=1)
        k_all = kv[:, kh * HD:(kh + 1) * HD].astype(BF16)
        v_all_t = kv[:, (KV + kh) * HD:(KV + kh + 1) * HD].T.astype(BF16)
        for j in range(tt // P):
            lo = j * P
            qs = jnp.concatenate(
                [qb[lo:lo + P, (kh * G + g) * HD:(kh * G + g + 1) * HD] for g in range(G)], axis=0)
            s = jnp.where(valid_first if j == 0 else valid_mid, _dot_nt(k_all[lo:lo + 2 * P], qs), NEG_INF)
            mx = jnp.maximum(jnp.max(s, axis=0, keepdims=True), sink)
            e = jnp.exp(s - mx)
            den = jnp.sum(e, axis=0, keepdims=True) + jnp.exp(sink - mx)
            o_t = _dot(v_all_t[:, lo:lo + 2 * P], e.astype(BF16)) * (1.0 / den)
            for g2 in range(G // 2):
                h0 = kh * G + 2 * g2
                pair = jnp.concatenate(
                    [o_t[:, 2 * g2 * P:(2 * g2 + 1) * P], o_t[:, (2 * g2 + 1) * P:(2 * g2 + 2) * P]], axis=0).T
                zp = z[lo:lo + P, h0 * HD:(h0 + 2) * HD]
                y_ref[lo:lo + P, h0 * HD:(h0 + 2) * HD] = (pair * _silu(zp)).astype(BF16)
            tick()


def _memattn_branch(q, z, kv_ref, y_ref, tick):
    HD = XM_HD
    qb = q.astype(BF16)
    for h in range(XM_HEADS):
        mk = kv_ref[:, h * HD:(h + 1) * HD]
        mv = kv_ref[:, BRANCH_W + h * HD:BRANCH_W + (h + 1) * HD]
        s = _dot_nt(qb[:, h * HD:(h + 1) * HD], mk) * (HD ** -0.5)
        e = jnp.exp(s - jnp.max(s, axis=-1, keepdims=True))
        o = _dot(e.astype(BF16), mv) * (1.0 / jnp.sum(e, axis=-1, keepdims=True))
        y_ref[:, h * HD:(h + 1) * HD] = (o * _silu(z[:, h * HD:(h + 1) * HD])).astype(BF16)
        tick()


def _layer_kernel(sinks_ref, x_ref, npre_ref, w_ref, cw_ref, gp_ref, on_ref, gmn_ref, ws_ref, bias_ref,
                  mkv_ref, wup_ref, wout_ref, npost_ref, o_ref, tail_ref, s_ref, kvprev_ref, y_ref, cols_ref,
                  merged_ref, *, step_rows, tt, layer):
    @pl.when(pl.program_id(1) == 0)
    def _():
        tail_ref[...] = jnp.zeros_like(tail_ref)
        s_ref[...] = jnp.zeros_like(s_ref)
        kvprev_ref[...] = jnp.zeros_like(kvprev_ref)

    for si in range(step_rows // tt):
        rows = pl.ds(si * tt, tt)
        has_prev = pl.program_id(1) > 0 if si == 0 else True
        _layer_tile(sinks_ref, x_ref.at[rows], npre_ref, w_ref, cw_ref, gp_ref, on_ref, gmn_ref, ws_ref, bias_ref,
                    mkv_ref, wup_ref, wout_ref, npost_ref, o_ref.at[rows], tail_ref, s_ref, kvprev_ref, y_ref,
                    cols_ref, merged_ref, has_prev, tt=tt, layer=layer)


def _layer_tile(sinks_ref, x_ref, npre_ref, w_ref, cw_ref, gp_ref, on_ref, gmn_ref, ws_ref, bias_ref,
                mkv_ref, wup_ref, wout_ref, npost_ref, o_ref, tail_ref, s_ref, kvprev_ref, y_ref, cols_ref,
                merged_ref, has_prev, *, tt, layer):
    x = x_ref[...]
    h = _rms(x, npre_ref[0:1, :]).astype(BF16)

    def proj(lo, width):
        return _dot(h, w_ref[:, lo:lo + width])

    filled = [OFF_BUV]
    merged_started = set()

    def proj_item(lo):
        def run():
            cols_ref[:, lo - OFF_BUV:lo - OFF_BUV + PROJ_TILE] = proj(lo, PROJ_TILE)
            filled[0] = lo + PROJ_TILE
        return run

    def col(lo, width):
        assert lo + width <= filled[0], "projection read before it was issued"
        return cols_ref[:, lo - OFF_BUV:lo - OFF_BUV + width]

    def merge_item(n, lo):
        def run():
            g = _sigmoid(col(OFF_GATES + n * D_MODEL + lo, PROJ_TILE))
            term = g * _dot(y_ref[n], wup_ref[n, :, lo:lo + PROJ_TILE])
            if lo in merged_started:
                merged_ref[:, lo:lo + PROJ_TILE] += term
            else:
                merged_ref[:, lo:lo + PROJ_TILE] = term
                merged_started.add(lo)
        return run

    queue = collections.deque(proj_item(lo) for lo in range(OFF_BUV, W_COLS, PROJ_TILE))

    def tick(count=1):
        for _ in range(count):
            if queue:
                queue.popleft()()

    def branch_done(n):
        queue.extend(merge_item(n, lo) for lo in range(0, D_MODEL, PROJ_TILE))

    def branch_b():
        _gmlp_branch(col(OFF_BUV, 2 * BRANCH_W), col(OFF_BZ, BRANCH_W), gmn_ref, ws_ref, bias_ref,
                     y_ref.at[1], tick, tt=tt)
        branch_done(1)

    def branch_c():
        _swa_branch(lambda hd: sinks_ref[layer, hd], col(OFF_CQ, BRANCH_W), col(OFF_CZ, BRANCH_W),
                    col(OFF_CKV, CKV_COLS), y_ref.at[2], kvprev_ref, tick, has_prev, tt=tt)
        branch_done(2)

    def branch_m():
        _memattn_branch(col(OFF_MQ, BRANCH_W), col(OFF_MZ, BRANCH_W), mkv_ref, y_ref.at[3], tick)
        branch_done(3)

    bodies = {BODY_AT[0]: branch_b, BODY_AT[1]: branch_c, BODY_AT[2]: branch_m}
    stages = _deltanet_stages(proj(OFF_AQKV, 3 * BRANCH_W), proj(OFF_AZ, BRANCH_W), proj(OFF_SMALL, LANES),
                              cw_ref, gp_ref, on_ref, y_ref.at[0], tail_ref, s_ref, tt=tt)
    for i, _ in enumerate(stages):
        tick(ITEMS_AT.get(i, 1))
        if i in bodies:
            bodies.pop(i)()
    assert not bodies, "stage list shorter than BODY_AT"
    branch_done(0)
    while queue:
        tick()

    out = _dot(merged_ref[...].astype(BF16), wout_ref[...])
    o_ref[...] = x + _rms(out, npost_ref[0:1, :])


def _layer_call(x2, layer, sinks, norm_pre, w_pad, conv_w, gate_params, o_norm, gm_norm, spatial_w, bias_mat,
                mem_kv, w_up, w_out, norm_post, *, batch, seq, step_rows, tt):
    nt = seq // step_rows
    ml = mem_kv.shape[2]
    row = lambda b, t, s: (b * nt + t, 0)

    def const(arr):
        shape = arr.shape[1:]
        return pl.BlockSpec((None,) + shape, lambda b, t, s: (layer,) + (0,) * len(shape),
                            pipeline_mode=pl.Buffered(1))

    in_specs = [
        pl.BlockSpec((step_rows, D_MODEL), row),
        const(norm_pre), const(w_pad), const(conv_w), const(gate_params), const(o_norm), const(gm_norm),
        const(spatial_w), const(bias_mat),
        pl.BlockSpec((None, None, ml, 2 * BRANCH_W), lambda b, t, s: (layer, b, 0, 0)),
        const(w_up), const(w_out), const(norm_post),
    ]
    return pl.pallas_call(
        functools.partial(_layer_kernel, step_rows=step_rows, tt=tt, layer=layer),
        grid_spec=pltpu.PrefetchScalarGridSpec(
            num_scalar_prefetch=1,
            grid=(batch, nt),
            in_specs=in_specs,
            out_specs=pl.BlockSpec((step_rows, D_MODEL), row),
            scratch_shapes=[
                pltpu.VMEM((3, 8, BRANCH_W), F32),
                pltpu.VMEM((DN_DK, DN_HEADS * DN_DK), F32),
                pltpu.VMEM((SW_BLOCK, CKV_COLS), F32),
                pltpu.VMEM((N_BRANCH, tt, BRANCH_W), BF16),
                pltpu.VMEM((tt, W_COLS - OFF_BUV), F32),
                pltpu.VMEM((tt, D_MODEL), F32),
            ],
        ),
        out_shape=jax.ShapeDtypeStruct((batch * seq, D_MODEL), F32),
        compiler_params=pltpu.CompilerParams(
            dimension_semantics=("arbitrary", "arbitrary"), vmem_limit_bytes=VMEM_LIMIT),
        name="layer",
    )(sinks, x2, norm_pre, w_pad, conv_w, gate_params, o_norm, gm_norm, spatial_w, bias_mat, mem_kv,
      w_up, w_out, norm_post)


def kernel(x, mem, norm_pre, norm_post, norm_mem, w_in, conv_w, a_log, dt_bias, dn_norm, gm_norm,
           spatial_w, spatial_b, sinks, w_mem_kv, w_up, w_out):
    batch, seq, d = x.shape
    nl = w_in.shape[0]
    assert d == D_MODEL and seq % STEP_ROWS == 0 and STEP_ROWS % TT_LAYER == 0
    assert TT_LAYER % SW_BLOCK == 0 and TT_LAYER % GM_CHUNK == 0 and TT_LAYER % DN_CHUNK == 0
    assert w_in.shape[1:] == (D_MODEL, W_COLS - PAD_ROWS) and mem.shape[0] == batch and mem.shape[2] == D_MODEL
    w_pad = _pad_w_in(jnp.swapaxes(w_in, 1, 2))
    lane_pad = ((0, 0), (DN_HEADS, LANES - 2 * DN_HEADS))
    gate_params = jnp.stack([jnp.pad(-jnp.exp(a_log), lane_pad), jnp.pad(dt_bias, lane_pad)], axis=1)
    gate_params = jnp.pad(gate_params, ((0, 0), (0, SMALL_ROWS - 2), (0, 0)))
    o_norm = jnp.broadcast_to(dn_norm[:, None, :], (nl, SMALL_ROWS, DN_DK))
    period_rows = lambda width: SMALL_ROWS * LANES // width
    npre = jnp.broadcast_to(norm_pre[:, None, :], (nl, period_rows(d), d))
    npost = jnp.broadcast_to(norm_post[:, None, :], (nl, period_rows(d), d))
    gmn = jnp.broadcast_to(gm_norm[:, None, :], (nl, period_rows(BRANCH_W), BRANCH_W))
    conv_p = jnp.pad(conv_w, ((0, 0), (0, 8 - CONV_W), (0, 0)))
    bias_mat = jnp.repeat(jnp.swapaxes(spatial_b, 1, 2), GM_CHUNK, axis=2)
    mem_kv = _memkv(mem, norm_mem[:, None, :], w_mem_kv)
    w_up_b = w_up.astype(BF16)
    w_out_b = w_out.astype(BF16)

    x2 = x.reshape(batch * seq, d)
    for l in range(nl):
        x2 = _layer_call(x2, l, sinks, npre, w_pad, conv_p, gate_params, o_norm, gmn, spatial_w, bias_mat, mem_kv,
                         w_up_b, w_out_b, npost, batch=batch, seq=seq, step_rows=STEP_ROWS, tt=TT_LAYER)
    return x2.reshape(batch, seq, d)
```

```python
import collections
import functools

import jax
import jax.numpy as jnp
from jax import lax
from jax.experimental import pallas as pl
from jax.experimental.pallas import tpu as pltpu

F32 = jnp.float32
BF16 = jnp.bfloat16

D_MODEL = 1024
N_BRANCH = 4
BRANCH_W = 512
DN_HEADS = 4
DN_DK = 128
DN_CHUNK = 64
CONV_W = 4
GM_GROUPS = 4
GM_CHUNK = 128
SW_HEADS = 8
SW_KV_HEADS = 2
SW_HD = 64
SW_BLOCK = 128
XM_HEADS = 4
XM_HD = 128
EPS = 1e-6
NEG_INF = -1e30
LANES = 128
SUBLANES = 8
SMALL_ROWS = 32

CKV_COLS = 2 * SW_KV_HEADS * SW_HD
OFF_AQKV = 0
OFF_AZ = OFF_AQKV + 3 * BRANCH_W
OFF_SMALL = OFF_AZ + BRANCH_W
N_SMALL = 2 * DN_HEADS
OFF_BUV = OFF_SMALL + LANES
OFF_BZ = OFF_BUV + 2 * BRANCH_W
OFF_CQ = OFF_BZ + BRANCH_W
OFF_CKV = OFF_CQ + BRANCH_W
OFF_CZ = OFF_CKV + CKV_COLS
OFF_MQ = OFF_CZ + BRANCH_W
OFF_MZ = OFF_MQ + BRANCH_W
OFF_GATES = OFF_MZ + BRANCH_W
W_COLS = OFF_GATES + N_BRANCH * D_MODEL

VMEM_LIMIT = 60 * 1024 * 1024
VMEM_LIMIT_PREP = 48 * 1024 * 1024
MEMKV_ROWS = 512
TT_LAYER = 256
STEP_ROWS = 256
PROJ_TILE = 256
BODY_AT = (6, 20, 30)
ITEMS_AT = {0: 2, 3: 2}


def _sigmoid(x):
    return 0.5 * jnp.tanh(0.5 * x) + 0.5


def _silu(x):
    u = 0.5 * x
    return u + u * jnp.tanh(u)


def _softplus(x):
    return jnp.maximum(x, 0.0) + jnp.log(1.0 + jnp.exp(-jnp.abs(x)))


def _rms(x, gain):
    return x * lax.rsqrt(jnp.mean(x * x, axis=-1, keepdims=True) + EPS) * gain


def _dot(a, b):
    return jnp.dot(a, b, preferred_element_type=F32)


def _dot_nt(a, b):
    return lax.dot_general(a, b, (((1,), (1,)), ((), ())), preferred_element_type=F32)


PADW_COLS = 2048
PAD_ROWS = LANES - N_SMALL


def _pad_w_kernel(prev_ref, cur_ref, o_ref):
    i = pl.program_id(1)
    small_blk = OFF_SMALL // PADW_COLS
    keep = PADW_COLS - PAD_ROWS

    @pl.when(i < small_blk)
    def _():
        o_ref[...] = cur_ref[...].T.astype(BF16)

    @pl.when(i == small_blk)
    def _():
        cur = cur_ref[...]
        src = jnp.concatenate([cur[:N_SMALL], jnp.zeros((PAD_ROWS, cur.shape[1]), F32), cur[N_SMALL:keep]], axis=0)
        o_ref[...] = src.T.astype(BF16)

    @pl.when(i > small_blk)
    def _():
        src = jnp.concatenate([prev_ref[LANES - PAD_ROWS:], cur_ref[:keep]], axis=0)
        o_ref[...] = src.T.astype(BF16)


def _pad_w_in(w_t):
    nl, n_in, d = w_t.shape
    assert n_in == W_COLS - PAD_ROWS and OFF_SMALL % PADW_COLS == 0 and OFF_BUV - OFF_SMALL == LANES
    sub = PADW_COLS // LANES
    return pl.pallas_call(
        _pad_w_kernel,
        grid=(nl, pl.cdiv(W_COLS, PADW_COLS)),
        in_specs=[
            pl.BlockSpec((None, LANES, d), lambda l, i: (l, jnp.maximum(i * sub - 1, 0), 0)),
            pl.BlockSpec((None, PADW_COLS, d), lambda l, i: (l, i, 0)),
        ],
        out_specs=pl.BlockSpec((None, d, PADW_COLS), lambda l, i: (l, 0, i)),
        out_shape=jax.ShapeDtypeStruct((nl, d, W_COLS), BF16),
        compiler_params=pltpu.CompilerParams(
            dimension_semantics=("arbitrary", "arbitrary"), vmem_limit_bytes=VMEM_LIMIT_PREP),
        name="padw",
    )(w_t, w_t)


def _memkv_kernel(m_ref, g_ref, w_ref, o_ref):
    wb = w_ref[...].astype(BF16)
    for lo in range(0, m_ref.shape[0], MEMKV_ROWS):
        o_ref[lo:lo + MEMKV_ROWS] = _dot(_rms(m_ref[lo:lo + MEMKV_ROWS], g_ref[...]).astype(BF16), wb).astype(BF16)


def _memkv(mem, gain, w):
    b, ml, _ = mem.shape
    nl = w.shape[0]
    assert (b * ml) % MEMKV_ROWS == 0
    out = pl.pallas_call(
        _memkv_kernel,
        grid=(nl,),
        in_specs=[
            pl.BlockSpec((b * ml, D_MODEL), lambda l: (0, 0)),
            pl.BlockSpec((None, 1, D_MODEL), lambda l: (l, 0, 0)),
            pl.BlockSpec((None, D_MODEL, 2 * BRANCH_W), lambda l: (l, 0, 0)),
        ],
        out_specs=pl.BlockSpec((None, b * ml, 2 * BRANCH_W), lambda l: (l, 0, 0)),
        out_shape=jax.ShapeDtypeStruct((nl, b * ml, 2 * BRANCH_W), BF16),
        compiler_params=pltpu.CompilerParams(
            dimension_semantics=("arbitrary",), vmem_limit_bytes=VMEM_LIMIT_PREP),
        name="memkv",
    )(mem.reshape(b * ml, D_MODEL), gain, w)
    return out.reshape(nl, b, ml, 2 * BRANCH_W)


def _deltanet_stages(qkv, z, sm, cw_ref, gp_ref, on_ref, y_ref, tail_ref, s_ref, *, tt):
    H, C, DK = DN_HEADS, DN_CHUNK, DN_DK
    HC = H * C
    cw = cw_ref[0:CONV_W, :]

    def conv_silu(idx):
        lo = idx * BRANCH_W
        cur = qkv[:, lo:lo + BRANCH_W]
        ext = jnp.concatenate([tail_ref[idx], cur], axis=0)
        acc = cur * cw[CONV_W - 1:CONV_W, lo:lo + BRANCH_W]
        for j in range(1, CONV_W):
            shifted = pltpu.roll(ext, j, axis=0)[SUBLANES:]
            acc = acc + shifted * cw[CONV_W - 1 - j:CONV_W - j, lo:lo + BRANCH_W]
        tail_ref[idx] = cur[tt - SUBLANES:]
        return _silu(acc)

    qc = conv_silu(0)
    yield
    kc = conv_silu(1)
    yield
    vc = conv_silu(2)
    yield

    def l2n(xh):
        return xh * lax.rsqrt(jnp.sum(xh * xh, axis=-1, keepdims=True) + EPS)

    ci = lax.broadcasted_iota(jnp.int32, (C, HC), 0)
    cl = lax.broadcasted_iota(jnp.int32, (C, HC), 1)
    cj = cl % C
    strict = ci > cj
    incl = ci >= cj
    eye_cat = (ci == cj).astype(F32)
    br = lax.broadcasted_iota(jnp.int32, (HC, HC), 0)
    bc = lax.broadcasted_iota(jnp.int32, (HC, HC), 1)
    bd_mask = jnp.where((br // C) == (bc // C), 1.0, 0.0).astype(BF16)
    kl = lax.broadcasted_iota(jnp.int32, (DK, HC), 1) // C
    head_lane_masks = [jnp.where(kl == h, 1.0, 0.0).astype(BF16) for h in range(H)]
    kr = lax.broadcasted_iota(jnp.int32, (HC, H * DK), 0) // C
    kc_ = lax.broadcasted_iota(jnp.int32, (HC, H * DK), 1) // DK
    kbd_mask = jnp.where(kr == kc_, 1.0, 0.0).astype(BF16)
    l64 = lax.broadcasted_iota(jnp.int32, (C, DK), 1) < C
    tr = lax.broadcasted_iota(jnp.int32, (C, C), 0)
    tc = lax.broadcasted_iota(jnp.int32, (C, C), 1)
    ltri = jnp.where(tr >= tc, 1.0, 0.0).astype(BF16)

    neg_a = gp_ref[0:1, :]
    dt_b = gp_ref[1:2, :]
    o_gain = on_ref[0:1, :]

    def stack(x, lo):
        return jnp.concatenate([x[lo:lo + C, h * DK:(h + 1) * DK] for h in range(H)], axis=0)

    def block_diag(x_cat):
        return jnp.concatenate([x_cat.astype(BF16)] * H, axis=0) * bd_mask

    def unstack(x_st):
        return jnp.concatenate([x_st[h * C:(h + 1) * C] for h in range(H)], axis=1)

    chunks = range(tt // C)
    qs, ks, vs, beta_all, gc3 = [], [], [], [], []
    for c in chunks:
        lo = c * C
        qs.append(jnp.concatenate([l2n(qc[lo:lo + C, h * DK:(h + 1) * DK]) for h in range(H)], axis=0) * (DK ** -0.5))
        ks.append(jnp.concatenate([l2n(kc[lo:lo + C, h * DK:(h + 1) * DK]) for h in range(H)], axis=0))
        vs.append(stack(vc, lo))
        smc = sm[lo:lo + C, :]
        beta_all.append(_sigmoid(smc))
        g_all = neg_a * _softplus(smc + dt_b)
        g_hi = g_all.astype(BF16)
        r1 = g_all - g_hi.astype(F32)
        g_mid = r1.astype(BF16)
        g_lo = (r1 - g_mid.astype(F32)).astype(BF16)
        gc3.append(_dot(ltri, jnp.concatenate([g_hi, g_mid, g_lo], axis=1)))
        yield

    cb, bb, decay, egc, kb, kq = [], [], [], [], [], []
    for c in chunks:
        gc_all = gc3[c][:, :LANES] + gc3[c][:, LANES:2 * LANES] + gc3[c][:, 2 * LANES:]
        gcb = [jnp.broadcast_to(gc_all[:, H + h:H + h + 1], (C, DK)) for h in range(H)]
        cb.append(jnp.concatenate(gcb, axis=0))
        bb.append(jnp.concatenate(
            [jnp.broadcast_to(beta_all[c][:, h:h + 1], (C, DK)) for h in range(H)], axis=0))
        c_cat = jnp.concatenate([jnp.where(l64, gcb[0], gcb[1]), jnp.where(l64, gcb[2], gcb[3])], axis=1)
        gct = jnp.concatenate([gc_all, gc_all], axis=0).T
        r_cat = jnp.concatenate(
            [jnp.where(l64[0:1], gct[H:H + 1], gct[H + 1:H + 2]),
             jnp.where(l64[0:1], gct[H + 2:H + 3], gct[H + 3:H + 4])], axis=1)
        decay.append(jnp.exp(jnp.where(incl, c_cat - r_cat, 0.0)))
        egc.append(jnp.exp(cb[c]))
        kb.append(ks[c] * bb[c])
        k_bd = jnp.concatenate([ks[c].astype(BF16)] * H, axis=1) * kbd_mask
        kq.append(_dot_nt(jnp.concatenate([unstack(kb[c]), unstack(qs[c])], axis=0).astype(BF16), k_bd))
        yield

    a_cat, p_cat, cur = [], [], []
    for c in chunks:
        n_cat = -jnp.where(strict, kq[c][:C] * decay[c], 0.0)
        a_cat.append(jnp.where(incl, kq[c][C:] * decay[c], 0.0))
        p_cat.append(eye_cat + n_cat)
        cur.append(_dot(n_cat.astype(BF16), block_diag(n_cat)))
        yield
    for _ in range(4):
        for c in chunks:
            out = _dot(jnp.concatenate([cur[c], p_cat[c]], axis=0).astype(BF16), block_diag(cur[c]))
            p_cat[c] = p_cat[c] + out[C:]
            cur[c] = out[:C]
            yield
    t_cat = [p_cat[c] + _dot(p_cat[c].astype(BF16), block_diag(cur[c])) for c in chunks]
    yield
    solb = []
    for c in chunks:
        rhs = jnp.concatenate([vs[c] * bb[c], kb[c] * egc[c]], axis=1)
        solb.append(_dot(block_diag(t_cat[c]), rhs.astype(BF16)).astype(BF16))
        yield
    auw, qp, gb = [], [], []
    for c in chunks:
        auw.append(_dot(block_diag(a_cat[c]), solb[c]))
        qp.append(qs[c] * egc[c] - auw[c][:, DK:])
        glast = jnp.concatenate(
            [jnp.broadcast_to(cb[c][h * C + C - 1:h * C + C, :], (C, DK)) for h in range(H)], axis=0)
        kd_t = (ks[c] * jnp.exp(glast - cb[c])).T.astype(BF16)
        gb.append([_dot(kd_t * head_lane_masks[h], solb[c]) for h in range(H)])
        yield

    for c in chunks:
        lo = c * C
        for h in range(H):
            s_h = s_ref[:, h * DK:(h + 1) * DK]
            gq = _dot(jnp.concatenate([gb[c][h][:, DK:], qp[c][h * C:(h + 1) * C]], axis=0).astype(BF16),
                      s_h.astype(BF16))
            d_h = jnp.exp(cb[c][h * C + C - 1:h * C + C, :])
            s_ref[:, h * DK:(h + 1) * DK] = s_h * d_h - gq[:DK] + gb[c][h][:, :DK]
            oh = _rms(gq[DK:] + auw[c][h * C:(h + 1) * C, :DK], o_gain)
            zh = z[lo:lo + C, h * DK:(h + 1) * DK]
            y_ref[lo:lo + C, h * DK:(h + 1) * DK] = (oh * _silu(zh)).astype(BF16)
        yield


def _gmlp_branch(uv, z, gain_ref, ws_ref, bias_ref, y_ref, tick, *, tt):
    P, G = GM_CHUNK, GM_GROUPS
    u = jax.nn.gelu(uv[:, :BRANCH_W])
    tick()
    v = _rms(jax.nn.gelu(uv[:, BRANCH_W:]), gain_ref[0:1, :])
    tick()
    vb = v.astype(BF16)
    tr = lax.broadcasted_iota(jnp.int32, (P, P), 0)
    tc = lax.broadcasted_iota(jnp.int32, (P, P), 1)
    causal = tr >= tc
    bias = bias_ref[...]
    for g in range(G):
        wg = jnp.where(causal, ws_ref[g], 0.0).astype(BF16)
        for ci in range(tt // P):
            lo = ci * P
            s = _dot(wg, vb[lo:lo + P, g * P:(g + 1) * P]) + bias[:, g * P:(g + 1) * P]
            zg = z[lo:lo + P, g * P:(g + 1) * P]
            y_ref[lo:lo + P, g * P:(g + 1) * P] = (u[lo:lo + P, g * P:(g + 1) * P] * s * _silu(zg)).astype(BF16)
        tick()


def _swa_branch(sink_of, q, z, kv_cur, y_ref, kvprev_ref, tick, has_prev, *, tt):
    P, HD, KV = SW_BLOCK, SW_HD, SW_KV_HEADS
    G = SW_HEADS // KV
    kj = lax.broadcasted_iota(jnp.int32, (2 * P, P), 0)
    qi = lax.broadcasted_iota(jnp.int32, (2 * P, P), 1)
    dist = qi + P - kj
    window = jnp.logical_and(dist >= 0, dist < P)
    valid_mid = jnp.concatenate([window] * G, axis=1)
    valid_first = jnp.concatenate([jnp.logical_and(window, jnp.logical_or(kj >= P, has_prev))] * G, axis=1)
    kv = jnp.concatenate([kvprev_ref[...], kv_cur], axis=0)
    kvprev_ref[...] = kv_cur[tt - P:, :]
    qb = (q * (HD ** -0.5)).astype(BF16)
    for kh in range(KV):
        sink = jnp.concatenate([jnp.full((1, P), sink_of(kh * G + g), F32) for g in range(G)], axis=1)
        k_all = kv[:, kh * HD:(kh + 1) * HD].astype(BF16)
        v_all_t = kv[:, (KV + kh) * HD:(KV + kh + 1) * HD].T.astype(BF16)
        for j in range(tt // P):
            lo = j * P
            qs = jnp.concatenate(
                [qb[lo:lo + P, (kh * G + g) * HD:(kh * G + g + 1) * HD] for g in range(G)], axis=0)
            s = jnp.where(valid_first if j == 0 else valid_mid, _dot_nt(k_all[lo:lo + 2 * P], qs), NEG_INF)
            mx = jnp.maximum(jnp.max(s, axis=0, keepdims=True), sink)
            e = jnp.exp(s - mx)
            den = jnp.sum(e, axis=0, keepdims=True) + jnp.exp(sink - mx)
            o_t = _dot(v_all_t[:, lo:lo + 2 * P], e.astype(BF16)) * (1.0 / den)
            for g2 in range(G // 2):
                h0 = kh * G + 2 * g2
                pair = jnp.concatenate(
                    [o_t[:, 2 * g2 * P:(2 * g2 + 1) * P], o_t[:, (2 * g2 + 1) * P:(2 * g2 + 2) * P]], axis=0).T
                zp = z[lo:lo + P, h0 * HD:(h0 + 2) * HD]
                y_ref[lo:lo + P, h0 * HD:(h0 + 2) * HD] = (pair * _silu(zp)).astype(BF16)
            tick()


def _memattn_branch(q, z, kv_ref, y_ref, tick):
    HD = XM_HD
    qb = q.astype(BF16)
    for h in range(XM_HEADS):
        mk = kv_ref[:, h * HD:(h + 1) * HD]
        mv = kv_ref[:, BRANCH_W + h * HD:BRANCH_W + (h + 1) * HD]
        s = _dot_nt(qb[:, h * HD:(h + 1) * HD], mk) * (HD ** -0.5)
        e = jnp.exp(s - jnp.max(s, axis=-1, keepdims=True))
        o = _dot(e.astype(BF16), mv) * (1.0 / jnp.sum(e, axis=-1, keepdims=True))
        y_ref[:, h * HD:(h + 1) * HD] = (o * _silu(z[:, h * HD:(h + 1) * HD])).astype(BF16)
        tick()


def _layer_kernel(sinks_ref, x_ref, npre_ref, w_ref, cw_ref, gp_ref, on_ref, gmn_ref, ws_ref, bias_ref,
                  mkv_ref, wup_ref, wout_ref, npost_ref, o_ref, tail_ref, s_ref, kvprev_ref, y_ref, cols_ref,
                  merged_ref, *, step_rows, tt, layer):
    @pl.when(pl.program_id(1) == 0)
    def _():
        tail_ref[...] = jnp.zeros_like(tail_ref)
        s_ref[...] = jnp.zeros_like(s_ref)
        kvprev_ref[...] = jnp.zeros_like(kvprev_ref)

    for si in range(step_rows // tt):
        rows = pl.ds(si * tt, tt)
        has_prev = pl.program_id(1) > 0 if si == 0 else True
        _layer_tile(sinks_ref, x_ref.at[rows], npre_ref, w_ref, cw_ref, gp_ref, on_ref, gmn_ref, ws_ref, bias_ref,
                    mkv_ref, wup_ref, wout_ref, npost_ref, o_ref.at[rows], tail_ref, s_ref, kvprev_ref, y_ref,
                    cols_ref, merged_ref, has_prev, tt=tt, layer=layer)


def _layer_tile(sinks_ref, x_ref, npre_ref, w_ref, cw_ref, gp_ref, on_ref, gmn_ref, ws_ref, bias_ref,
                mkv_ref, wup_ref, wout_ref, npost_ref, o_ref, tail_ref, s_ref, kvprev_ref, y_ref, cols_ref,
                merged_ref, has_prev, *, tt, layer):
    x = x_ref[...]
    h = _rms(x, npre_ref[0:1, :]).astype(BF16)

    def proj(lo, width):
        return _dot(h, w_ref[:, lo:lo + width])

    filled = [OFF_BUV]
    merged_started = set()

    def proj_item(lo):
        def run():
            cols_ref[:, lo - OFF_BUV:lo - OFF_BUV + PROJ_TILE] = proj(lo, PROJ_TILE)
            filled[0] = lo + PROJ_TILE
        return run

    def col(lo, width):
        assert lo + width <= filled[0], "projection read before it was issued"
        return cols_ref[:, lo - OFF_BUV:lo - OFF_BUV + width]

    def merge_item(n, lo):
        def run():
            g = _sigmoid(col(OFF_GATES + n * D_MODEL + lo, PROJ_TILE))
            term = g * _dot(y_ref[n], wup_ref[n, :, lo:lo + PROJ_TILE])
            if lo in merged_started:
                merged_ref[:, lo:lo + PROJ_TILE] += term
            else:
                merged_ref[:, lo:lo + PROJ_TILE] = term
                merged_started.add(lo)
        return run

    queue = collections.deque(proj_item(lo) for lo in range(OFF_BUV, W_COLS, PROJ_TILE))

    def tick(count=1):
        for _ in range(count):
            if queue:
                queue.popleft()()

    def branch_done(n):
        queue.extend(merge_item(n, lo) for lo in range(0, D_MODEL, PROJ_TILE))

    def branch_b():
        _gmlp_branch(col(OFF_BUV, 2 * BRANCH_W), col(OFF_BZ, BRANCH_W), gmn_ref, ws_ref, bias_ref,
                     y_ref.at[1], tick, tt=tt)
        branch_done(1)

    def branch_c():
        _swa_branch(lambda hd: sinks_ref[layer, hd], col(OFF_CQ, BRANCH_W), col(OFF_CZ, BRANCH_W),
                    col(OFF_CKV, CKV_COLS), y_ref.at[2], kvprev_ref, tick, has_prev, tt=tt)
        branch_done(2)

    def branch_m():
        _memattn_branch(col(OFF_MQ, BRANCH_W), col(OFF_MZ, BRANCH_W), mkv_ref, y_ref.at[3], tick)
        branch_done(3)

    bodies = {BODY_AT[0]: branch_b, BODY_AT[1]: branch_c, BODY_AT[2]: branch_m}
    stages = _deltanet_stages(proj(OFF_AQKV, 3 * BRANCH_W), proj(OFF_AZ, BRANCH_W), proj(OFF_SMALL, LANES),
                              cw_ref, gp_ref, on_ref, y_ref.at[0], tail_ref, s_ref, tt=tt)
    for i, _ in enumerate(stages):
        tick(ITEMS_AT.get(i, 1))
        if i in bodies:
            bodies.pop(i)()
    assert not bodies, "stage list shorter than BODY_AT"
    branch_done(0)
    while queue:
        tick()

    out = _dot(merged_ref[...].astype(BF16), wout_ref[...])
    o_ref[...] = x + _rms(out, npost_ref[0:1, :])


def _layer_call(x2, layer, sinks, norm_pre, w_pad, conv_w, gate_params, o_norm, gm_norm, spatial_w, bias_mat,
                mem_kv, w_up, w_out, norm_post, *, batch, seq, step_rows, tt):
    nt = seq // step_rows
    ml = mem_kv.shape[2]
    row = lambda b, t, s: (b * nt + t, 0)

    def const(arr):
        shape = arr.shape[1:]
        return pl.BlockSpec((None,) + shape, lambda b, t, s: (layer,) + (0,) * len(shape),
                            pipeline_mode=pl.Buffered(1))

    in_specs = [
        pl.BlockSpec((step_rows, D_MODEL), row),
        const(norm_pre), const(w_pad), const(conv_w), const(gate_params), const(o_norm), const(gm_norm),
        const(spatial_w), const(bias_mat),
        pl.BlockSpec((None, None, ml, 2 * BRANCH_W), lambda b, t, s: (layer, b, 0, 0)),
        const(w_up), const(w_out), const(norm_post),
    ]
    return pl.pallas_call(
        functools.partial(_layer_kernel, step_rows=step_rows, tt=tt, layer=layer),
        grid_spec=pltpu.PrefetchScalarGridSpec(
            num_scalar_prefetch=1,
            grid=(batch, nt),
            in_specs=in_specs,
            out_specs=pl.BlockSpec((step_rows, D_MODEL), row),
            scratch_shapes=[
                pltpu.VMEM((3, SUBLANES, BRANCH_W), F32),
                pltpu.VMEM((DN_DK, DN_HEADS * DN_DK), F32),
                pltpu.VMEM((SW_BLOCK, CKV_COLS), F32),
                pltpu.VMEM((N_BRANCH, tt, BRANCH_W), BF16),
                pltpu.VMEM((tt, W_COLS - OFF_BUV), F32),
                pltpu.VMEM((tt, D_MODEL), F32),
            ],
        ),
        out_shape=jax.ShapeDtypeStruct((batch * seq, D_MODEL), F32),
        compiler_params=pltpu.CompilerParams(
            dimension_semantics=("arbitrary", "arbitrary"), vmem_limit_bytes=VMEM_LIMIT),
        name="layer",
    )(sinks, x2, norm_pre, w_pad, conv_w, gate_params, o_norm, gm_norm, spatial_w, bias_mat, mem_kv,
      w_up, w_out, norm_post)


def kernel(x, mem, norm_pre, norm_post, norm_mem, w_in, conv_w, a_log, dt_bias, dn_norm, gm_norm,
           spatial_w, spatial_b, sinks, w_mem_kv, w_up, w_out):
    batch, seq, d = x.shape
    nl = w_in.shape[0]
    assert d == D_MODEL and seq % STEP_ROWS == 0 and STEP_ROWS % TT_LAYER == 0
    assert TT_LAYER % SW_BLOCK == 0 and TT_LAYER % GM_CHUNK == 0 and TT_LAYER % DN_CHUNK == 0
    assert w_in.shape[1:] == (D_MODEL, W_COLS - PAD_ROWS) and mem.shape[0] == batch and mem.shape[2] == D_MODEL
    w_pad = _pad_w_in(jnp.swapaxes(w_in, 1, 2))
    lane_pad = ((0, 0), (DN_HEADS, LANES - 2 * DN_HEADS))
    gate_params = jnp.stack([jnp.pad(-jnp.exp(a_log), lane_pad), jnp.pad(dt_bias, lane_pad)], axis=1)
    gate_params = jnp.pad(gate_params, ((0, 0), (0, SMALL_ROWS - 2), (0, 0)))
    o_norm = jnp.broadcast_to(dn_norm[:, None, :], (nl, SMALL_ROWS, DN_DK))
    period_rows = lambda width: SMALL_ROWS * LANES // width
    npre = jnp.broadcast_to(norm_pre[:, None, :], (nl, period_rows(d), d))
    npost = jnp.broadcast_to(norm_post[:, None, :], (nl, period_rows(d), d))
    gmn = jnp.broadcast_to(gm_norm[:, None, :], (nl, period_rows(BRANCH_W), BRANCH_W))
    conv_p = jnp.pad(conv_w, ((0, 0), (0, SUBLANES - CONV_W), (0, 0)))
    bias_mat = jnp.repeat(jnp.swapaxes(spatial_b, 1, 2), GM_CHUNK, axis=2)
    mem_kv = _memkv(mem, norm_mem[:, None, :], w_mem_kv)
    w_up_b = w_up.astype(BF16)
    w_out_b = w_out.astype(BF16)

    x2 = x.reshape(batch * seq, d)
    for l in range(nl):
        x2 = _layer_call(x2, l, sinks, npre, w_pad, conv_p, gate_params, o_norm, gmn, spatial_w, bias_mat, mem_kv,
                         w_up_b, w_out_b, npost, batch=batch, seq=seq, step_rows=STEP_ROWS, tt=TT_LAYER)
    return x2.reshape(batch, seq, d)
```

```python
import collections
import functools

import jax
import jax.numpy as jnp
from jax import lax
from jax.experimental import pallas as pl
from jax.experimental.pallas import tpu as pltpu

F32 = jnp.float32
BF16 = jnp.bfloat16

D_MODEL = 1024
N_BRANCH = 4
BRANCH_W = 512
DN_HEADS = 4
DN_DK = 128
DN_CHUNK = 64
CONV_W = 4
GM_GROUPS = 4
GM_CHUNK = 128
SW_HEADS = 8
SW_KV_HEADS = 2
SW_HD = 64
SW_BLOCK = 128
XM_HEADS = 4
XM_HD = 128
EPS = 1e-6
NEG_INF = -1e30
LANES = 128
SUBLANES = 8
SMALL_ROWS = 32

CKV_COLS = 2 * SW_KV_HEADS * SW_HD
OFF_AQKV = 0
OFF_AZ = OFF_AQKV + 3 * BRANCH_W
OFF_SMALL = OFF_AZ + BRANCH_W
N_SMALL = 2 * DN_HEADS
OFF_BUV = OFF_SMALL + LANES
OFF_BZ = OFF_BUV + 2 * BRANCH_W
OFF_CQ = OFF_BZ + BRANCH_W
OFF_CKV = OFF_CQ + BRANCH_W
OFF_CZ = OFF_CKV + CKV_COLS
OFF_MQ = OFF_CZ + BRANCH_W
OFF_MZ = OFF_MQ + BRANCH_W
OFF_GATES = OFF_MZ + BRANCH_W
W_COLS = OFF_GATES + N_BRANCH * D_MODEL

VMEM_LIMIT = 60 * 1024 * 1024
VMEM_LIMIT_PREP = 48 * 1024 * 1024
MEMKV_ROWS = 512
TT_LAYER = 256
ROW_GROUPS = 2
ROW_SKEW = 40
PROJ_TILE = 256
BODY_AT = (6, 20, 30)
ITEMS_AT = {0: 2, 3: 2}


def _sigmoid(x):
    return 0.5 * jnp.tanh(0.5 * x) + 0.5


def _silu(x):
    u = 0.5 * x
    return u + u * jnp.tanh(u)


def _softplus(x):
    return jnp.maximum(x, 0.0) + jnp.log(1.0 + jnp.exp(-jnp.abs(x)))


def _rms(x, gain):
    return x * lax.rsqrt(jnp.mean(x * x, axis=-1, keepdims=True) + EPS) * gain


def _dot(a, b):
    return jnp.dot(a, b, preferred_element_type=F32)


def _dot_nt(a, b):
    return lax.dot_general(a, b, (((1,), (1,)), ((), ())), preferred_element_type=F32)


PADW_COLS = 2048
PAD_ROWS = LANES - N_SMALL


def _pad_w_kernel(prev_ref, cur_ref, o_ref):
    i = pl.program_id(1)
    small_blk = OFF_SMALL // PADW_COLS
    keep = PADW_COLS - PAD_ROWS

    @pl.when(i < small_blk)
    def _():
        o_ref[...] = cur_ref[...].T.astype(BF16)

    @pl.when(i == small_blk)
    def _():
        cur = cur_ref[...]
        src = jnp.concatenate([cur[:N_SMALL], jnp.zeros((PAD_ROWS, cur.shape[1]), F32), cur[N_SMALL:keep]], axis=0)
        o_ref[...] = src.T.astype(BF16)

    @pl.when(i > small_blk)
    def _():
        src = jnp.concatenate([prev_ref[LANES - PAD_ROWS:], cur_ref[:keep]], axis=0)
        o_ref[...] = src.T.astype(BF16)


def _pad_w_in(w_t):
    nl, n_in, d = w_t.shape
    assert n_in == W_COLS - PAD_ROWS and OFF_SMALL % PADW_COLS == 0 and OFF_BUV - OFF_SMALL == LANES
    sub = PADW_COLS // LANES
    return pl.pallas_call(
        _pad_w_kernel,
        grid=(nl, pl.cdiv(W_COLS, PADW_COLS)),
        in_specs=[
            pl.BlockSpec((None, LANES, d), lambda l, i: (l, jnp.maximum(i * sub - 1, 0), 0)),
            pl.BlockSpec((None, PADW_COLS, d), lambda l, i: (l, i, 0)),
        ],
        out_specs=pl.BlockSpec((None, d, PADW_COLS), lambda l, i: (l, 0, i)),
        out_shape=jax.ShapeDtypeStruct((nl, d, W_COLS), BF16),
        compiler_params=pltpu.CompilerParams(
            dimension_semantics=("arbitrary", "arbitrary"), vmem_limit_bytes=VMEM_LIMIT_PREP),
        name="padw",
    )(w_t, w_t)


def _memkv_kernel(m_ref, g_ref, w_ref, o_ref):
    wb = w_ref[...].astype(BF16)
    for lo in range(0, m_ref.shape[0], MEMKV_ROWS):
        o_ref[lo:lo + MEMKV_ROWS] = _dot(_rms(m_ref[lo:lo + MEMKV_ROWS], g_ref[...]).astype(BF16), wb).astype(BF16)


def _memkv(mem, gain, w):
    b, ml, _ = mem.shape
    nl = w.shape[0]
    assert (b * ml) % MEMKV_ROWS == 0
    out = pl.pallas_call(
        _memkv_kernel,
        grid=(nl,),
        in_specs=[
            pl.BlockSpec((b * ml, D_MODEL), lambda l: (0, 0)),
            pl.BlockSpec((None, 1, D_MODEL), lambda l: (l, 0, 0)),
            pl.BlockSpec((None, D_MODEL, 2 * BRANCH_W), lambda l: (l, 0, 0)),
        ],
        out_specs=pl.BlockSpec((None, b * ml, 2 * BRANCH_W), lambda l: (l, 0, 0)),
        out_shape=jax.ShapeDtypeStruct((nl, b * ml, 2 * BRANCH_W), BF16),
        compiler_params=pltpu.CompilerParams(
            dimension_semantics=("arbitrary",), vmem_limit_bytes=VMEM_LIMIT_PREP),
        name="memkv",
    )(mem.reshape(b * ml, D_MODEL), gain, w)
    return out.reshape(nl, b, ml, 2 * BRANCH_W)


def _deltanet_stages(qkv, z, sm, cw_ref, gp_ref, on_ref, y_ref, tail_ref, s_ref, *, tt):
    H, C, DK = DN_HEADS, DN_CHUNK, DN_DK
    HC = H * C
    cw = cw_ref[0:CONV_W, :]

    def conv_silu(idx):
        lo = idx * BRANCH_W
        cur = qkv[:, lo:lo + BRANCH_W]
        ext = jnp.concatenate([tail_ref[idx], cur], axis=0)
        acc = cur * cw[CONV_W - 1:CONV_W, lo:lo + BRANCH_W]
        for j in range(1, CONV_W):
            shifted = pltpu.roll(ext, j, axis=0)[SUBLANES:]
            acc = acc + shifted * cw[CONV_W - 1 - j:CONV_W - j, lo:lo + BRANCH_W]
        tail_ref[idx] = cur[tt - SUBLANES:]
        return _silu(acc)

    qc = conv_silu(0)
    yield
    kc = conv_silu(1)
    yield
    vc = conv_silu(2)
    yield

    def l2n(xh):
        return xh * lax.rsqrt(jnp.sum(xh * xh, axis=-1, keepdims=True) + EPS)

    ci = lax.broadcasted_iota(jnp.int32, (C, HC), 0)
    cl = lax.broadcasted_iota(jnp.int32, (C, HC), 1)
    cj = cl % C
    strict = ci > cj
    incl = ci >= cj
    eye_cat = (ci == cj).astype(F32)
    br = lax.broadcasted_iota(jnp.int32, (HC, HC), 0)
    bc = lax.broadcasted_iota(jnp.int32, (HC, HC), 1)
    bd_mask = jnp.where((br // C) == (bc // C), 1.0, 0.0).astype(BF16)
    kl = lax.broadcasted_iota(jnp.int32, (DK, HC), 1) // C
    head_lane_masks = [jnp.where(kl == h, 1.0, 0.0).astype(BF16) for h in range(H)]
    kr = lax.broadcasted_iota(jnp.int32, (HC, H * DK), 0) // C
    kc_ = lax.broadcasted_iota(jnp.int32, (HC, H * DK), 1) // DK
    kbd_mask = jnp.where(kr == kc_, 1.0, 0.0).astype(BF16)
    l64 = lax.broadcasted_iota(jnp.int32, (C, DK), 1) < C
    tr = lax.broadcasted_iota(jnp.int32, (C, C), 0)
    tc = lax.broadcasted_iota(jnp.int32, (C, C), 1)
    ltri = jnp.where(tr >= tc, 1.0, 0.0).astype(BF16)

    neg_a = gp_ref[0:1, :]
    dt_b = gp_ref[1:2, :]
    o_gain = on_ref[0:1, :]

    def stack(x, lo):
        return jnp.concatenate([x[lo:lo + C, h * DK:(h + 1) * DK] for h in range(H)], axis=0)

    def block_diag(x_cat):
        return jnp.concatenate([x_cat.astype(BF16)] * H, axis=0) * bd_mask

    def unstack(x_st):
        return jnp.concatenate([x_st[h * C:(h + 1) * C] for h in range(H)], axis=1)

    chunks = range(tt // C)
    qs, ks, vs, beta_all, gc3 = [], [], [], [], []
    for c in chunks:
        lo = c * C
        qs.append(jnp.concatenate([l2n(qc[lo:lo + C, h * DK:(h + 1) * DK]) for h in range(H)], axis=0) * (DK ** -0.5))
        ks.append(jnp.concatenate([l2n(kc[lo:lo + C, h * DK:(h + 1) * DK]) for h in range(H)], axis=0))
        vs.append(stack(vc, lo))
        smc = sm[lo:lo + C, :]
        beta_all.append(_sigmoid(smc))
        g_all = neg_a * _softplus(smc + dt_b)
        g_hi = g_all.astype(BF16)
        r1 = g_all - g_hi.astype(F32)
        g_mid = r1.astype(BF16)
        g_lo = (r1 - g_mid.astype(F32)).astype(BF16)
        gc3.append(_dot(ltri, jnp.concatenate([g_hi, g_mid, g_lo], axis=1)))
        yield

    cb, bb, decay, egc, kb, kq = [], [], [], [], [], []
    for c in chunks:
        gc_all = gc3[c][:, :LANES] + gc3[c][:, LANES:2 * LANES] + gc3[c][:, 2 * LANES:]
        gcb = [jnp.broadcast_to(gc_all[:, H + h:H + h + 1], (C, DK)) for h in range(H)]
        cb.append(jnp.concatenate(gcb, axis=0))
        bb.append(jnp.concatenate(
            [jnp.broadcast_to(beta_all[c][:, h:h + 1], (C, DK)) for h in range(H)], axis=0))
        c_cat = jnp.concatenate([jnp.where(l64, gcb[0], gcb[1]), jnp.where(l64, gcb[2], gcb[3])], axis=1)
        gct = jnp.concatenate([gc_all, gc_all], axis=0).T
        r_cat = jnp.concatenate(
            [jnp.where(l64[0:1], gct[H:H + 1], gct[H + 1:H + 2]),
             jnp.where(l64[0:1], gct[H + 2:H + 3], gct[H + 3:H + 4])], axis=1)
        decay.append(jnp.exp(jnp.where(incl, c_cat - r_cat, 0.0)))
        egc.append(jnp.exp(cb[c]))
        kb.append(ks[c] * bb[c])
        k_bd = jnp.concatenate([ks[c].astype(BF16)] * H, axis=1) * kbd_mask
        kq.append(_dot_nt(jnp.concatenate([unstack(kb[c]), unstack(qs[c])], axis=0).astype(BF16), k_bd))
        yield

    a_cat, p_cat, cur = [], [], []
    for c in chunks:
        n_cat = -jnp.where(strict, kq[c][:C] * decay[c], 0.0)
        a_cat.append(jnp.where(incl, kq[c][C:] * decay[c], 0.0))
        p_cat.append(eye_cat + n_cat)
        cur.append(_dot(n_cat.astype(BF16), block_diag(n_cat)))
        yield
    for _ in range(4):
        for c in chunks:
            out = _dot(jnp.concatenate([cur[c], p_cat[c]], axis=0).astype(BF16), block_diag(cur[c]))
            p_cat[c] = p_cat[c] + out[C:]
            cur[c] = out[:C]
            yield
    t_cat = [p_cat[c] + _dot(p_cat[c].astype(BF16), block_diag(cur[c])) for c in chunks]
    yield
    solb = []
    for c in chunks:
        rhs = jnp.concatenate([vs[c] * bb[c], kb[c] * egc[c]], axis=1)
        solb.append(_dot(block_diag(t_cat[c]), rhs.astype(BF16)).astype(BF16))
        yield
    auw, qp, gb = [], [], []
    for c in chunks:
        auw.append(_dot(block_diag(a_cat[c]), solb[c]))
        qp.append(qs[c] * egc[c] - auw[c][:, DK:])
        glast = jnp.concatenate(
            [jnp.broadcast_to(cb[c][h * C + C - 1:h * C + C, :], (C, DK)) for h in range(H)], axis=0)
        kd_t = (ks[c] * jnp.exp(glast - cb[c])).T.astype(BF16)
        gb.append([_dot(kd_t * head_lane_masks[h], solb[c]) for h in range(H)])
        yield

    for c in chunks:
        lo = c * C
        for h in range(H):
            s_h = s_ref[:, h * DK:(h + 1) * DK]
            gq = _dot(jnp.concatenate([gb[c][h][:, DK:], qp[c][h * C:(h + 1) * C]], axis=0).astype(BF16),
                      s_h.astype(BF16))
            d_h = jnp.exp(cb[c][h * C + C - 1:h * C + C, :])
            s_ref[:, h * DK:(h + 1) * DK] = s_h * d_h - gq[:DK] + gb[c][h][:, :DK]
            oh = _rms(gq[DK:] + auw[c][h * C:(h + 1) * C, :DK], o_gain)
            zh = z[lo:lo + C, h * DK:(h + 1) * DK]
            y_ref[lo:lo + C, h * DK:(h + 1) * DK] = (oh * _silu(zh)).astype(BF16)
        yield


def _gmlp_branch(uv, z, gain_ref, ws_ref, bias_ref, y_ref, tick, *, tt):
    P, G = GM_CHUNK, GM_GROUPS
    u = jax.nn.gelu(uv[:, :BRANCH_W])
    tick()
    v = _rms(jax.nn.gelu(uv[:, BRANCH_W:]), gain_ref[0:1, :])
    tick()
    vb = v.astype(BF16)
    tr = lax.broadcasted_iota(jnp.int32, (P, P), 0)
    tc = lax.broadcasted_iota(jnp.int32, (P, P), 1)
    causal = tr >= tc
    bias = bias_ref[...]
    for g in range(G):
        wg = jnp.where(causal, ws_ref[g], 0.0).astype(BF16)
        for ci in range(tt // P):
            lo = ci * P
            s = _dot(wg, vb[lo:lo + P, g * P:(g + 1) * P]) + bias[:, g * P:(g + 1) * P]
            zg = z[lo:lo + P, g * P:(g + 1) * P]
            y_ref[lo:lo + P, g * P:(g + 1) * P] = (u[lo:lo + P, g * P:(g + 1) * P] * s * _silu(zg)).astype(BF16)
        tick()


def _swa_branch(sink_of, q, z, kv_cur, y_ref, kvprev_ref, tick, has_prev, *, tt):
    P, HD, KV = SW_BLOCK, SW_HD, SW_KV_HEADS
    G = SW_HEADS // KV
    kj = lax.broadcasted_iota(jnp.int32, (2 * P, P), 0)
    qi = lax.broadcasted_iota(jnp.int32, (2 * P, P), 1)
    dist = qi + P - kj
    window = jnp.logical_and(dist >= 0, dist < P)
    valid_mid = jnp.concatenate([window] * G, axis=1)
    valid_first = jnp.concatenate([jnp.logical_and(window, jnp.logical_or(kj >= P, has_prev))] * G, axis=1)
    kv = jnp.concatenate([kvprev_ref[...], kv_cur], axis=0)
    kvprev_ref[...] = kv_cur[tt - P:, :]
    qb = (q * (HD ** -0.5)).astype(BF16)
    for kh in range(KV):
        sink = jnp.concatenate([jnp.full((1, P), sink_of(kh * G + g), F32) for g in range(G)], axis=1)
        k_all = kv[:, kh * HD:(kh + 1) * HD].astype(BF16)
        v_all_t = kv[:, (KV + kh) * HD:(KV + kh + 1) * HD].T.astype(BF16)
        for j in range(tt // P):
            lo = j * P
            qs = jnp.concatenate(
                [qb[lo:lo + P, (kh * G + g) * HD:(kh * G + g + 1) * HD] for g in range(G)], axis=0)
            s = jnp.where(valid_first if j == 0 else valid_mid, _dot_nt(k_all[lo:lo + 2 * P], qs), NEG_INF)
            mx = jnp.maximum(jnp.max(s, axis=0, keepdims=True), sink)
            e = jnp.exp(s - mx)
            den = jnp.sum(e, axis=0, keepdims=True) + jnp.exp(sink - mx)
            o_t = _dot(v_all_t[:, lo:lo + 2 * P], e.astype(BF16)) * (1.0 / den)
            for g2 in range(G // 2):
                h0 = kh * G + 2 * g2
                pair = jnp.concatenate(
                    [o_t[:, 2 * g2 * P:(2 * g2 + 1) * P], o_t[:, (2 * g2 + 1) * P:(2 * g2 + 2) * P]], axis=0).T
                zp = z[lo:lo + P, h0 * HD:(h0 + 2) * HD]
                y_ref[lo:lo + P, h0 * HD:(h0 + 2) * HD] = (pair * _silu(zp)).astype(BF16)
            tick()


def _memattn_branch(q, z, kv_ref, y_ref, tick):
    HD = XM_HD
    qb = q.astype(BF16)
    for h in range(XM_HEADS):
        mk = kv_ref[:, h * HD:(h + 1) * HD]
        mv = kv_ref[:, BRANCH_W + h * HD:BRANCH_W + (h + 1) * HD]
        s = _dot_nt(qb[:, h * HD:(h + 1) * HD], mk) * (HD ** -0.5)
        e = jnp.exp(s - jnp.max(s, axis=-1, keepdims=True))
        o = _dot(e.astype(BF16), mv) * (1.0 / jnp.sum(e, axis=-1, keepdims=True))
        y_ref[:, h * HD:(h + 1) * HD] = (o * _silu(z[:, h * HD:(h + 1) * HD])).astype(BF16)
        tick()


def _layer_kernel(sinks_ref, x_ref, npre_ref, w_ref, cw_ref, gp_ref, on_ref, gmn_ref, ws_ref, bias_ref,
                  mkv_ref, wup_ref, wout_ref, npost_ref, o_ref, tail_ref, s_ref, kvprev_ref, y_ref, cols_ref,
                  merged_ref, *, tt, layer):
    @pl.when(pl.program_id(1) == 0)
    def _():
        tail_ref[...] = jnp.zeros_like(tail_ref)
        s_ref[...] = jnp.zeros_like(s_ref)
        kvprev_ref[...] = jnp.zeros_like(kvprev_ref)

    has_prev = pl.program_id(1) > 0
    tiles = [_layer_tile(sinks_ref, x_ref.at[g], npre_ref, w_ref, cw_ref, gp_ref, on_ref, gmn_ref, ws_ref, bias_ref,
                         mkv_ref.at[g], wup_ref, wout_ref, npost_ref, o_ref.at[g], tail_ref.at[g], s_ref.at[g],
                         kvprev_ref.at[g], y_ref.at[g], cols_ref.at[g], merged_ref.at[g], has_prev, tt=tt, layer=layer)
             for g in range(ROW_GROUPS)]
    alive = [True] * ROW_GROUPS
    step = 0
    while any(alive):
        for g, tile in enumerate(tiles):
            if alive[g] and step >= g * ROW_SKEW:
                alive[g] = next(tile, None) is not None
        step += 1


def _layer_tile(sinks_ref, x_ref, npre_ref, w_ref, cw_ref, gp_ref, on_ref, gmn_ref, ws_ref, bias_ref,
                mkv_ref, wup_ref, wout_ref, npost_ref, o_ref, tail_ref, s_ref, kvprev_ref, y_ref, cols_ref,
                merged_ref, has_prev, *, tt, layer):
    x = x_ref[...]
    h = _rms(x, npre_ref[0:1, :]).astype(BF16)

    def proj(lo, width):
        return _dot(h, w_ref[:, lo:lo + width])

    filled = [OFF_BUV]
    merged_started = set()

    def proj_item(lo):
        def run():
            cols_ref[:, lo - OFF_BUV:lo - OFF_BUV + PROJ_TILE] = proj(lo, PROJ_TILE)
            filled[0] = lo + PROJ_TILE
        return run

    def col(lo, width):
        assert lo + width <= filled[0], "projection read before it was issued"
        return cols_ref[:, lo - OFF_BUV:lo - OFF_BUV + width]

    def merge_item(n, lo):
        def run():
            g = _sigmoid(col(OFF_GATES + n * D_MODEL + lo, PROJ_TILE))
            term = g * _dot(y_ref[n], wup_ref[n, :, lo:lo + PROJ_TILE])
            if lo in merged_started:
                merged_ref[:, lo:lo + PROJ_TILE] += term
            else:
                merged_ref[:, lo:lo + PROJ_TILE] = term
                merged_started.add(lo)
        return run

    queue = collections.deque(proj_item(lo) for lo in range(OFF_BUV, W_COLS, PROJ_TILE))

    def tick(count=1):
        for _ in range(count):
            if queue:
                queue.popleft()()

    def branch_done(n):
        queue.extend(merge_item(n, lo) for lo in range(0, D_MODEL, PROJ_TILE))

    def branch_b():
        _gmlp_branch(col(OFF_BUV, 2 * BRANCH_W), col(OFF_BZ, BRANCH_W), gmn_ref, ws_ref, bias_ref,
                     y_ref.at[1], tick, tt=tt)
        branch_done(1)

    def branch_c():
        _swa_branch(lambda hd: sinks_ref[layer, hd], col(OFF_CQ, BRANCH_W), col(OFF_CZ, BRANCH_W),
                    col(OFF_CKV, CKV_COLS), y_ref.at[2], kvprev_ref, tick, has_prev, tt=tt)
        branch_done(2)

    def branch_m():
        _memattn_branch(col(OFF_MQ, BRANCH_W), col(OFF_MZ, BRANCH_W), mkv_ref, y_ref.at[3], tick)
        branch_done(3)

    bodies = {BODY_AT[0]: branch_b, BODY_AT[1]: branch_c, BODY_AT[2]: branch_m}
    stages = _deltanet_stages(proj(OFF_AQKV, 3 * BRANCH_W), proj(OFF_AZ, BRANCH_W), proj(OFF_SMALL, LANES),
                              cw_ref, gp_ref, on_ref, y_ref.at[0], tail_ref, s_ref, tt=tt)
    yield True
    for i, _ in enumerate(stages):
        tick(ITEMS_AT.get(i, 1))
        if i in bodies:
            bodies.pop(i)()
        yield True
    assert not bodies, "stage list shorter than BODY_AT"
    branch_done(0)
    while queue:
        tick()
    yield True

    out = _dot(merged_ref[...].astype(BF16), wout_ref[...])
    o_ref[...] = x + _rms(out, npost_ref[0:1, :])


def _layer_call(x2, layer, sinks, norm_pre, w_pad, conv_w, gate_params, o_norm, gm_norm, spatial_w, bias_mat,
                mem_kv, w_up, w_out, norm_post, *, batch, seq, tt):
    nt = seq // tt
    rows_per_group = batch // ROW_GROUPS
    ml = mem_kv.shape[2]
    x3 = x2.reshape(ROW_GROUPS, rows_per_group * seq, D_MODEL)
    mem_kv = mem_kv.reshape(mem_kv.shape[0], ROW_GROUPS, rows_per_group, ml, 2 * BRANCH_W)
    row = lambda b, t, s: (0, b * nt + t, 0)

    def const(arr):
        shape = arr.shape[1:]
        return pl.BlockSpec((None,) + shape, lambda b, t, s: (layer,) + (0,) * len(shape),
                            pipeline_mode=pl.Buffered(1))

    in_specs = [
        pl.BlockSpec((ROW_GROUPS, tt, D_MODEL), row),
        const(norm_pre), const(w_pad), const(conv_w), const(gate_params), const(o_norm), const(gm_norm),
        const(spatial_w), const(bias_mat),
        pl.BlockSpec((None, ROW_GROUPS, None, ml, 2 * BRANCH_W), lambda b, t, s: (layer, 0, b, 0, 0)),
        const(w_up), const(w_out), const(norm_post),
    ]
    out = pl.pallas_call(
        functools.partial(_layer_kernel, tt=tt, layer=layer),
        grid_spec=pltpu.PrefetchScalarGridSpec(
            num_scalar_prefetch=1,
            grid=(rows_per_group, nt),
            in_specs=in_specs,
            out_specs=pl.BlockSpec((ROW_GROUPS, tt, D_MODEL), row),
            scratch_shapes=[
                pltpu.VMEM((ROW_GROUPS, 3, SUBLANES, BRANCH_W), F32),
                pltpu.VMEM((ROW_GROUPS, DN_DK, DN_HEADS * DN_DK), F32),
                pltpu.VMEM((ROW_GROUPS, SW_BLOCK, CKV_COLS), F32),
                pltpu.VMEM((ROW_GROUPS, N_BRANCH, tt, BRANCH_W), BF16),
                pltpu.VMEM((ROW_GROUPS, tt, W_COLS - OFF_BUV), F32),
                pltpu.VMEM((ROW_GROUPS, tt, D_MODEL), F32),
            ],
        ),
        out_shape=jax.ShapeDtypeStruct(x3.shape, F32),
        compiler_params=pltpu.CompilerParams(
            dimension_semantics=("arbitrary", "arbitrary"), vmem_limit_bytes=VMEM_LIMIT),
        name="layer",
    )(sinks, x3, norm_pre, w_pad, conv_w, gate_params, o_norm, gm_norm, spatial_w, bias_mat, mem_kv,
      w_up, w_out, norm_post)
    return out.reshape(batch * seq, D_MODEL)


def kernel(x, mem, norm_pre, norm_post, norm_mem, w_in, conv_w, a_log, dt_bias, dn_norm, gm_norm,
           spatial_w, spatial_b, sinks, w_mem_kv, w_up, w_out):
    batch, seq, d = x.shape
    nl = w_in.shape[0]
    assert d == D_MODEL and seq % TT_LAYER == 0 and batch % ROW_GROUPS == 0
    assert TT_LAYER % SW_BLOCK == 0 and TT_LAYER % GM_CHUNK == 0 and TT_LAYER % DN_CHUNK == 0
    assert w_in.shape[1:] == (D_MODEL, W_COLS - PAD_ROWS) and mem.shape[0] == batch and mem.shape[2] == D_MODEL
    w_pad = _pad_w_in(jnp.swapaxes(w_in, 1, 2))
    lane_pad = ((0, 0), (DN_HEADS, LANES - 2 * DN_HEADS))
    gate_params = jnp.stack([jnp.pad(-jnp.exp(a_log), lane_pad), jnp.pad(dt_bias, lane_pad)], axis=1)
    gate_params = jnp.pad(gate_params, ((0, 0), (0, SMALL_ROWS - 2), (0, 0)))
    o_norm = jnp.broadcast_to(dn_norm[:, None, :], (nl, SMALL_ROWS, DN_DK))
    period_rows = lambda width: SMALL_ROWS * LANES // width
    npre = jnp.broadcast_to(norm_pre[:, None, :], (nl, period_rows(d), d))
    npost = jnp.broadcast_to(norm_post[:, None, :], (nl, period_rows(d), d))
    gmn = jnp.broadcast_to(gm_norm[:, None, :], (nl, period_rows(BRANCH_W), BRANCH_W))
    conv_p = jnp.pad(conv_w, ((0, 0), (0, SUBLANES - CONV_W), (0, 0)))
    bias_mat = jnp.repeat(jnp.swapaxes(spatial_b, 1, 2), GM_CHUNK, axis=2)
    mem_kv = _memkv(mem, norm_mem[:, None, :], w_mem_kv)
    w_up_b = w_up.astype(BF16)
    w_out_b = w_out.astype(BF16)

    x2 = x.reshape(batch * seq, d)
    for l in range(nl):
        x2 = _layer_call(x2, l, sinks, npre, w_pad, conv_p, gate_params, o_norm, gmn, spatial_w, bias_mat, mem_kv,
                         w_up_b, w_out_b, npost, batch=batch, seq=seq, tt=TT_LAYER)
    return x2.reshape(batch, seq, d)
```

```python
import collections
import functools

import jax
import jax.numpy as jnp
from jax import lax
from jax.experimental import pallas as pl
from jax.experimental.pallas import tpu as pltpu

F32 = jnp.float32
BF16 = jnp.bfloat16

D_MODEL = 1024
N_BRANCH = 4
BRANCH_W = 512
DN_HEADS = 4
DN_DK = 128
DN_CHUNK = 64
CONV_W = 4
GM_GROUPS = 4
GM_CHUNK = 128
SW_HEADS = 8
SW_KV_HEADS = 2
SW_HD = 64
SW_BLOCK = 128
XM_HEADS = 4
XM_HD = 128
EPS = 1e-6
NEG_INF = -1e30
LANES = 128
SUBLANES = 8
SMALL_ROWS = 32

CKV_COLS = 2 * SW_KV_HEADS * SW_HD
OFF_AQKV = 0
OFF_AZ = OFF_AQKV + 3 * BRANCH_W
OFF_SMALL = OFF_AZ + BRANCH_W
N_SMALL = 2 * DN_HEADS
OFF_BUV = OFF_SMALL + LANES
OFF_BZ = OFF_BUV + 2 * BRANCH_W
OFF_CQ = OFF_BZ + BRANCH_W
OFF_CKV = OFF_CQ + BRANCH_W
OFF_CZ = OFF_CKV + CKV_COLS
OFF_MQ = OFF_CZ + BRANCH_W
OFF_MZ = OFF_MQ + BRANCH_W
OFF_GATES = OFF_MZ + BRANCH_W
W_COLS = OFF_GATES + N_BRANCH * D_MODEL

VMEM_LIMIT = 60 * 1024 * 1024
VMEM_LIMIT_PREP = 48 * 1024 * 1024
MEMKV_ROWS = 512
TT_LAYER = 256
ROW_GROUPS = 2
ROW_SKEW = 0
GROUP_BODY_SHIFT = -2
PROJ_TILE = 256
BODY_AT = (6, 20, 30)
ITEMS_AT = {0: 2, 3: 2}


def _sigmoid(x):
    return 0.5 * jnp.tanh(0.5 * x) + 0.5


def _silu(x):
    u = 0.5 * x
    return u + u * jnp.tanh(u)


def _softplus(x):
    return jnp.maximum(x, 0.0) + jnp.log(1.0 + jnp.exp(-jnp.abs(x)))


def _rms(x, gain):
    return x * lax.rsqrt(jnp.mean(x * x, axis=-1, keepdims=True) + EPS) * gain


def _dot(a, b):
    return jnp.dot(a, b, preferred_element_type=F32)


def _dot_nt(a, b):
    return lax.dot_general(a, b, (((1,), (1,)), ((), ())), preferred_element_type=F32)


PADW_COLS = 2048
PAD_ROWS = LANES - N_SMALL


def _pad_w_kernel(prev_ref, cur_ref, o_ref):
    i = pl.program_id(1)
    small_blk = OFF_SMALL // PADW_COLS
    keep = PADW_COLS - PAD_ROWS

    @pl.when(i < small_blk)
    def _():
        o_ref[...] = cur_ref[...].T.astype(BF16)

    @pl.when(i == small_blk)
    def _():
        cur = cur_ref[...]
        src = jnp.concatenate([cur[:N_SMALL], jnp.zeros((PAD_ROWS, cur.shape[1]), F32), cur[N_SMALL:keep]], axis=0)
        o_ref[...] = src.T.astype(BF16)

    @pl.when(i > small_blk)
    def _():
        src = jnp.concatenate([prev_ref[LANES - PAD_ROWS:], cur_ref[:keep]], axis=0)
        o_ref[...] = src.T.astype(BF16)


def _pad_w_in(w_t):
    nl, n_in, d = w_t.shape
    assert n_in == W_COLS - PAD_ROWS and OFF_SMALL % PADW_COLS == 0 and OFF_BUV - OFF_SMALL == LANES
    sub = PADW_COLS // LANES
    return pl.pallas_call(
        _pad_w_kernel,
        grid=(nl, pl.cdiv(W_COLS, PADW_COLS)),
        in_specs=[
            pl.BlockSpec((None, LANES, d), lambda l, i: (l, jnp.maximum(i * sub - 1, 0), 0)),
            pl.BlockSpec((None, PADW_COLS, d), lambda l, i: (l, i, 0)),
        ],
        out_specs=pl.BlockSpec((None, d, PADW_COLS), lambda l, i: (l, 0, i)),
        out_shape=jax.ShapeDtypeStruct((nl, d, W_COLS), BF16),
        compiler_params=pltpu.CompilerParams(
            dimension_semantics=("arbitrary", "arbitrary"), vmem_limit_bytes=VMEM_LIMIT_PREP),
        name="padw",
    )(w_t, w_t)


def _memkv_kernel(m_ref, g_ref, w_ref, o_ref):
    wb = w_ref[...].astype(BF16)
    for lo in range(0, m_ref.shape[0], MEMKV_ROWS):
        o_ref[lo:lo + MEMKV_ROWS] = _dot(_rms(m_ref[lo:lo + MEMKV_ROWS], g_ref[...]).astype(BF16), wb).astype(BF16)


def _memkv(mem, gain, w):
    b, ml, _ = mem.shape
    nl = w.shape[0]
    assert (b * ml) % MEMKV_ROWS == 0
    out = pl.pallas_call(
        _memkv_kernel,
        grid=(nl,),
        in_specs=[
            pl.BlockSpec((b * ml, D_MODEL), lambda l: (0, 0)),
            pl.BlockSpec((None, 1, D_MODEL), lambda l: (l, 0, 0)),
            pl.BlockSpec((None, D_MODEL, 2 * BRANCH_W), lambda l: (l, 0, 0)),
        ],
        out_specs=pl.BlockSpec((None, b * ml, 2 * BRANCH_W), lambda l: (l, 0, 0)),
        out_shape=jax.ShapeDtypeStruct((nl, b * ml, 2 * BRANCH_W), BF16),
        compiler_params=pltpu.CompilerParams(
            dimension_semantics=("arbitrary",), vmem_limit_bytes=VMEM_LIMIT_PREP),
        name="memkv",
    )(mem.reshape(b * ml, D_MODEL), gain, w)
    return out.reshape(nl, b, ml, 2 * BRANCH_W)


def _deltanet_stages(qkv, z, sm, cw_ref, gp_ref, on_ref, y_ref, tail_ref, s_ref, *, tt):
    H, C, DK = DN_HEADS, DN_CHUNK, DN_DK
    HC = H * C
    cw = cw_ref[0:CONV_W, :]

    def conv_silu(idx):
        lo = idx * BRANCH_W
        cur = qkv[:, lo:lo + BRANCH_W]
        ext = jnp.concatenate([tail_ref[idx], cur], axis=0)
        acc = cur * cw[CONV_W - 1:CONV_W, lo:lo + BRANCH_W]
        for j in range(1, CONV_W):
            shifted = pltpu.roll(ext, j, axis=0)[SUBLANES:]
            acc = acc + shifted * cw[CONV_W - 1 - j:CONV_W - j, lo:lo + BRANCH_W]
        tail_ref[idx] = cur[tt - SUBLANES:]
        return _silu(acc)

    qc = conv_silu(0)
    yield
    kc = conv_silu(1)
    yield
    vc = conv_silu(2)
    yield

    def l2n(xh):
        return xh * lax.rsqrt(jnp.sum(xh * xh, axis=-1, keepdims=True) + EPS)

    ci = lax.broadcasted_iota(jnp.int32, (C, HC), 0)
    cl = lax.broadcasted_iota(jnp.int32, (C, HC), 1)
    cj = cl % C
    strict = ci > cj
    incl = ci >= cj
    eye_cat = (ci == cj).astype(F32)
    br = lax.broadcasted_iota(jnp.int32, (HC, HC), 0)
    bc = lax.broadcasted_iota(jnp.int32, (HC, HC), 1)
    bd_mask = jnp.where((br // C) == (bc // C), 1.0, 0.0).astype(BF16)
    kl = lax.broadcasted_iota(jnp.int32, (DK, HC), 1) // C
    head_lane_masks = [jnp.where(kl == h, 1.0, 0.0).astype(BF16) for h in range(H)]
    kr = lax.broadcasted_iota(jnp.int32, (HC, H * DK), 0) // C
    kc_ = lax.broadcasted_iota(jnp.int32, (HC, H * DK), 1) // DK
    kbd_mask = jnp.where(kr == kc_, 1.0, 0.0).astype(BF16)
    l64 = lax.broadcasted_iota(jnp.int32, (C, DK), 1) < C
    tr = lax.broadcasted_iota(jnp.int32, (C, C), 0)
    tc = lax.broadcasted_iota(jnp.int32, (C, C), 1)
    ltri = jnp.where(tr >= tc, 1.0, 0.0).astype(BF16)

    neg_a = gp_ref[0:1, :]
    dt_b = gp_ref[1:2, :]
    o_gain = on_ref[0:1, :]

    def stack(x, lo):
        return jnp.concatenate([x[lo:lo + C, h * DK:(h + 1) * DK] for h in range(H)], axis=0)

    def block_diag(x_cat):
        return jnp.concatenate([x_cat.astype(BF16)] * H, axis=0) * bd_mask

    def unstack(x_st):
        return jnp.concatenate([x_st[h * C:(h + 1) * C] for h in range(H)], axis=1)

    chunks = range(tt // C)
    qs, ks, vs, beta_all, gc3 = [], [], [], [], []
    for c in chunks:
        lo = c * C
        qs.append(jnp.concatenate([l2n(qc[lo:lo + C, h * DK:(h + 1) * DK]) for h in range(H)], axis=0) * (DK ** -0.5))
        ks.append(jnp.concatenate([l2n(kc[lo:lo + C, h * DK:(h + 1) * DK]) for h in range(H)], axis=0))
        vs.append(stack(vc, lo))
        smc = sm[lo:lo + C, :]
        beta_all.append(_sigmoid(smc))
        g_all = neg_a * _softplus(smc + dt_b)
        g_hi = g_all.astype(BF16)
        r1 = g_all - g_hi.astype(F32)
        g_mid = r1.astype(BF16)
        g_lo = (r1 - g_mid.astype(F32)).astype(BF16)
        gc3.append(_dot(ltri, jnp.concatenate([g_hi, g_mid, g_lo], axis=1)))
        yield

    cb, bb, decay, egc, kb, kq = [], [], [], [], [], []
    for c in chunks:
        gc_all = gc3[c][:, :LANES] + gc3[c][:, LANES:2 * LANES] + gc3[c][:, 2 * LANES:]
        gcb = [jnp.broadcast_to(gc_all[:, H + h:H + h + 1], (C, DK)) for h in range(H)]
        cb.append(jnp.concatenate(gcb, axis=0))
        bb.append(jnp.concatenate(
            [jnp.broadcast_to(beta_all[c][:, h:h + 1], (C, DK)) for h in range(H)], axis=0))
        c_cat = jnp.concatenate([jnp.where(l64, gcb[0], gcb[1]), jnp.where(l64, gcb[2], gcb[3])], axis=1)
        gct = jnp.concatenate([gc_all, gc_all], axis=0).T
        r_cat = jnp.concatenate(
            [jnp.where(l64[0:1], gct[H:H + 1], gct[H + 1:H + 2]),
             jnp.where(l64[0:1], gct[H + 2:H + 3], gct[H + 3:H + 4])], axis=1)
        decay.append(jnp.exp(jnp.where(incl, c_cat - r_cat, 0.0)))
        egc.append(jnp.exp(cb[c]))
        kb.append(ks[c] * bb[c])
        k_bd = jnp.concatenate([ks[c].astype(BF16)] * H, axis=1) * kbd_mask
        kq.append(_dot_nt(jnp.concatenate([unstack(kb[c]), unstack(qs[c])], axis=0).astype(BF16), k_bd))
        yield

    a_cat, p_cat, cur = [], [], []
    for c in chunks:
        n_cat = -jnp.where(strict, kq[c][:C] * decay[c], 0.0)
        a_cat.append(jnp.where(incl, kq[c][C:] * decay[c], 0.0))
        p_cat.append(eye_cat + n_cat)
        cur.append(_dot(n_cat.astype(BF16), block_diag(n_cat)))
        yield
    for _ in range(4):
        for c in chunks:
            out = _dot(jnp.concatenate([cur[c], p_cat[c]], axis=0).astype(BF16), block_diag(cur[c]))
            p_cat[c] = p_cat[c] + out[C:]
            cur[c] = out[:C]
            yield
    t_cat = [p_cat[c] + _dot(p_cat[c].astype(BF16), block_diag(cur[c])) for c in chunks]
    yield
    solb = []
    for c in chunks:
        rhs = jnp.concatenate([vs[c] * bb[c], kb[c] * egc[c]], axis=1)
        solb.append(_dot(block_diag(t_cat[c]), rhs.astype(BF16)).astype(BF16))
        yield
    auw, qp, gb = [], [], []
    for c in chunks:
        auw.append(_dot(block_diag(a_cat[c]), solb[c]))
        qp.append(qs[c] * egc[c] - auw[c][:, DK:])
        glast = jnp.concatenate(
            [jnp.broadcast_to(cb[c][h * C + C - 1:h * C + C, :], (C, DK)) for h in range(H)], axis=0)
        kd_t = (ks[c] * jnp.exp(glast - cb[c])).T.astype(BF16)
        gb.append([_dot(kd_t * head_lane_masks[h], solb[c]) for h in range(H)])
        yield

    for c in chunks:
        lo = c * C
        for h in range(H):
            s_h = s_ref[:, h * DK:(h + 1) * DK]
            gq = _dot(jnp.concatenate([gb[c][h][:, DK:], qp[c][h * C:(h + 1) * C]], axis=0).astype(BF16),
                      s_h.astype(BF16))
            d_h = jnp.exp(cb[c][h * C + C - 1:h * C + C, :])
            s_ref[:, h * DK:(h + 1) * DK] = s_h * d_h - gq[:DK] + gb[c][h][:, :DK]
            oh = _rms(gq[DK:] + auw[c][h * C:(h + 1) * C, :DK], o_gain)
            zh = z[lo:lo + C, h * DK:(h + 1) * DK]
            y_ref[lo:lo + C, h * DK:(h + 1) * DK] = (oh * _silu(zh)).astype(BF16)
        yield


def _gmlp_branch(uv, z, gain_ref, ws_ref, bias_ref, y_ref, tick, *, tt):
    P, G = GM_CHUNK, GM_GROUPS
    u = jax.nn.gelu(uv[:, :BRANCH_W])
    tick()
    v = _rms(jax.nn.gelu(uv[:, BRANCH_W:]), gain_ref[0:1, :])
    tick()
    vb = v.astype(BF16)
    tr = lax.broadcasted_iota(jnp.int32, (P, P), 0)
    tc = lax.broadcasted_iota(jnp.int32, (P, P), 1)
    causal = tr >= tc
    bias = bias_ref[...]
    for g in range(G):
        wg = jnp.where(causal, ws_ref[g], 0.0).astype(BF16)
        for ci in range(tt // P):
            lo = ci * P
            s = _dot(wg, vb[lo:lo + P, g * P:(g + 1) * P]) + bias[:, g * P:(g + 1) * P]
            zg = z[lo:lo + P, g * P:(g + 1) * P]
            y_ref[lo:lo + P, g * P:(g + 1) * P] = (u[lo:lo + P, g * P:(g + 1) * P] * s * _silu(zg)).astype(BF16)
        tick()


def _swa_branch(sink_of, q, z, kv_cur, y_ref, kvprev_ref, tick, has_prev, *, tt):
    P, HD, KV = SW_BLOCK, SW_HD, SW_KV_HEADS
    G = SW_HEADS // KV
    kj = lax.broadcasted_iota(jnp.int32, (2 * P, P), 0)
    qi = lax.broadcasted_iota(jnp.int32, (2 * P, P), 1)
    dist = qi + P - kj
    window = jnp.logical_and(dist >= 0, dist < P)
    valid_mid = jnp.concatenate([window] * G, axis=1)
    valid_first = jnp.concatenate([jnp.logical_and(window, jnp.logical_or(kj >= P, has_prev))] * G, axis=1)
    kv = jnp.concatenate([kvprev_ref[...], kv_cur], axis=0)
    kvprev_ref[...] = kv_cur[tt - P:, :]
    qb = (q * (HD ** -0.5)).astype(BF16)
    for kh in range(KV):
        sink = jnp.concatenate([jnp.full((1, P), sink_of(kh * G + g), F32) for g in range(G)], axis=1)
        k_all = kv[:, kh * HD:(kh + 1) * HD].astype(BF16)
        v_all_t = kv[:, (KV + kh) * HD:(KV + kh + 1) * HD].T.astype(BF16)
        for j in range(tt // P):
            lo = j * P
            qs = jnp.concatenate(
                [qb[lo:lo + P, (kh * G + g) * HD:(kh * G + g + 1) * HD] for g in range(G)], axis=0)
            s = jnp.where(valid_first if j == 0 else valid_mid, _dot_nt(k_all[lo:lo + 2 * P], qs), NEG_INF)
            mx = jnp.maximum(jnp.max(s, axis=0, keepdims=True), sink)
            e = jnp.exp(s - mx)
            den = jnp.sum(e, axis=0, keepdims=True) + jnp.exp(sink - mx)
            o_t = _dot(v_all_t[:, lo:lo + 2 * P], e.astype(BF16)) * (1.0 / den)
            for g2 in range(G // 2):
                h0 = kh * G + 2 * g2
                pair = jnp.concatenate(
                    [o_t[:, 2 * g2 * P:(2 * g2 + 1) * P], o_t[:, (2 * g2 + 1) * P:(2 * g2 + 2) * P]], axis=0).T
                zp = z[lo:lo + P, h0 * HD:(h0 + 2) * HD]
                y_ref[lo:lo + P, h0 * HD:(h0 + 2) * HD] = (pair * _silu(zp)).astype(BF16)
            tick()


def _memattn_branch(q, z, kv_ref, y_ref, tick):
    HD = XM_HD
    qb = q.astype(BF16)
    for h in range(XM_HEADS):
        mk = kv_ref[:, h * HD:(h + 1) * HD]
        mv = kv_ref[:, BRANCH_W + h * HD:BRANCH_W + (h + 1) * HD]
        s = _dot_nt(qb[:, h * HD:(h + 1) * HD], mk) * (HD ** -0.5)
        e = jnp.exp(s - jnp.max(s, axis=-1, keepdims=True))
        o = _dot(e.astype(BF16), mv) * (1.0 / jnp.sum(e, axis=-1, keepdims=True))
        y_ref[:, h * HD:(h + 1) * HD] = (o * _silu(z[:, h * HD:(h + 1) * HD])).astype(BF16)
        tick()


def _layer_kernel(sinks_ref, x_ref, npre_ref, w_ref, cw_ref, gp_ref, on_ref, gmn_ref, ws_ref, bias_ref,
                  mkv_ref, wup_ref, wout_ref, npost_ref, o_ref, tail_ref, s_ref, kvprev_ref, y_ref, cols_ref,
                  merged_ref, *, tt, layer):
    @pl.when(pl.program_id(1) == 0)
    def _():
        tail_ref[...] = jnp.zeros_like(tail_ref)
        s_ref[...] = jnp.zeros_like(s_ref)
        kvprev_ref[...] = jnp.zeros_like(kvprev_ref)

    has_prev = pl.program_id(1) > 0
    tiles = [_layer_tile(sinks_ref, x_ref.at[g], npre_ref, w_ref, cw_ref, gp_ref, on_ref, gmn_ref, ws_ref, bias_ref,
                         mkv_ref.at[g], wup_ref, wout_ref, npost_ref, o_ref.at[g], tail_ref.at[g], s_ref.at[g],
                         kvprev_ref.at[g], y_ref.at[g], cols_ref.at[g], merged_ref.at[g], has_prev, tt=tt, layer=layer,
                         body_shift=GROUP_BODY_SHIFT * g)
             for g in range(ROW_GROUPS)]
    alive = [True] * ROW_GROUPS
    step = 0
    while any(alive):
        for g, tile in enumerate(tiles):
            if alive[g] and step >= g * ROW_SKEW:
                alive[g] = next(tile, None) is not None
        step += 1


def _layer_tile(sinks_ref, x_ref, npre_ref, w_ref, cw_ref, gp_ref, on_ref, gmn_ref, ws_ref, bias_ref,
                mkv_ref, wup_ref, wout_ref, npost_ref, o_ref, tail_ref, s_ref, kvprev_ref, y_ref, cols_ref,
                merged_ref, has_prev, *, tt, layer, body_shift=0):
    x = x_ref[...]
    h = _rms(x, npre_ref[0:1, :]).astype(BF16)

    def proj(lo, width):
        return _dot(h, w_ref[:, lo:lo + width])

    filled = [OFF_BUV]
    merged_started = set()

    def proj_item(lo):
        def run():
            cols_ref[:, lo - OFF_BUV:lo - OFF_BUV + PROJ_TILE] = proj(lo, PROJ_TILE)
            filled[0] = lo + PROJ_TILE
        return run

    def col(lo, width):
        assert lo + width <= filled[0], "projection read before it was issued"
        return cols_ref[:, lo - OFF_BUV:lo - OFF_BUV + width]

    def merge_item(n, lo):
        def run():
            g = _sigmoid(col(OFF_GATES + n * D_MODEL + lo, PROJ_TILE))
            term = g * _dot(y_ref[n], wup_ref[n, :, lo:lo + PROJ_TILE])
            if lo in merged_started:
                merged_ref[:, lo:lo + PROJ_TILE] += term
            else:
                merged_ref[:, lo:lo + PROJ_TILE] = term
                merged_started.add(lo)
        return run

    queue = collections.deque(proj_item(lo) for lo in range(OFF_BUV, W_COLS, PROJ_TILE))

    def tick(count=1):
        for _ in range(count):
            if queue:
                queue.popleft()()

    def branch_done(n):
        queue.extend(merge_item(n, lo) for lo in range(0, D_MODEL, PROJ_TILE))

    def branch_b():
        _gmlp_branch(col(OFF_BUV, 2 * BRANCH_W), col(OFF_BZ, BRANCH_W), gmn_ref, ws_ref, bias_ref,
                     y_ref.at[1], tick, tt=tt)
        branch_done(1)

    def branch_c():
        _swa_branch(lambda hd: sinks_ref[layer, hd], col(OFF_CQ, BRANCH_W), col(OFF_CZ, BRANCH_W),
                    col(OFF_CKV, CKV_COLS), y_ref.at[2], kvprev_ref, tick, has_prev, tt=tt)
        branch_done(2)

    def branch_m():
        _memattn_branch(col(OFF_MQ, BRANCH_W), col(OFF_MZ, BRANCH_W), mkv_ref, y_ref.at[3], tick)
        branch_done(3)

    bodies = {BODY_AT[0] + body_shift: branch_b, BODY_AT[1] + body_shift: branch_c, BODY_AT[2] + body_shift: branch_m}
    stages = _deltanet_stages(proj(OFF_AQKV, 3 * BRANCH_W), proj(OFF_AZ, BRANCH_W), proj(OFF_SMALL, LANES),
                              cw_ref, gp_ref, on_ref, y_ref.at[0], tail_ref, s_ref, tt=tt)
    yield True
    for i, _ in enumerate(stages):
        tick(ITEMS_AT.get(i, 1))
        if i in bodies:
            bodies.pop(i)()
        yield True
    assert not bodies, "stage list shorter than BODY_AT"
    branch_done(0)
    while queue:
        tick()
    yield True

    out = _dot(merged_ref[...].astype(BF16), wout_ref[...])
    o_ref[...] = x + _rms(out, npost_ref[0:1, :])


def _layer_call(x2, layer, sinks, norm_pre, w_pad, conv_w, gate_params, o_norm, gm_norm, spatial_w, bias_mat,
                mem_kv, w_up, w_out, norm_post, *, batch, seq, tt):
    nt = seq // tt
    rows_per_group = batch // ROW_GROUPS
    ml = mem_kv.shape[2]
    x3 = x2.reshape(ROW_GROUPS, rows_per_group * seq, D_MODEL)
    mem_kv = mem_kv.reshape(mem_kv.shape[0], ROW_GROUPS, rows_per_group, ml, 2 * BRANCH_W)
    row = lambda b, t, s: (0, b * nt + t, 0)

    def const(arr):
        shape = arr.shape[1:]
        return pl.BlockSpec((None,) + shape, lambda b, t, s: (layer,) + (0,) * len(shape),
                            pipeline_mode=pl.Buffered(1))

    in_specs = [
        pl.BlockSpec((ROW_GROUPS, tt, D_MODEL), row),
        const(norm_pre), const(w_pad), const(conv_w), const(gate_params), const(o_norm), const(gm_norm),
        const(spatial_w), const(bias_mat),
        pl.BlockSpec((None, ROW_GROUPS, None, ml, 2 * BRANCH_W), lambda b, t, s: (layer, 0, b, 0, 0)),
        const(w_up), const(w_out), const(norm_post),
    ]
    out = pl.pallas_call(
        functools.partial(_layer_kernel, tt=tt, layer=layer),
        grid_spec=pltpu.PrefetchScalarGridSpec(
            num_scalar_prefetch=1,
            grid=(rows_per_group, nt),
            in_specs=in_specs,
            out_specs=pl.BlockSpec((ROW_GROUPS, tt, D_MODEL), row),
            scratch_shapes=[
                pltpu.VMEM((ROW_GROUPS, 3, SUBLANES, BRANCH_W), F32),
                pltpu.VMEM((ROW_GROUPS, DN_DK, DN_HEADS * DN_DK), F32),
                pltpu.VMEM((ROW_GROUPS, SW_BLOCK, CKV_COLS), F32),
                pltpu.VMEM((ROW_GROUPS, N_BRANCH, tt, BRANCH_W), BF16),
                pltpu.VMEM((ROW_GROUPS, tt, W_COLS - OFF_BUV), F32),
                pltpu.VMEM((ROW_GROUPS, tt, D_MODEL), F32),
            ],
        ),
        out_shape=jax.ShapeDtypeStruct(x3.shape, F32),
        compiler_params=pltpu.CompilerParams(
            dimension_semantics=("arbitrary", "arbitrary"), vmem_limit_bytes=VMEM_LIMIT),
        name="layer",
    )(sinks, x3, norm_pre, w_pad, conv_w, gate_params, o_norm, gm_norm, spatial_w, bias_mat, mem_kv,
      w_up, w_out, norm_post)
    return out.reshape(batch * seq, D_MODEL)


def kernel(x, mem, norm_pre, norm_post, norm_mem, w_in, conv_w, a_log, dt_bias, dn_norm, gm_norm,
           spatial_w, spatial_b, sinks, w_mem_kv, w_up, w_out):
    batch, seq, d = x.shape
    nl = w_in.shape[0]
    assert d == D_MODEL and seq % TT_LAYER == 0 and batch % ROW_GROUPS == 0
    assert TT_LAYER % SW_BLOCK == 0 and TT_LAYER % GM_CHUNK == 0 and TT_LAYER % DN_CHUNK == 0
    assert w_in.shape[1:] == (D_MODEL, W_COLS - PAD_ROWS) and mem.shape[0] == batch and mem.shape[2] == D_MODEL
    w_pad = _pad_w_in(jnp.swapaxes(w_in, 1, 2))
    lane_pad = ((0, 0), (DN_HEADS, LANES - 2 * DN_HEADS))
    gate_params = jnp.stack([jnp.pad(-jnp.exp(a_log), lane_pad), jnp.pad(dt_bias, lane_pad)], axis=1)
    gate_params = jnp.pad(gate_params, ((0, 0), (0, SMALL_ROWS - 2), (0, 0)))
    o_norm = jnp.broadcast_to(dn_norm[:, None, :], (nl, SMALL_ROWS, DN_DK))
    period_rows = lambda width: SMALL_ROWS * LANES // width
    npre = jnp.broadcast_to(norm_pre[:, None, :], (nl, period_rows(d), d))
    npost = jnp.broadcast_to(norm_post[:, None, :], (nl, period_rows(d), d))
    gmn = jnp.broadcast_to(gm_norm[:, None, :], (nl, period_rows(BRANCH_W), BRANCH_W))
    conv_p = jnp.pad(conv_w, ((0, 0), (0, SUBLANES - CONV_W), (0, 0)))
    bias_mat = jnp.repeat(jnp.swapaxes(spatial_b, 1, 2), GM_CHUNK, axis=2)
    mem_kv = _memkv(mem, norm_mem[:, None, :], w_mem_kv)
    w_up_b = w_up.astype(BF16)
    w_out_b = w_out.astype(BF16)

    x2 = x.reshape(batch * seq, d)
    for l in range(nl):
        x2 = _layer_call(x2, l, sinks, npre, w_pad, conv_p, gate_params, o_norm, gmn, spatial_w, bias_mat, mem_kv,
                         w_up_b, w_out_b, npost, batch=batch, seq=seq, tt=TT_LAYER)
    return x2.reshape(batch, seq, d)
```
